```python
import jax
import jax.numpy as jnp
from jax import lax
import numpy as np

D_MODEL = 4096
BATCH = 2
SEQ = 8192
DEPTH = 4

CTX_LEN = 256
GRID_W = 64
HEAD_DIM = 128
N_HEAD_SLOTS = D_MODEL // HEAD_DIM
A_Q_HEADS = 3 * N_HEAD_SLOTS // 8
A_KV_HEADS = A_Q_HEADS // 3
B_HEADS = N_HEAD_SLOTS // 4
C_Q_HEADS = N_HEAD_SLOTS - A_Q_HEADS - B_HEADS
C_KV_HEADS = C_Q_HEADS // 3
MIX_WIDTH = N_HEAD_SLOTS * HEAD_DIM
SPLIT_SIZES = (A_Q_HEADS * HEAD_DIM, A_KV_HEADS * HEAD_DIM, A_KV_HEADS * HEAD_DIM,
               B_HEADS * HEAD_DIM, B_HEADS * HEAD_DIM, B_HEADS * HEAD_DIM,
               C_Q_HEADS * HEAD_DIM, C_KV_HEADS * HEAD_DIM, C_KV_HEADS * HEAD_DIM,
               MIX_WIDTH)
PROJ_WIDTH = sum(SPLIT_SIZES)
BLOCK = 128
WINDOW = 128
RET_CHUNK = 128
ROPE_BASE = 10000.0
EPS = 1e-6

kernel_name = 'hybrid_parallel_heads_dit_block'


def rmsnorm(x, g):
    x32 = x.astype(jnp.float32)
    y = x32 * lax.rsqrt(jnp.mean(x32 * x32, axis=-1, keepdims=True) + EPS)
    return (y * g.astype(jnp.float32)).astype(x.dtype)


def head_groupnorm(o):
    o32 = o.astype(jnp.float32)
    mu = jnp.mean(o32, axis=-1, keepdims=True)
    var = jnp.mean(jnp.square(o32 - mu), axis=-1, keepdims=True)
    y = (o32 - mu) * lax.rsqrt(var + EPS)
    return y.reshape(o.shape[0], o.shape[1], -1).astype(o.dtype)


def axial_rope_tables(n_tokens):
    rows = n_tokens // GRID_W
    row = jnp.repeat(jnp.arange(rows, dtype=jnp.float32), GRID_W)
    col = jnp.tile(jnp.arange(GRID_W, dtype=jnp.float32), rows)
    axis_dim = HEAD_DIM // 2
    inv = ROPE_BASE ** (-jnp.arange(0, axis_dim, 2, dtype=jnp.float32) / axis_dim)
    ang_r = row[:, None] * inv
    ang_c = col[:, None] * inv
    ang = jnp.concatenate([ang_r, ang_r, ang_c, ang_c], axis=-1)
    return jnp.cos(ang), jnp.sin(ang)


def apply_rope(x, cos, sin):
    x1, x2, x3, x4 = jnp.split(x, 4, axis=-1)
    rot = jnp.concatenate([-x2, x1, -x4, x3], axis=-1)
    return (x * cos[:, None, :] + rot * sin[:, None, :]).astype(x.dtype)


def split_projection(p):
    bsz, t = p.shape[0], p.shape[1]
    idx = np.cumsum(SPLIT_SIZES)[:-1].tolist()
    parts = jnp.split(p, idx, axis=-1)
    heads = [u.reshape(bsz, t, -1, HEAD_DIM) for u in parts[:-1]]
    return heads, parts[-1]


def attend(q, k, v, sink=None):
    s = jnp.einsum('btkgd,bjkd->bkgtj', q, k).astype(jnp.float32) * (HEAD_DIM ** -0.5)
    if sink is not None:
        col = jnp.broadcast_to(sink.astype(jnp.float32)[None, :, :, None, None], s.shape[:-1] + (1,))
        s = jnp.concatenate([s, col], axis=-1)
    p = jax.nn.softmax(s, axis=-1)
    if sink is not None:
        p = p[..., :-1]
    return jnp.einsum('bkgtj,bjkd->btkgd', p.astype(v.dtype), v)


def windowed_attention(q, k, v, k_ctx, v_ctx, sink):
    bsz, n = q.shape[0], q.shape[1]
    nb = n // BLOCK
    qb = q.reshape(bsz, nb, BLOCK, *q.shape[2:])

    def band(t):
        tp = jnp.pad(t, ((0, 0), (BLOCK, BLOCK), (0, 0), (0, 0)))
        tp = tp.reshape(bsz, nb + 2, BLOCK, *t.shape[2:])
        return jnp.concatenate([tp[:, :-2], tp[:, 1:-1], tp[:, 2:]], axis=2)

    kw, vw = band(k), band(v)
    scale = HEAD_DIM ** -0.5
    s_loc = jnp.einsum('bnqkgd,bnjkd->bnkgqj', qb, kw).astype(jnp.float32) * scale
    s_ctx = jnp.einsum('bnqkgd,bjkd->bnkgqj', qb, k_ctx).astype(jnp.float32) * scale
    i = jnp.arange(BLOCK)[:, None]
    j = jnp.arange(3 * BLOCK)[None, :]
    rel = j - BLOCK - i
    kpos = jnp.arange(nb)[:, None, None] * BLOCK - BLOCK + j[None]
    valid = (jnp.abs(rel) <= WINDOW)[None] & (kpos >= 0) & (kpos < n)
    s_loc = jnp.where(valid[None, :, None, None], s_loc, -jnp.inf)
    col = jnp.broadcast_to(sink.astype(jnp.float32)[None, None, :, :, None, None], s_ctx.shape[:-1] + (1,))
    p = jax.nn.softmax(jnp.concatenate([s_ctx, s_loc, col], axis=-1), axis=-1)
    n_ctx = k_ctx.shape[1]
    p_ctx = p[..., :n_ctx].astype(v.dtype)
    p_loc = p[..., n_ctx:n_ctx + 3 * BLOCK].astype(v.dtype)
    o = (jnp.einsum('bnkgqj,bjkd->bnqkgd', p_ctx, v_ctx)
         + jnp.einsum('bnkgqj,bnjkd->bnqkgd', p_loc, vw))
    return o.reshape(bsz, n, -1)


def dense_block_attention(q, k_all, v_all):
    bsz, n = q.shape[0], q.shape[1]
    nb = n // BLOCK
    qb = jnp.moveaxis(q.reshape(bsz, nb, BLOCK, *q.shape[2:]), 1, 0)
    o = lax.map(lambda qi: attend(qi, k_all, v_all), qb)
    return jnp.moveaxis(o, 0, 1).reshape(bsz, n, -1)


def retention_chunkwise(q, k, v, log_g, s0):
    bsz, t, h, dk = q.shape
    dv = v.shape[-1]
    nc = t // RET_CHUNK
    qc = q.reshape(bsz, nc, RET_CHUNK, h, dk)
    kc = k.reshape(bsz, nc, RET_CHUNK, h, dk)
    vc = v.reshape(bsz, nc, RET_CHUNK, h, dv)
    pos = jnp.arange(RET_CHUNK, dtype=jnp.float32)
    diff = pos[:, None] - pos[None, :]
    expo = jnp.where((diff >= 0)[None], diff[None] * log_g[:, None, None], -jnp.inf)
    dmat = jnp.exp(expo).astype(q.dtype)
    intra = jnp.einsum('bcnhd,bcmhd->bchnm', qc, kc) * dmat
    intra = jnp.einsum('bchnm,bcmhe->bcnhe', intra, vc)
    wk = jnp.exp((RET_CHUNK - 1 - pos)[:, None] * log_g[None]).astype(k.dtype)
    kv = jnp.einsum('bcmhd,bcmhe->cbhde', kc * wk[:, :, None], vc)
    g_chunk = jnp.exp(RET_CHUNK * log_g).astype(kv.dtype)[:, None, None]

    def step(s, kv_i):
        return g_chunk * s + kv_i, s

    _, s_prev = lax.scan(step, s0.astype(kv.dtype), kv)
    wq = jnp.exp((pos + 1.0)[:, None] * log_g[None]).astype(q.dtype)
    cross = jnp.einsum('bcnhd,cbhde->bcnhe', qc * wq[:, :, None], s_prev.astype(q.dtype))
    return (intra + cross).reshape(bsz, t, h, dv)


def retention_state(k, v, log_g):
    t = k.shape[1]
    pos = jnp.arange(t, dtype=jnp.float32)
    w = jnp.exp((t - 1 - pos)[:, None] * log_g[None]).astype(k.dtype)
    return jnp.einsum('bmhd,bmhe->bhde', k * w[:, :, None], v)


def retention_bidir(q, k, v, q_ctx, k_ctx, v_ctx, log_g, need_ctx):
    flip = lambda t: jnp.flip(t, axis=1)
    s_fwd = retention_state(k_ctx, v_ctx, log_g[0])
    s_bwd = retention_state(flip(k_ctx), flip(v_ctx), log_g[1])
    o = (retention_chunkwise(q, k, v, log_g[0], s_fwd)
         + flip(retention_chunkwise(flip(q), flip(k), flip(v), log_g[1], s_bwd)))
    o_ctx = None
    if need_ctx:
        zero = jnp.zeros_like(s_fwd)
        o_ctx = head_groupnorm(
            retention_chunkwise(q_ctx, k_ctx, v_ctx, log_g[0], zero)
            + flip(retention_chunkwise(flip(q_ctx), flip(k_ctx), flip(v_ctx), log_g[1], zero)))
    return head_groupnorm(o), o_ctx


def setup_inputs(seed: int = 0) -> dict:
    key = jax.random.key(seed)
    ks = jax.random.split(key, 16)
    f32 = jnp.float32
    x = jax.random.normal(ks[0], (BATCH, SEQ, D_MODEL), f32)
    c = jax.random.normal(ks[1], (BATCH, D_MODEL), f32)
    ctx = jax.random.normal(ks[2], (BATCH, CTX_LEN, D_MODEL), f32)
    c_ctx = jax.random.normal(ks[3], (D_MODEL,), f32)
    w_ada = jax.random.normal(ks[4], (DEPTH, D_MODEL, 3 * D_MODEL), f32) * (0.5 * D_MODEL ** -0.5)
    b_ada = jax.random.normal(ks[5], (DEPTH, 3 * D_MODEL), f32) * 0.02
    g_pre = 1.0 + 0.02 * jax.random.normal(ks[6], (DEPTH, D_MODEL), f32)
    g_post = 1.0 + 0.02 * jax.random.normal(ks[7], (DEPTH, D_MODEL), f32)
    w_in = jax.random.normal(ks[8], (DEPTH, D_MODEL, PROJ_WIDTH), f32) * (D_MODEL ** -0.5)
    w_out = jax.random.normal(ks[9], (DEPTH, MIX_WIDTH, D_MODEL), f32) * (MIX_WIDTH ** -0.5)
    sink_a = jax.random.normal(ks[10], (DEPTH, A_Q_HEADS), f32) * 0.5
    qnorm_c = 1.0 + 0.02 * jax.random.normal(ks[11], (DEPTH, HEAD_DIM), f32)
    knorm_c = 1.0 + 0.02 * jax.random.normal(ks[12], (DEPTH, HEAD_DIM), f32)
    base = jnp.log(2.0 ** (5.0 + jnp.arange(B_HEADS, dtype=f32)) - 1.0)
    ret_decay = base[None, None, :] + 0.05 * jax.random.normal(ks[13], (DEPTH, 2, B_HEADS), f32)
    return {'x': x, 'c': c, 'ctx': ctx, 'c_ctx': c_ctx, 'w_ada': w_ada, 'b_ada': b_ada,
            'g_pre': g_pre, 'g_post': g_post, 'w_in': w_in, 'w_out': w_out, 'sink_a': sink_a,
            'qnorm_c': qnorm_c, 'knorm_c': knorm_c, 'ret_decay': ret_decay}


def reference(x, c, ctx, c_ctx, w_ada, b_ada, g_pre, g_post, w_in, w_out, sink_a, qnorm_c, knorm_c, ret_decay):
    n_lat = x.shape[1]
    cos, sin = axial_rope_tables(n_lat)
    rope = lambda t: apply_rope(t, cos, sin)
    gq = lambda t, n_kv: t.reshape(t.shape[0], t.shape[1], n_kv, -1, HEAD_DIM)
    flat = lambda t: t.reshape(t.shape[0], t.shape[1], -1)
    cond_lat = jax.nn.silu(c)
    cond_ctx = jax.nn.silu(c_ctx)
    k_scale = HEAD_DIM ** -0.5
    xc = ctx
    for layer in range(DEPTH):
        need_ctx = layer < DEPTH - 1
        shift, scale, gate = jnp.split(cond_lat @ w_ada[layer] + b_ada[layer], 3, axis=-1)
        shift_c, scale_c, gate_c = jnp.split(cond_ctx @ w_ada[layer] + b_ada[layer], 3, axis=-1)
        h = rmsnorm(x, g_pre[layer]) * (1.0 + scale[:, None]) + shift[:, None]
        hc = rmsnorm(xc, g_pre[layer]) * (1.0 + scale_c) + shift_c
        (qa, ka, va, qb, kb, vb, qc, kc, vc), z = split_projection(h @ w_in[layer])
        (qa_x, ka_x, va_x, qb_x, kb_x, vb_x, qc_x, kc_x, vc_x), z_x = split_projection(hc @ w_in[layer])
        sink = sink_a[layer].reshape(A_KV_HEADS, -1)
        oa = windowed_attention(gq(rope(qa), A_KV_HEADS), rope(ka), va, ka_x, va_x, sink)
        log_g = jax.nn.log_sigmoid(ret_decay[layer].astype(jnp.float32))
        ob, ob_x = retention_bidir(rope(qb), rope(kb) * k_scale, vb,
                                   qb_x, kb_x * k_scale, vb_x, log_g, need_ctx)
        qc_n = rope(rmsnorm(qc, qnorm_c[layer]))
        kc_n = rope(rmsnorm(kc, knorm_c[layer]))
        kc_xn = rmsnorm(kc_x, knorm_c[layer])
        k_all = jnp.concatenate([kc_xn, kc_n], axis=1)
        v_all = jnp.concatenate([vc_x, vc], axis=1)
        oc = dense_block_attention(gq(qc_n, C_KV_HEADS), k_all, v_all)
        y = (jnp.concatenate([oa, ob, oc], axis=-1) * jax.nn.silu(z)) @ w_out[layer]
        x = x + gate[:, None] * rmsnorm(y, g_post[layer])
        if need_ctx:
            oa_x = attend(gq(qa_x, A_KV_HEADS), ka_x, va_x, sink)
            qc_xn = rmsnorm(qc_x, qnorm_c[layer])
            oc_x = attend(gq(qc_xn, C_KV_HEADS), kc_xn, vc_x)
            y_x = (jnp.concatenate([flat(oa_x), ob_x, flat(oc_x)], axis=-1) * jax.nn.silu(z_x)) @ w_out[layer]
            xc = xc + gate_c * rmsnorm(y_x, g_post[layer])
    return x
```

```python
import functools
import math

import jax
import jax.numpy as jnp
from jax import lax
from jax.experimental import pallas as pl
from jax.experimental.pallas import tpu as pltpu

HEAD_DIM = 128
GRID_W = 64
WINDOW = 128
RET_CHUNK = 128
ROPE_BASE = 10000.0
EPS = 1e-6
LOG2E = math.log2(math.e)
GROUP = 3
VMEM_LIMIT = 56 * 1024 * 1024

F32 = jnp.float32
BF16 = jnp.bfloat16
NT_DIMS = (((1,), (1,)), ((), ()))
TN_DIMS = (((0,), (0,)), ((), ()))


def _params(sem):
    return pltpu.CompilerParams(dimension_semantics=sem, vmem_limit_bytes=VMEM_LIMIT)


def _silu(z):
    return z / (1.0 + jnp.exp(-z))


def _stack_heads(q):
    return jnp.concatenate([q[:, g * HEAD_DIM:(g + 1) * HEAD_DIM] for g in range(GROUP)], axis=0)


def _unstack_heads(o, rows):
    return jnp.concatenate([o[g * rows:(g + 1) * rows, :] for g in range(GROUP)], axis=1)


def _ada_kernel(c_ref, w_ref, b_ref, o_ref):
    cond = _silu(c_ref[...])
    o_ref[...] = jnp.dot(cond.astype(BF16), w_ref[...].astype(BF16),
                         preferred_element_type=F32) + b_ref[...]


def _ada(cond_rows, w_ada, b_ada):
    depth, d, n3 = w_ada.shape
    tn = 512
    return pl.pallas_call(
        _ada_kernel,
        grid=(depth, n3 // tn),
        in_specs=[
            pl.BlockSpec((8, d), lambda l, j: (0, 0)),
            pl.BlockSpec((None, d, tn), lambda l, j: (l, 0, j)),
            pl.BlockSpec((None, 1, tn), lambda l, j: (l, 0, j)),
        ],
        out_specs=pl.BlockSpec((None, 8, tn), lambda l, j: (l, 0, j)),
        out_shape=jax.ShapeDtypeStruct((depth, 8, n3), F32),
        compiler_params=_params(("arbitrary", "arbitrary")),
        name="ada",
    )(cond_rows, w_ada, b_ada.reshape(depth, 1, n3))


def _inproj_kernel(x_ref, g_ref, ml_ref, mc_ref, w_ref, o_ref, h_ref, *, tm, rc, ctx_len, d):
    i = pl.program_id(1)
    j = pl.program_id(2)

    @pl.when(j == 0)
    def _():
        g = g_ref[...]
        sh_l = ml_ref[:, 0:d]
        sc_l = 1.0 + ml_ref[:, d:2 * d]
        sh_c = mc_ref[:, 0:d]
        sc_c = 1.0 + mc_ref[:, d:2 * d]

        def body(c, carry):
            r0 = pl.multiple_of(c * rc, rc)
            is_ctx = jnp.logical_and(i == 0, r0 < ctx_len)
            sh = jnp.where(is_ctx, sh_c, sh_l)
            sc = jnp.where(is_ctx, sc_c, sc_l)
            x = x_ref[pl.ds(r0, rc), :]
            ms = jnp.mean(x * x, axis=-1, keepdims=True)
            y = x * lax.rsqrt(ms + EPS) * g
            h_ref[pl.ds(r0, rc), :] = (y * sc + sh).astype(BF16)
            return carry

        lax.fori_loop(0, tm // rc, body, 0)

    o_ref[...] = jnp.dot(h_ref[...], w_ref[...], preferred_element_type=F32).astype(o_ref.dtype)


def _inproj(xs, g_pre, mod_lat, mod_ctx, w, layer, *, ctx_len):
    b, t, d = xs.shape
    n = w.shape[-1]
    tm, tn, rc = 768, 512, 32
    assert t % tm == 0 and n % tn == 0 and ctx_len % rc == 0 and ctx_len <= tm
    return pl.pallas_call(
        functools.partial(_inproj_kernel, tm=tm, rc=rc, ctx_len=ctx_len, d=d),
        grid=(b, t // tm, n // tn),
        in_specs=[
            pl.BlockSpec((None, tm, d), lambda bb, i, j: (bb, i, 0)),
            pl.BlockSpec((1, d), lambda bb, i, j: (0, 0)),
            pl.BlockSpec((None, 1, 3 * d), lambda bb, i, j: (bb, 0, 0)),
            pl.BlockSpec((1, 3 * d), lambda bb, i, j: (0, 0)),
            pl.BlockSpec((None, d, tn), lambda bb, i, j: (layer, 0, j)),
        ],
        out_specs=pl.BlockSpec((None, tm, tn), lambda bb, i, j: (bb, i, j)),
        out_shape=jax.ShapeDtypeStruct((b, t, n), BF16),
        scratch_shapes=[pltpu.VMEM((tm, d), BF16)],
        compiler_params=_params(("arbitrary", "arbitrary", "arbitrary")),
        name="inproj",
    )(xs, g_pre, mod_lat, mod_ctx, w)


def _prep_kernel(p_ref, cos_ref, sin_ref, qn_ref, kn_ref, o_ref, *, kinds, tp):
    cos = cos_ref[...]
    sin = sin_ref[...]
    lane = lax.broadcasted_iota(jnp.int32, (tp, HEAD_DIM), 1)
    odd = (lane & 32) != 0
    qn = qn_ref[...]
    kn = kn_ref[...]

    def rope(x):
        r32 = pltpu.roll(x, 32, 1)
        r96 = pltpu.roll(x, 96, 1)
        return x * cos + jnp.where(odd, r32, r96) * sin

    def rms(x, g):
        return x * lax.rsqrt(jnp.mean(x * x, axis=-1, keepdims=True) + EPS) * g

    for t, (norm, scale) in enumerate(kinds):
        sl = slice(t * HEAD_DIM, (t + 1) * HEAD_DIM)
        x = p_ref[:, sl].astype(F32)
        if norm == "q":
            x = rms(x, qn)
        elif norm == "k":
            x = rms(x, kn)
        x = rope(x)
        if scale != 1.0:
            x = x * scale
        o_ref[:, sl] = x.astype(o_ref.dtype)


def _prep(p, cos, sin_s, qn, kn, *, kinds):
    b, t, n = p.shape
    tp = 256
    npre = len(kinds) * HEAD_DIM
    assert t % tp == 0
    return pl.pallas_call(
        functools.partial(_prep_kernel, kinds=kinds, tp=tp),
        grid=(b, t // tp),
        in_specs=[
            pl.BlockSpec((None, tp, npre), lambda bb, i: (bb, i, 0)),
            pl.BlockSpec((tp, HEAD_DIM), lambda bb, i: (i, 0)),
            pl.BlockSpec((tp, HEAD_DIM), lambda bb, i: (i, 0)),
            pl.BlockSpec((1, HEAD_DIM), lambda bb, i: (0, 0)),
            pl.BlockSpec((1, HEAD_DIM), lambda bb, i: (0, 0)),
        ],
        out_specs=pl.BlockSpec((None, tp, npre), lambda bb, i: (bb, i, 0)),
        out_shape=jax.ShapeDtypeStruct(p.shape, p.dtype),
        input_output_aliases={0: 0},
        compiler_params=_params(("arbitrary", "arbitrary")),
        name="prep",
    )(p, cos, sin_s, qn, kn)


def _mixa_kernel(sink_ref, q_ref, z_ref, k_ref, v_ref, o_ref, *, tq, ctx_len, s_lat):
    kv = pl.program_id(1)
    n = pl.program_id(2)
    nctx = ctx_len // tq
    is_lat = n >= nctx
    q0 = (n - nctx) * tq
    wl = tq + 2 * WINDOW
    cs = pl.multiple_of(jnp.clip(q0 - WINDOW, 0, s_lat - wl), HEAD_DIM)

    q3 = _stack_heads(q_ref[...])
    k_ctx = k_ref[0:ctx_len, :]
    v_ctx = v_ref[0:ctx_len, :]
    k_loc = k_ref[pl.ds(ctx_len + cs, wl), :]
    v_loc = v_ref[pl.ds(ctx_len + cs, wl), :]
    s_ctx = lax.dot_general(q3, k_ctx, NT_DIMS, preferred_element_type=F32)
    s_loc = lax.dot_general(q3, k_loc, NT_DIMS, preferred_element_type=F32)

    rows = GROUP * tq
    row = lax.broadcasted_iota(jnp.int32, (rows, wl), 0)
    col = lax.broadcasted_iota(jnp.int32, (rows, wl), 1)
    rel = (cs + col) - (q0 + (row & (tq - 1)))
    valid = jnp.logical_and(jnp.abs(rel) <= WINDOW, is_lat)
    s_loc = jnp.where(valid, s_loc, -jnp.inf)

    sink = jnp.concatenate(
        [jnp.full((tq, 1), sink_ref[kv * GROUP + g], F32) for g in range(GROUP)], axis=0)
    m = jnp.maximum(jnp.maximum(jnp.max(s_ctx, axis=-1, keepdims=True),
                                jnp.max(s_loc, axis=-1, keepdims=True)), sink)
    p_ctx = jnp.exp2(s_ctx - m)
    p_loc = jnp.exp2(s_loc - m)
    den = (jnp.sum(p_ctx, axis=-1, keepdims=True) + jnp.sum(p_loc, axis=-1, keepdims=True)
           + jnp.exp2(sink - m))
    o = (jnp.dot(p_ctx.astype(BF16), v_ctx, preferred_element_type=F32)
         + jnp.dot(p_loc.astype(BF16), v_loc, preferred_element_type=F32))
    o = _unstack_heads(o / den, tq)
    o_ref[...] = (o * _silu(z_ref[...].astype(F32))).astype(o_ref.dtype)


def _mixa(sink2, p, *, lay, ctx_len):
    b, t, _ = p.shape
    tq = 256
    gw = GROUP * HEAD_DIM
    nkv = lay["akv"]
    s_lat = t - ctx_len
    assert t % tq == 0 and ctx_len % tq == 0 and tq & (tq - 1) == 0
    qoff, zoff, koff, voff = lay["qa"] // GROUP, lay["za"] // GROUP, lay["ka"], lay["va"]
    return pl.pallas_call(
        functools.partial(_mixa_kernel, tq=tq, ctx_len=ctx_len, s_lat=s_lat),
        grid=(b, nkv, t // tq),
        in_specs=[
            pl.BlockSpec(memory_space=pltpu.SMEM),
            pl.BlockSpec((None, tq, gw), lambda bb, k, i: (bb, i, qoff + k)),
            pl.BlockSpec((None, tq, gw), lambda bb, k, i: (bb, i, zoff + k)),
            pl.BlockSpec((None, t, HEAD_DIM), lambda bb, k, i: (bb, 0, koff + k)),
            pl.BlockSpec((None, t, HEAD_DIM), lambda bb, k, i: (bb, 0, voff + k)),
        ],
        out_specs=pl.BlockSpec((None, tq, gw), lambda bb, k, i: (bb, i, k)),
        out_shape=jax.ShapeDtypeStruct((b, t, lay["nh"] * HEAD_DIM), BF16),
        compiler_params=_params(("arbitrary", "arbitrary", "arbitrary")),
        name="mixa",
    )(sink2, p, p, p, p)


def _mixc_kernel(q_ref, z_ref, k_ref, v_ref, oin_ref, o_ref, m_ref, l_ref, acc_ref,
                 *, tq, tk, ctx_len, s_lat):
    del oin_ref
    qi = pl.program_id(2)
    q3 = _stack_heads(q_ref[...])
    m_ref[...] = jnp.full(m_ref.shape, -jnp.inf, F32)
    l_ref[...] = jnp.zeros(l_ref.shape, F32)
    acc_ref[...] = jnp.zeros(acc_ref.shape, F32)

    def process(kc, vc):
        s = lax.dot_general(q3, kc, NT_DIMS, preferred_element_type=F32)
        m_prev = m_ref[...]
        m_new = jnp.maximum(m_prev, jnp.max(s, axis=-1, keepdims=True))
        alpha = jnp.exp2(m_prev - m_new)
        p = jnp.exp2(s - m_new)
        l_ref[...] = alpha * l_ref[...] + jnp.sum(p, axis=-1, keepdims=True)
        acc_ref[...] = alpha * acc_ref[...] + jnp.dot(p.astype(BF16), vc, preferred_element_type=F32)
        m_ref[...] = m_new

    process(k_ref[0:ctx_len, :], v_ref[0:ctx_len, :])
    nlat = jnp.where(qi < ctx_len // tq, 0, s_lat // tk)

    def body(i, carry):
        off = pl.multiple_of(ctx_len + i * tk, HEAD_DIM)
        process(k_ref[pl.ds(off, tk), :], v_ref[pl.ds(off, tk), :])
        return carry

    lax.fori_loop(0, nlat, body, 0)
    o = _unstack_heads(acc_ref[...] / l_ref[...], tq)
    o_ref[...] = (o * _silu(z_ref[...].astype(F32))).astype(o_ref.dtype)


def _mixc(p, o_prev, *, lay, ctx_len):
    b, t, _ = p.shape
    tq = 256
    gw = GROUP * HEAD_DIM
    s_lat = t - ctx_len
    tk = 512 if s_lat % 512 == 0 else 256
    nkv = lay["ckv"]
    assert t % tq == 0 and ctx_len % tq == 0 and s_lat % tk == 0
    qoff, zoff, koff, voff = lay["qc"] // GROUP, lay["zc"] // GROUP, lay["kc"], lay["vc"]
    ooff = lay["oc"] // GROUP
    return pl.pallas_call(
        functools.partial(_mixc_kernel, tq=tq, tk=tk, ctx_len=ctx_len, s_lat=s_lat),
        grid=(b, nkv, t // tq),
        in_specs=[
            pl.BlockSpec((None, tq, gw), lambda bb, k, i: (bb, i, qoff + k)),
            pl.BlockSpec((None, tq, gw), lambda bb, k, i: (bb, i, zoff + k)),
            pl.BlockSpec((None, t, HEAD_DIM), lambda bb, k, i: (bb, 0, koff + k)),
            pl.BlockSpec((None, t, HEAD_DIM), lambda bb, k, i: (bb, 0, voff + k)),
            pl.BlockSpec(memory_space=pl.ANY),
        ],
        out_specs=pl.BlockSpec((None, tq, gw), lambda bb, k, i: (bb, i, ooff + k)),
        out_shape=jax.ShapeDtypeStruct(o_prev.shape, o_prev.dtype),
        input_output_aliases={4: 0},
        scratch_shapes=[pltpu.VMEM((GROUP * tq, 1), F32), pltpu.VMEM((GROUP * tq, 1), F32),
                        pltpu.VMEM((GROUP * tq, HEAD_DIM), F32)],
        compiler_params=_params(("arbitrary", "arbitrary", "arbitrary")),
        name="mixc",
    )(p, p, p, p, o_prev)


def _ret_state_kernel(gc_ref, k_ref, v_ref, wk_ref, o_ref, s_ref, *, nh):
    d = pl.program_id(1)
    step = pl.program_id(2)

    @pl.when(step == 0)
    def _():
        s_ref[...] = jnp.zeros(s_ref.shape, F32)

    wk = wk_ref[...]
    for it in range(2):
        ci = d if it == 0 else 1 - d
        r0 = pl.multiple_of(ci * RET_CHUNK, RET_CHUNK)
        k = (k_ref[pl.ds(r0, RET_CHUNK), :].astype(F32) * wk).astype(BF16)
        v = v_ref[pl.ds(r0, RET_CHUNK), :]
        o_ref[ci] = s_ref[...].astype(o_ref.dtype)
        for h in range(nh):
            hs = slice(h * HEAD_DIM, (h + 1) * HEAD_DIM)
            kv = lax.dot_general(k[:, hs], v[:, hs], TN_DIMS, preferred_element_type=F32)
            s_ref[hs, :] = gc_ref[d, h] * s_ref[hs, :] + kv


def _ret_states(gc, p, wk, *, lay):
    b, t, _ = p.shape
    nh = lay["bh"]
    bw = nh * HEAD_DIM
    tb = 2 * RET_CHUNK
    nblk = t // tb
    koff, voff = lay["kb"] // nh, lay["vb"] // nh
    assert t % tb == 0

    def blk(d, s):
        return jnp.where(d == 0, s, jnp.where(s == 0, 0, nblk - s))

    return pl.pallas_call(
        functools.partial(_ret_state_kernel, nh=nh),
        grid=(b, 2, nblk),
        in_specs=[
            pl.BlockSpec(memory_space=pltpu.SMEM),
            pl.BlockSpec((None, tb, bw), lambda bb, d, s: (bb, blk(d, s), koff)),
            pl.BlockSpec((None, tb, bw), lambda bb, d, s: (bb, blk(d, s), voff)),
            pl.BlockSpec((None, RET_CHUNK, bw), lambda bb, d, s: (d, 0, 0)),
        ],
        out_specs=pl.BlockSpec((None, None, 2, bw, HEAD_DIM),
                               lambda bb, d, s: (bb, d, blk(d, s), 0, 0)),
        out_shape=jax.ShapeDtypeStruct((b, 2, t // RET_CHUNK, bw, HEAD_DIM), BF16),
        scratch_shapes=[pltpu.VMEM((bw, HEAD_DIM), F32)],
        compiler_params=_params(("arbitrary", "arbitrary", "arbitrary")),
        name="ret_states",
    )(gc, p, p, wk)


def _ret_out_kernel(q_ref, k_ref, v_ref, z_ref, sp_ref, dm_ref, wq_ref, oin_ref, o_ref, *, nh):
    del oin_ref
    for c in range(2):
        rs = slice(c * RET_CHUNK, (c + 1) * RET_CHUNK)
        for h in range(nh):
            hs = slice(h * HEAD_DIM, (h + 1) * HEAD_DIM)
            q = q_ref[rs, hs]
            a = lax.dot_general(q, k_ref[rs, hs], NT_DIMS, preferred_element_type=F32) * dm_ref[h]
            q32 = q.astype(F32)
            lhs = jnp.concatenate([a.astype(BF16),
                                   (q32 * wq_ref[0, :, hs]).astype(BF16),
                                   (q32 * wq_ref[1, :, hs]).astype(BF16)], axis=1)
            rhs = jnp.concatenate([v_ref[rs, hs], sp_ref[0, c, hs, :], sp_ref[1, c, hs, :]], axis=0)
            o = jnp.dot(lhs, rhs, preferred_element_type=F32)
            mu = jnp.mean(o, axis=-1, keepdims=True)
            oc = o - mu
            var = jnp.mean(oc * oc, axis=-1, keepdims=True)
            y = oc * lax.rsqrt(var + EPS)
            o_ref[rs, hs] = (y * _silu(z_ref[rs, hs].astype(F32))).astype(o_ref.dtype)


def _ret_out(p, sprev, dmat, wq, o_prev, *, lay):
    b, t, _ = p.shape
    nh = lay["bh"]
    bw = nh * HEAD_DIM
    tb = 2 * RET_CHUNK
    qoff, koff, voff, zoff = lay["qb"] // nh, lay["kb"] // nh, lay["vb"] // nh, lay["zb"] // nh
    ooff = lay["ob"] // nh
    return pl.pallas_call(
        functools.partial(_ret_out_kernel, nh=nh),
        grid=(b, t // tb),
        in_specs=[
            pl.BlockSpec((None, tb, bw), lambda bb, i: (bb, i, qoff)),
            pl.BlockSpec((None, tb, bw), lambda bb, i: (bb, i, koff)),
            pl.BlockSpec((None, tb, bw), lambda bb, i: (bb, i, voff)),
            pl.BlockSpec((None, tb, bw), lambda bb, i: (bb, i, zoff)),
            pl.BlockSpec((None, 2, 2, bw, HEAD_DIM), lambda bb, i: (bb, 0, i, 0, 0)),
            pl.BlockSpec((nh, RET_CHUNK, RET_CHUNK), lambda bb, i: (0, 0, 0)),
            pl.BlockSpec((2, RET_CHUNK, bw), lambda bb, i: (0, 0, 0)),
            pl.BlockSpec(memory_space=pl.ANY),
        ],
        out_specs=pl.BlockSpec((None, tb, bw), lambda bb, i: (bb, i, ooff)),
        out_shape=jax.ShapeDtypeStruct(o_prev.shape, o_prev.dtype),
        input_output_aliases={7: 0},
        compiler_params=_params(("arbitrary", "arbitrary")),
        name="ret_out",
    )(p, p, p, p, sprev, dmat, wq, o_prev)


def _outproj_kernel(o_ref, w_ref, x_ref, gl_ref, gc_ref, gp_ref, y_ref, *, tm, rc, ctx_len, nk):
    i = pl.program_id(1)
    k = pl.program_id(2)
    part = jnp.dot(o_ref[...], w_ref[...], preferred_element_type=F32)

    @pl.when(k == 0)
    def _():
        y_ref[...] = part

    @pl.when(k > 0)
    def _():
        y_ref[...] += part

    @pl.when(k == nk - 1)
    def _():
        gp = gp_ref[...]
        gate_l = gl_ref[...]
        gate_c = gc_ref[...]

        def body(c, carry):
            r0 = pl.multiple_of(c * rc, rc)
            is_ctx = jnp.logical_and(i == 0, r0 < ctx_len)
            gate = jnp.where(is_ctx, gate_c, gate_l)
            y = y_ref[pl.ds(r0, rc), :]
            ms = jnp.mean(y * y, axis=-1, keepdims=True)
            yn = y * lax.rsqrt(ms + EPS) * gp
            y_ref[pl.ds(r0, rc), :] = x_ref[pl.ds(r0, rc), :] + gate * yn
            return carry

        lax.fori_loop(0, tm // rc, body, 0)


def _outproj(o, w, layer, xs, gate_lat, gate_ctx, g_post, *, ctx_len):
    b, t, d = xs.shape
    kdim = o.shape[-1]
    tm, tk, rc = 384, 512, 32
    nk = kdim // tk
    assert t % tm == 0 and kdim % tk == 0 and ctx_len % rc == 0 and ctx_len <= tm
    return pl.pallas_call(
        functools.partial(_outproj_kernel, tm=tm, rc=rc, ctx_len=ctx_len, nk=nk),
        grid=(b, t // tm, nk),
        in_specs=[
            pl.BlockSpec((None, tm, tk), lambda bb, i, k: (bb, i, k)),
            pl.BlockSpec((None, tk, d), lambda bb, i, k: (layer, k, 0)),
            pl.BlockSpec((None, tm, d), lambda bb, i, k: (bb, i, 0)),
            pl.BlockSpec((None, 1, d), lambda bb, i, k: (bb, 0, 0)),
            pl.BlockSpec((1, d), lambda bb, i, k: (0, 0)),
            pl.BlockSpec((1, d), lambda bb, i, k: (0, 0)),
        ],
        out_specs=pl.BlockSpec((None, tm, d), lambda bb, i, k: (bb, i, 0)),
        out_shape=jax.ShapeDtypeStruct(xs.shape, xs.dtype),
        compiler_params=_params(("arbitrary", "arbitrary", "arbitrary")),
        name="outproj",
    )(o, w, xs, gate_lat, gate_ctx, g_post)


def _layout(d_model):
    nh = d_model // HEAD_DIM
    aq = 3 * nh // 8
    akv = aq // GROUP
    bh = nh // 4
    cq = nh - aq - bh
    ckv = cq // GROUP
    order = [("qa", aq), ("qc", cq), ("ka", akv), ("kc", ckv), ("qb", bh), ("kb", bh),
             ("za", aq), ("zc", cq), ("zb", bh), ("va", akv), ("vc", ckv), ("vb", bh)]
    lay = {"nh": nh, "aq": aq, "akv": akv, "bh": bh, "cq": cq, "ckv": ckv}
    off = 0
    for name, width in order:
        lay[name] = off
        off += width
    lay["nprep"] = lay["za"]
    lay["oa"], lay["oc"], lay["ob"] = 0, aq, aq + cq
    ref_order = [("qa", aq), ("ka", akv), ("va", akv), ("qb", bh), ("kb", bh), ("vb", bh),
                 ("qc", cq), ("kc", ckv), ("vc", ckv), ("za", aq), ("zb", bh), ("zc", cq)]
    ref_off, off = {}, 0
    for name, width in ref_order:
        ref_off[name] = (off * HEAD_DIM, (off + width) * HEAD_DIM)
        off += width
    lay["in_perm"] = [ref_off[name] for name, _ in order]
    r_oa = (0, aq * HEAD_DIM)
    r_ob = (aq * HEAD_DIM, (aq + bh) * HEAD_DIM)
    r_oc = ((aq + bh) * HEAD_DIM, nh * HEAD_DIM)
    lay["out_perm"] = [r_oa, r_oc, r_ob]
    scale = HEAD_DIM ** -0.5
    lay["kinds"] = tuple(
        [(None, scale * LOG2E)] * aq + [("q", scale * LOG2E)] * cq + [(None, 1.0)] * akv
        + [("k", 1.0)] * ckv + [(None, 1.0)] * bh + [(None, scale)] * bh)
    return lay


def _rope_tables(n_lat, ctx_len):
    rows = n_lat // GRID_W
    row = jnp.repeat(jnp.arange(rows, dtype=F32), GRID_W)
    col = jnp.tile(jnp.arange(GRID_W, dtype=F32), rows)
    axis_dim = HEAD_DIM // 2
    inv = ROPE_BASE ** (-jnp.arange(0, axis_dim, 2, dtype=F32) / axis_dim)
    ang_r = row[:, None] * inv
    ang_c = col[:, None] * inv
    ang = jnp.concatenate([ang_r, ang_r, ang_c, ang_c], axis=-1)
    sign = jnp.where((jnp.arange(HEAD_DIM) // 32) % 2 == 1, 1.0, -1.0).astype(F32)
    cos = jnp.concatenate([jnp.ones((ctx_len, HEAD_DIM), F32), jnp.cos(ang)], axis=0)
    sin_s = jnp.concatenate([jnp.zeros((ctx_len, HEAD_DIM), F32), jnp.sin(ang) * sign], axis=0)
    return cos, sin_s


def _decay_tables(ret_decay_l):
    log_g = jax.nn.log_sigmoid(ret_decay_l.astype(F32))
    nh = log_g.shape[1]
    pos = jnp.arange(RET_CHUNK, dtype=F32)
    lane = lambda w: jnp.repeat(w, HEAD_DIM, axis=-1)
    wk_f = jnp.exp((RET_CHUNK - 1 - pos)[:, None] * log_g[0][None])
    wk_b = jnp.exp(pos[:, None] * log_g[1][None])
    wq_f = jnp.exp((pos + 1.0)[:, None] * log_g[0][None])
    wq_b = jnp.exp((RET_CHUNK - pos)[:, None] * log_g[1][None])
    wk = jnp.stack([lane(wk_f), lane(wk_b)])
    wq = jnp.stack([lane(wq_f), lane(wq_b)])
    gc = jnp.exp(RET_CHUNK * log_g)
    diff = pos[:, None] - pos[None, :]
    ef = jnp.where((diff >= 0)[None], diff[None] * log_g[0][:, None, None], -jnp.inf)
    eb = jnp.where((diff <= 0)[None], -diff[None] * log_g[1][:, None, None], -jnp.inf)
    dmat = jnp.exp(ef) + jnp.exp(eb)
    del nh
    return wk, wq, gc, dmat


def kernel(x, c, ctx, c_ctx, w_ada, b_ada, g_pre, g_post, w_in, w_out, sink_a, qnorm_c, knorm_c, ret_decay):
    bsz, n_lat, d = x.shape
    ctx_len = ctx.shape[1]
    depth = w_ada.shape[0]
    lay = _layout(d)

    cond_rows = jnp.zeros((8, d), F32).at[0:bsz].set(c).at[bsz].set(c_ctx)
    mods = _ada(cond_rows, w_ada, b_ada)
    cos, sin_s = _rope_tables(n_lat, ctx_len)
    w_in_p = jnp.concatenate([w_in[:, :, a:b] for a, b in lay["in_perm"]], axis=-1).astype(BF16)
    w_out_p = jnp.concatenate([w_out[:, a:b, :] for a, b in lay["out_perm"]], axis=1).astype(BF16)

    xs = jnp.concatenate([ctx, x], axis=1)
    for l in range(depth):
        mod_lat = mods[l, 0:bsz].reshape(bsz, 1, 3 * d)
        mod_ctx = mods[l, bsz:bsz + 1]
        p = _inproj(xs, g_pre[l].reshape(1, d), mod_lat, mod_ctx, w_in_p, l, ctx_len=ctx_len)
        p = _prep(p, cos, sin_s, qnorm_c[l].reshape(1, HEAD_DIM), knorm_c[l].reshape(1, HEAD_DIM),
                  kinds=lay["kinds"])
        o = _mixa(sink_a[l] * LOG2E, p, lay=lay, ctx_len=ctx_len)
        o = _mixc(p, o, lay=lay, ctx_len=ctx_len)
        wk, wq, gc, dmat = _decay_tables(ret_decay[l])
        sprev = _ret_states(gc, p, wk, lay=lay)
        o = _ret_out(p, sprev, dmat, wq, o, lay=lay)
        xs = _outproj(o, w_out_p, l, xs, mod_lat[:, :, 2 * d:], mod_ctx[:, 2 * d:],
                      g_post[l].reshape(1, d), ctx_len=ctx_len)
    return xs[:, ctx_len:, :]
```

```python
import functools
import math

import jax
import jax.numpy as jnp
from jax import lax
from jax.experimental import pallas as pl
from jax.experimental.pallas import tpu as pltpu

HEAD_DIM = 128
GRID_W = 64
WINDOW = 128
RET_CHUNK = 128
ROPE_BASE = 10000.0
EPS = 1e-6
LOG2E = math.log2(math.e)
GROUP = 3
VMEM_LIMIT = 56 * 1024 * 1024

F32 = jnp.float32
BF16 = jnp.bfloat16
NT_DIMS = (((1,), (1,)), ((), ()))
TN_DIMS = (((0,), (0,)), ((), ()))


def _params(sem):
    return pltpu.CompilerParams(dimension_semantics=sem, vmem_limit_bytes=VMEM_LIMIT)


def _silu(z):
    return z / (1.0 + jnp.exp(-z))


def _stack_heads(q):
    return jnp.concatenate([q[:, g * HEAD_DIM:(g + 1) * HEAD_DIM] for g in range(GROUP)], axis=0)


def _unstack_heads(o, rows):
    return jnp.concatenate([o[g * rows:(g + 1) * rows, :] for g in range(GROUP)], axis=1)


def _ada_kernel(c_ref, w_ref, b_ref, o_ref):
    cond = _silu(c_ref[...])
    o_ref[...] = jnp.dot(cond.astype(BF16), w_ref[...].astype(BF16),
                         preferred_element_type=F32) + b_ref[...]


def _ada(cond_rows, w_ada, b_ada):
    depth, d, n3 = w_ada.shape
    tn = 512
    return pl.pallas_call(
        _ada_kernel,
        grid=(depth, n3 // tn),
        in_specs=[
            pl.BlockSpec((8, d), lambda l, j: (0, 0)),
            pl.BlockSpec((None, d, tn), lambda l, j: (l, 0, j)),
            pl.BlockSpec((None, 1, tn), lambda l, j: (l, 0, j)),
        ],
        out_specs=pl.BlockSpec((None, 8, tn), lambda l, j: (l, 0, j)),
        out_shape=jax.ShapeDtypeStruct((depth, 8, n3), F32),
        compiler_params=_params(("arbitrary", "arbitrary")),
        name="ada",
    )(cond_rows, w_ada, b_ada.reshape(depth, 1, n3))


def _inproj_kernel(x_ref, g_ref, ml_ref, mc_ref, w_ref, o_ref, h_ref, *, tm, rc, ctx_len, d):
    i = pl.program_id(1)
    j = pl.program_id(2)

    @pl.when(j == 0)
    def _():
        g = g_ref[...]
        sh_l = ml_ref[:, 0:d]
        sc_l = 1.0 + ml_ref[:, d:2 * d]
        sh_c = mc_ref[:, 0:d]
        sc_c = 1.0 + mc_ref[:, d:2 * d]

        def body(c, carry):
            r0 = pl.multiple_of(c * rc, rc)
            is_ctx = jnp.logical_and(i == 0, r0 < ctx_len)
            sh = jnp.where(is_ctx, sh_c, sh_l)
            sc = jnp.where(is_ctx, sc_c, sc_l)
            x = x_ref[pl.ds(r0, rc), :]
            ms = jnp.mean(x * x, axis=-1, keepdims=True)
            y = x * lax.rsqrt(ms + EPS) * g
            h_ref[pl.ds(r0, rc), :] = (y * sc + sh).astype(BF16)
            return carry

        lax.fori_loop(0, tm // rc, body, 0)

    o_ref[...] = jnp.dot(h_ref[...], w_ref[...], preferred_element_type=F32).astype(o_ref.dtype)


def _inproj(xs, g_pre, mod_lat, mod_ctx, w, layer, *, ctx_len):
    b, t, d = xs.shape
    n = w.shape[-1]
    tm, tn, rc = 768, 512, 32
    assert t % tm == 0 and n % tn == 0 and ctx_len % rc == 0 and ctx_len <= tm
    return pl.pallas_call(
        functools.partial(_inproj_kernel, tm=tm, rc=rc, ctx_len=ctx_len, d=d),
        grid=(b, t // tm, n // tn),
        in_specs=[
            pl.BlockSpec((None, tm, d), lambda bb, i, j: (bb, i, 0)),
            pl.BlockSpec((1, d), lambda bb, i, j: (0, 0)),
            pl.BlockSpec((None, 1, 3 * d), lambda bb, i, j: (bb, 0, 0)),
            pl.BlockSpec((1, 3 * d), lambda bb, i, j: (0, 0)),
            pl.BlockSpec((None, d, tn), lambda bb, i, j: (layer, 0, j)),
        ],
        out_specs=pl.BlockSpec((None, tm, tn), lambda bb, i, j: (bb, i, j)),
        out_shape=jax.ShapeDtypeStruct((b, t, n), BF16),
        scratch_shapes=[pltpu.VMEM((tm, d), BF16)],
        compiler_params=_params(("arbitrary", "arbitrary", "arbitrary")),
        name="inproj",
    )(xs, g_pre, mod_lat, mod_ctx, w)


def _prep_kernel(p_ref, cos_ref, sin_ref, qn_ref, kn_ref, o_ref, *, kinds, tp):
    cos = cos_ref[...]
    sin = sin_ref[...]
    lane = lax.broadcasted_iota(jnp.int32, (tp, HEAD_DIM), 1)
    odd = (lane & 32) != 0
    qn = qn_ref[...]
    kn = kn_ref[...]

    def rope(x):
        r32 = pltpu.roll(x, 32, 1)
        r96 = pltpu.roll(x, 96, 1)
        return x * cos + jnp.where(odd, r32, r96) * sin

    def rms(x, g):
        return x * lax.rsqrt(jnp.mean(x * x, axis=-1, keepdims=True) + EPS) * g

    for t, (norm, scale) in enumerate(kinds):
        sl = slice(t * HEAD_DIM, (t + 1) * HEAD_DIM)
        x = p_ref[:, sl].astype(F32)
        if norm == "q":
            x = rms(x, qn)
        elif norm == "k":
            x = rms(x, kn)
        x = rope(x)
        if scale != 1.0:
            x = x * scale
        o_ref[:, sl] = x.astype(o_ref.dtype)


def _prep(p, cos, sin_s, qn, kn, *, kinds):
    b, t, n = p.shape
    tp = 256
    npre = len(kinds) * HEAD_DIM
    assert t % tp == 0
    return pl.pallas_call(
        functools.partial(_prep_kernel, kinds=kinds, tp=tp),
        grid=(b, t // tp),
        in_specs=[
            pl.BlockSpec((None, tp, npre), lambda bb, i: (bb, i, 0)),
            pl.BlockSpec((tp, HEAD_DIM), lambda bb, i: (i, 0)),
            pl.BlockSpec((tp, HEAD_DIM), lambda bb, i: (i, 0)),
            pl.BlockSpec((1, HEAD_DIM), lambda bb, i: (0, 0)),
            pl.BlockSpec((1, HEAD_DIM), lambda bb, i: (0, 0)),
        ],
        out_specs=pl.BlockSpec((None, tp, npre), lambda bb, i: (bb, i, 0)),
        out_shape=jax.ShapeDtypeStruct(p.shape, p.dtype),
        input_output_aliases={0: 0},
        compiler_params=_params(("arbitrary", "arbitrary")),
        name="prep",
    )(p, cos, sin_s, qn, kn)


def _mixa_kernel(sink_ref, q_ref, z_ref, k_ref, v_ref, o_ref, *, tq, ctx_len, s_lat):
    kv = pl.program_id(1)
    n = pl.program_id(2)
    nctx = ctx_len // tq
    is_lat = n >= nctx
    q0 = (n - nctx) * tq
    wl = tq + 2 * WINDOW
    cs = pl.multiple_of(jnp.clip(q0 - WINDOW, 0, s_lat - wl), HEAD_DIM)

    q3 = _stack_heads(q_ref[...])
    k_ctx = k_ref[0:ctx_len, :]
    v_ctx = v_ref[0:ctx_len, :]
    k_loc = k_ref[pl.ds(ctx_len + cs, wl), :]
    v_loc = v_ref[pl.ds(ctx_len + cs, wl), :]
    s_ctx = lax.dot_general(q3, k_ctx, NT_DIMS, preferred_element_type=F32)
    s_loc = lax.dot_general(q3, k_loc, NT_DIMS, preferred_element_type=F32)

    rows = GROUP * tq
    row = lax.broadcasted_iota(jnp.int32, (rows, wl), 0)
    col = lax.broadcasted_iota(jnp.int32, (rows, wl), 1)
    rel = (cs + col) - (q0 + (row & (tq - 1)))
    valid = jnp.logical_and(jnp.abs(rel) <= WINDOW, is_lat)
    s_loc = jnp.where(valid, s_loc, -jnp.inf)

    sink = jnp.concatenate(
        [jnp.full((tq, 1), sink_ref[kv * GROUP + g], F32) for g in range(GROUP)], axis=0)
    m = jnp.maximum(jnp.maximum(jnp.max(s_ctx, axis=-1, keepdims=True),
                                jnp.max(s_loc, axis=-1, keepdims=True)), sink)
    p_ctx = jnp.exp2(s_ctx - m)
    p_loc = jnp.exp2(s_loc - m)
    den = (jnp.sum(p_ctx, axis=-1, keepdims=True) + jnp.sum(p_loc, axis=-1, keepdims=True)
           + jnp.exp2(sink - m))
    o = (jnp.dot(p_ctx.astype(BF16), v_ctx, preferred_element_type=F32)
         + jnp.dot(p_loc.astype(BF16), v_loc, preferred_element_type=F32))
    o = _unstack_heads(o / den, tq)
    o_ref[...] = (o * _silu(z_ref[...].astype(F32))).astype(o_ref.dtype)


def _mixa(sink2, p, *, lay, ctx_len):
    b, t, _ = p.shape
    tq = 256
    gw = GROUP * HEAD_DIM
    nkv = lay["akv"]
    s_lat = t - ctx_len
    assert t % tq == 0 and ctx_len % tq == 0 and tq & (tq - 1) == 0
    qoff, zoff, koff, voff = lay["qa"] // GROUP, lay["za"] // GROUP, lay["ka"], lay["va"]
    return pl.pallas_call(
        functools.partial(_mixa_kernel, tq=tq, ctx_len=ctx_len, s_lat=s_lat),
        grid=(b, nkv, t // tq),
        in_specs=[
            pl.BlockSpec(memory_space=pltpu.SMEM),
            pl.BlockSpec((None, tq, gw), lambda bb, k, i: (bb, i, qoff + k)),
            pl.BlockSpec((None, tq, gw), lambda bb, k, i: (bb, i, zoff + k)),
            pl.BlockSpec((None, t, HEAD_DIM), lambda bb, k, i: (bb, 0, koff + k)),
            pl.BlockSpec((None, t, HEAD_DIM), lambda bb, k, i: (bb, 0, voff + k)),
        ],
        out_specs=pl.BlockSpec((None, tq, gw), lambda bb, k, i: (bb, i, k)),
        out_shape=jax.ShapeDtypeStruct((b, t, lay["nh"] * HEAD_DIM), BF16),
        compiler_params=_params(("arbitrary", "arbitrary", "arbitrary")),
        name="mixa",
    )(sink2, p, p, p, p)


def _mixc_kernel(q_ref, z_ref, k_ref, v_ref, oin_ref, o_ref,
                 vt_ref, s_ref, p_ref, a_ref, m_ref, l_ref, acc_ref, *, tq, tk, ctx_len, nchunk):
    del oin_ref
    qi = pl.program_id(2)

    @pl.when(qi == 0)
    def _():
        def tbody(c, carry):
            r0 = pl.multiple_of(c * tk, tk)
            vt_ref[c] = v_ref[pl.ds(r0, tk), :].astype(F32).T.astype(BF16)
            return carry

        lax.fori_loop(0, nchunk, tbody, 0)

    def q_head(g):
        return q_ref[:, g * HEAD_DIM:(g + 1) * HEAD_DIM]

    def scores(c, buf):
        r0 = pl.multiple_of(c * tk, tk)
        kc = k_ref[pl.ds(r0, tk), :]
        for g in range(GROUP):
            s_ref[buf, g] = lax.dot_general(kc, q_head(g), NT_DIMS, preferred_element_type=F32)

    def softmax(buf, first):
        for g in range(GROUP):
            s = s_ref[buf, g]
            m_cur = jnp.max(s, axis=0, keepdims=True)
            m_new = m_cur if first else jnp.maximum(m_ref[g], m_cur)
            p = jnp.exp2(s - m_new)
            l_cur = jnp.sum(p, axis=0, keepdims=True)
            if first:
                l_ref[g] = l_cur
            else:
                alpha = jnp.exp2(m_ref[g] - m_new)
                a_ref[buf, g] = alpha
                l_ref[g] = alpha * l_ref[g] + l_cur
            p_ref[buf, g] = p.astype(BF16)
            m_ref[g] = m_new

    def output(c, buf, first):
        vtc = vt_ref[c]
        for g in range(GROUP):
            pv = jnp.dot(vtc, p_ref[buf, g], preferred_element_type=F32)
            acc_ref[g] = pv if first else a_ref[buf, g] * acc_ref[g] + pv

    nctx = ctx_len // tq

    @pl.when(qi < nctx)
    def _():
        kc = k_ref[0:ctx_len, :]
        vtc = vt_ref[0, :, 0:ctx_len]
        for g in range(GROUP):
            s = lax.dot_general(kc, q_head(g), NT_DIMS, preferred_element_type=F32)
            p = jnp.exp2(s - jnp.max(s, axis=0, keepdims=True))
            l_ref[g] = jnp.sum(p, axis=0, keepdims=True)
            acc_ref[g] = jnp.dot(vtc, p.astype(BF16), preferred_element_type=F32)

    @pl.when(qi >= nctx)
    def _():
        def step(i, par, first=False):
            scores(i + 1, 1 - par)
            softmax(par, False)
            output(i - 1, 1 - par, first)

        scores(0, 0)
        scores(1, 1)
        softmax(0, True)
        mid = nchunk - 2
        if mid >= 1:
            step(1, 1, first=True)
        pairs = (mid - 1) // 2 if mid >= 1 else 0

        def body(j, carry):
            i = 2 + 2 * j
            step(i, 0)
            step(i + 1, 1)
            return carry

        lax.fori_loop(0, pairs, body, 0)
        if mid >= 1 and (mid - 1) % 2 == 1:
            step(nchunk - 2, (nchunk - 2) % 2)
        last = nchunk - 1
        softmax(last % 2, False)
        output(last - 1, (last - 1) % 2, last == 1)
        output(last, last % 2, False)

    o = jnp.concatenate([(acc_ref[g] / l_ref[g]).T for g in range(GROUP)], axis=1)
    o_ref[...] = (o * _silu(z_ref[...].astype(F32))).astype(o_ref.dtype)


def _mixc(p, o_prev, *, lay, ctx_len):
    b, t, _ = p.shape
    tq, tk = 256, 768
    gw = GROUP * HEAD_DIM
    nkv = lay["ckv"]
    assert t % tq == 0 and ctx_len % tq == 0 and t % tk == 0 and ctx_len <= tk
    qoff, zoff, koff, voff = lay["qc"] // GROUP, lay["zc"] // GROUP, lay["kc"], lay["vc"]
    ooff = lay["oc"] // GROUP
    return pl.pallas_call(
        functools.partial(_mixc_kernel, tq=tq, tk=tk, ctx_len=ctx_len, nchunk=t // tk),
        grid=(b, nkv, t // tq),
        in_specs=[
            pl.BlockSpec((None, tq, gw), lambda bb, k, i: (bb, i, qoff + k)),
            pl.BlockSpec((None, tq, gw), lambda bb, k, i: (bb, i, zoff + k)),
            pl.BlockSpec((None, t, HEAD_DIM), lambda bb, k, i: (bb, 0, koff + k)),
            pl.BlockSpec((None, t, HEAD_DIM), lambda bb, k, i: (bb, 0, voff + k)),
            pl.BlockSpec(memory_space=pl.ANY),
        ],
        out_specs=pl.BlockSpec((None, tq, gw), lambda bb, k, i: (bb, i, ooff + k)),
        out_shape=jax.ShapeDtypeStruct(o_prev.shape, o_prev.dtype),
        input_output_aliases={4: 0},
        scratch_shapes=[pltpu.VMEM((t // tk, HEAD_DIM, tk), BF16),
                        pltpu.VMEM((2, GROUP, tk, tq), F32), pltpu.VMEM((2, GROUP, tk, tq), BF16),
                        pltpu.VMEM((2, GROUP, 1, tq), F32),
                        pltpu.VMEM((GROUP, 1, tq), F32), pltpu.VMEM((GROUP, 1, tq), F32),
                        pltpu.VMEM((GROUP, HEAD_DIM, tq), F32)],
        compiler_params=_params(("arbitrary", "arbitrary", "arbitrary")),
        name="mixc",
    )(p, p, p, p, o_prev)


def _ret_state_kernel(gc_ref, k_ref, v_ref, wk_ref, o_ref, s_ref, *, nh):
    d = pl.program_id(1)
    step = pl.program_id(2)

    @pl.when(step == 0)
    def _():
        s_ref[...] = jnp.zeros(s_ref.shape, F32)

    wk = wk_ref[...]
    for it in range(2):
        ci = d if it == 0 else 1 - d
        r0 = pl.multiple_of(ci * RET_CHUNK, RET_CHUNK)
        k = (k_ref[pl.ds(r0, RET_CHUNK), :].astype(F32) * wk).astype(BF16)
        v = v_ref[pl.ds(r0, RET_CHUNK), :]
        o_ref[ci] = s_ref[...].astype(o_ref.dtype)
        for h in range(nh):
            hs = slice(h * HEAD_DIM, (h + 1) * HEAD_DIM)
            kv = lax.dot_general(k[:, hs], v[:, hs], TN_DIMS, preferred_element_type=F32)
            s_ref[hs, :] = gc_ref[d, h] * s_ref[hs, :] + kv


def _ret_states(gc, p, wk, *, lay):
    b, t, _ = p.shape
    nh = lay["bh"]
    bw = nh * HEAD_DIM
    tb = 2 * RET_CHUNK
    nblk = t // tb
    koff, voff = lay["kb"] // nh, lay["vb"] // nh
    assert t % tb == 0

    def blk(d, s):
        return jnp.where(d == 0, s, jnp.where(s == 0, 0, nblk - s))

    return pl.pallas_call(
        functools.partial(_ret_state_kernel, nh=nh),
        grid=(b, 2, nblk),
        in_specs=[
            pl.BlockSpec(memory_space=pltpu.SMEM),
            pl.BlockSpec((None, tb, bw), lambda bb, d, s: (bb, blk(d, s), koff)),
            pl.BlockSpec((None, tb, bw), lambda bb, d, s: (bb, blk(d, s), voff)),
            pl.BlockSpec((None, RET_CHUNK, bw), lambda bb, d, s: (d, 0, 0)),
        ],
        out_specs=pl.BlockSpec((None, None, 2, bw, HEAD_DIM),
                               lambda bb, d, s: (bb, d, blk(d, s), 0, 0)),
        out_shape=jax.ShapeDtypeStruct((b, 2, t // RET_CHUNK, bw, HEAD_DIM), BF16),
        scratch_shapes=[pltpu.VMEM((bw, HEAD_DIM), F32)],
        compiler_params=_params(("arbitrary", "arbitrary", "arbitrary")),
        name="ret_states",
    )(gc, p, p, wk)


def _ret_out_kernel(q_ref, k_ref, v_ref, z_ref, sp_ref, dm_ref, wq_ref, oin_ref, o_ref, *, nh):
    del oin_ref
    for c in range(2):
        rs = slice(c * RET_CHUNK, (c + 1) * RET_CHUNK)
        for h in range(nh):
            hs = slice(h * HEAD_DIM, (h + 1) * HEAD_DIM)
            q = q_ref[rs, hs]
            a = lax.dot_general(q, k_ref[rs, hs], NT_DIMS, preferred_element_type=F32) * dm_ref[h]
            q32 = q.astype(F32)
            lhs = jnp.concatenate([a.astype(BF16),
                                   (q32 * wq_ref[0, :, hs]).astype(BF16),
                                   (q32 * wq_ref[1, :, hs]).astype(BF16)], axis=1)
            rhs = jnp.concatenate([v_ref[rs, hs], sp_ref[0, c, hs, :], sp_ref[1, c, hs, :]], axis=0)
            o = jnp.dot(lhs, rhs, preferred_element_type=F32)
            mu = jnp.mean(o, axis=-1, keepdims=True)
            oc = o - mu
            var = jnp.mean(oc * oc, axis=-1, keepdims=True)
            y = oc * lax.rsqrt(var + EPS)
            o_ref[rs, hs] = (y * _silu(z_ref[rs, hs].astype(F32))).astype(o_ref.dtype)


def _ret_out(p, sprev, dmat, wq, o_prev, *, lay):
    b, t, _ = p.shape
    nh = lay["bh"]
    bw = nh * HEAD_DIM
    tb = 2 * RET_CHUNK
    qoff, koff, voff, zoff = lay["qb"] // nh, lay["kb"] // nh, lay["vb"] // nh, lay["zb"] // nh
    ooff = lay["ob"] // nh
    return pl.pallas_call(
        functools.partial(_ret_out_kernel, nh=nh),
        grid=(b, t // tb),
        in_specs=[
            pl.BlockSpec((None, tb, bw), lambda bb, i: (bb, i, qoff)),
            pl.BlockSpec((None, tb, bw), lambda bb, i: (bb, i, koff)),
            pl.BlockSpec((None, tb, bw), lambda bb, i: (bb, i, voff)),
            pl.BlockSpec((None, tb, bw), lambda bb, i: (bb, i, zoff)),
            pl.BlockSpec((None, 2, 2, bw, HEAD_DIM), lambda bb, i: (bb, 0, i, 0, 0)),
            pl.BlockSpec((nh, RET_CHUNK, RET_CHUNK), lambda bb, i: (0, 0, 0)),
            pl.BlockSpec((2, RET_CHUNK, bw), lambda bb, i: (0, 0, 0)),
            pl.BlockSpec(memory_space=pl.ANY),
        ],
        out_specs=pl.BlockSpec((None, tb, bw), lambda bb, i: (bb, i, ooff)),
        out_shape=jax.ShapeDtypeStruct(o_prev.shape, o_prev.dtype),
        input_output_aliases={7: 0},
        compiler_params=_params(("arbitrary", "arbitrary")),
        name="ret_out",
    )(p, p, p, p, sprev, dmat, wq, o_prev)


def _outproj_kernel(o_ref, w_ref, x_ref, gl_ref, gc_ref, gp_ref, y_ref, *, tm, rc, ctx_len, nk):
    i = pl.program_id(1)
    k = pl.program_id(2)
    part = jnp.dot(o_ref[...], w_ref[...], preferred_element_type=F32)

    @pl.when(k == 0)
    def _():
        y_ref[...] = part

    @pl.when(k > 0)
    def _():
        y_ref[...] += part

    @pl.when(k == nk - 1)
    def _():
        gp = gp_ref[...]
        gate_l = gl_ref[...]
        gate_c = gc_ref[...]

        def body(c, carry):
            r0 = pl.multiple_of(c * rc, rc)
            is_ctx = jnp.logical_and(i == 0, r0 < ctx_len)
            gate = jnp.where(is_ctx, gate_c, gate_l)
            y = y_ref[pl.ds(r0, rc), :]
            ms = jnp.mean(y * y, axis=-1, keepdims=True)
            yn = y * lax.rsqrt(ms + EPS) * gp
            y_ref[pl.ds(r0, rc), :] = x_ref[pl.ds(r0, rc), :] + gate * yn
            return carry

        lax.fori_loop(0, tm // rc, body, 0)


def _outproj(o, w, layer, xs, gate_lat, gate_ctx, g_post, *, ctx_len):
    b, t, d = xs.shape
    kdim = o.shape[-1]
    tm, tk, rc = 384, 1024, 32
    nk = kdim // tk
    assert t % tm == 0 and kdim % tk == 0 and ctx_len % rc == 0 and ctx_len <= tm
    return pl.pallas_call(
        functools.partial(_outproj_kernel, tm=tm, rc=rc, ctx_len=ctx_len, nk=nk),
        grid=(b, t // tm, nk),
        in_specs=[
            pl.BlockSpec((None, tm, tk), lambda bb, i, k: (bb, i, k)),
            pl.BlockSpec((None, tk, d), lambda bb, i, k: (layer, k, 0)),
            pl.BlockSpec((None, tm, d), lambda bb, i, k: (bb, i, 0)),
            pl.BlockSpec((None, 1, d), lambda bb, i, k: (bb, 0, 0)),
            pl.BlockSpec((1, d), lambda bb, i, k: (0, 0)),
            pl.BlockSpec((1, d), lambda bb, i, k: (0, 0)),
        ],
        out_specs=pl.BlockSpec((None, tm, d), lambda bb, i, k: (bb, i, 0)),
        out_shape=jax.ShapeDtypeStruct(xs.shape, xs.dtype),
        compiler_params=_params(("arbitrary", "arbitrary", "arbitrary")),
        name="outproj",
    )(o, w, xs, gate_lat, gate_ctx, g_post)


def _layout(d_model):
    nh = d_model // HEAD_DIM
    aq = 3 * nh // 8
    akv = aq // GROUP
    bh = nh // 4
    cq = nh - aq - bh
    ckv = cq // GROUP
    order = [("qa", aq), ("qc", cq), ("ka", akv), ("kc", ckv), ("qb", bh), ("kb", bh),
             ("za", aq), ("zc", cq), ("zb", bh), ("va", akv), ("vc", ckv), ("vb", bh)]
    lay = {"nh": nh, "aq": aq, "akv": akv, "bh": bh, "cq": cq, "ckv": ckv}
    off = 0
    for name, width in order:
        lay[name] = off
        off += width
    lay["nprep"] = lay["za"]
    lay["oa"], lay["oc"], lay["ob"] = 0, aq, aq + cq
    ref_order = [("qa", aq), ("ka", akv), ("va", akv), ("qb", bh), ("kb", bh), ("vb", bh),
                 ("qc", cq), ("kc", ckv), ("vc", ckv), ("za", aq), ("zb", bh), ("zc", cq)]
    ref_off, off = {}, 0
    for name, width in ref_order:
        ref_off[name] = (off * HEAD_DIM, (off + width) * HEAD_DIM)
        off += width
    lay["in_perm"] = [ref_off[name] for name, _ in order]
    r_oa = (0, aq * HEAD_DIM)
    r_ob = (aq * HEAD_DIM, (aq + bh) * HEAD_DIM)
    r_oc = ((aq + bh) * HEAD_DIM, nh * HEAD_DIM)
    lay["out_perm"] = [r_oa, r_oc, r_ob]
    scale = HEAD_DIM ** -0.5
    lay["kinds"] = tuple(
        [(None, scale * LOG2E)] * aq + [("q", scale * LOG2E)] * cq + [(None, 1.0)] * akv
        + [("k", 1.0)] * ckv + [(None, 1.0)] * bh + [(None, scale)] * bh)
    return lay


def _rope_tables(n_lat, ctx_len):
    rows = n_lat // GRID_W
    row = jnp.repeat(jnp.arange(rows, dtype=F32), GRID_W)
    col = jnp.tile(jnp.arange(GRID_W, dtype=F32), rows)
    axis_dim = HEAD_DIM // 2
    inv = ROPE_BASE ** (-jnp.arange(0, axis_dim, 2, dtype=F32) / axis_dim)
    ang_r = row[:, None] * inv
    ang_c = col[:, None] * inv
    ang = jnp.concatenate([ang_r, ang_r, ang_c, ang_c], axis=-1)
    sign = jnp.where((jnp.arange(HEAD_DIM) // 32) % 2 == 1, 1.0, -1.0).astype(F32)
    cos = jnp.concatenate([jnp.ones((ctx_len, HEAD_DIM), F32), jnp.cos(ang)], axis=0)
    sin_s = jnp.concatenate([jnp.zeros((ctx_len, HEAD_DIM), F32), jnp.sin(ang) * sign], axis=0)
    return cos, sin_s


def _decay_tables(ret_decay_l):
    log_g = jax.nn.log_sigmoid(ret_decay_l.astype(F32))
    nh = log_g.shape[1]
    pos = jnp.arange(RET_CHUNK, dtype=F32)
    lane = lambda w: jnp.repeat(w, HEAD_DIM, axis=-1)
    wk_f = jnp.exp((RET_CHUNK - 1 - pos)[:, None] * log_g[0][None])
    wk_b = jnp.exp(pos[:, None] * log_g[1][None])
    wq_f = jnp.exp((pos + 1.0)[:, None] * log_g[0][None])
    wq_b = jnp.exp((RET_CHUNK - pos)[:, None] * log_g[1][None])
    wk = jnp.stack([lane(wk_f), lane(wk_b)])
    wq = jnp.stack([lane(wq_f), lane(wq_b)])
    gc = jnp.exp(RET_CHUNK * log_g)
    diff = pos[:, None] - pos[None, :]
    ef = jnp.where((diff >= 0)[None], diff[None] * log_g[0][:, None, None], -jnp.inf)
    eb = jnp.where((diff <= 0)[None], -diff[None] * log_g[1][:, None, None], -jnp.inf)
    dmat = jnp.exp(ef) + jnp.exp(eb)
    del nh
    return wk, wq, gc, dmat


def kernel(x, c, ctx, c_ctx, w_ada, b_ada, g_pre, g_post, w_in, w_out, sink_a, qnorm_c, knorm_c, ret_decay):
    bsz, n_lat, d = x.shape
    ctx_len = ctx.shape[1]
    depth = w_ada.shape[0]
    lay = _layout(d)

    cond_rows = jnp.zeros((8, d), F32).at[0:bsz].set(c).at[bsz].set(c_ctx)
    mods = _ada(cond_rows, w_ada, b_ada)
    cos, sin_s = _rope_tables(n_lat, ctx_len)
    w_in_p = jnp.concatenate([w_in[:, :, a:b] for a, b in lay["in_perm"]], axis=-1).astype(BF16)
    w_out_p = jnp.concatenate([w_out[:, a:b, :] for a, b in lay["out_perm"]], axis=1).astype(BF16)

    xs = jnp.concatenate([ctx, x], axis=1)
    for l in range(depth):
        mod_lat = mods[l, 0:bsz].reshape(bsz, 1, 3 * d)
        mod_ctx = mods[l, bsz:bsz + 1]
        p = _inproj(xs, g_pre[l].reshape(1, d), mod_lat, mod_ctx, w_in_p, l, ctx_len=ctx_len)
        p = _prep(p, cos, sin_s, qnorm_c[l].reshape(1, HEAD_DIM), knorm_c[l].reshape(1, HEAD_DIM),
                  kinds=lay["kinds"])
        o = _mixa(sink_a[l] * LOG2E, p, lay=lay, ctx_len=ctx_len)
        o = _mixc(p, o, lay=lay, ctx_len=ctx_len)
        wk, wq, gc, dmat = _decay_tables(ret_decay[l])
        sprev = _ret_states(gc, p, wk, lay=lay)
        o = _ret_out(p, sprev, dmat, wq, o, lay=lay)
        xs = _outproj(o, w_out_p, l, xs, mod_lat[:, :, 2 * d:], mod_ctx[:, 2 * d:],
                      g_post[l].reshape(1, d), ctx_len=ctx_len)
    return xs[:, ctx_len:, :]
```

```python
import functools
import math

import jax
import jax.numpy as jnp
from jax import lax
from jax.experimental import pallas as pl
from jax.experimental.pallas import tpu as pltpu

HEAD_DIM = 128
GRID_W = 64
WINDOW = 128
RET_CHUNK = 128
ROPE_BASE = 10000.0
EPS = 1e-6
LOG2E = math.log2(math.e)
GROUP = 3
VMEM_LIMIT = 56 * 1024 * 1024
OUTPROJ_VMEM_LIMIT = 60 * 1024 * 1024

F32 = jnp.float32
BF16 = jnp.bfloat16
NT_DIMS = (((1,), (1,)), ((), ()))
TN_DIMS = (((0,), (0,)), ((), ()))


def _params(sem):
    return pltpu.CompilerParams(dimension_semantics=sem, vmem_limit_bytes=VMEM_LIMIT)


def _silu(z):
    return z / (1.0 + jnp.exp(-z))


def _ada_kernel(c_ref, w_ref, b_ref, o_ref):
    cond = _silu(c_ref[...])
    o_ref[...] = jnp.dot(cond.astype(BF16), w_ref[...].astype(BF16),
                         preferred_element_type=F32) + b_ref[...]


def _ada(cond_rows, w_ada, b_ada):
    depth, d, n3 = w_ada.shape
    tn = 512
    return pl.pallas_call(
        _ada_kernel,
        grid=(depth, n3 // tn),
        in_specs=[
            pl.BlockSpec((8, d), lambda l, j: (0, 0)),
            pl.BlockSpec((None, d, tn), lambda l, j: (l, 0, j)),
            pl.BlockSpec((None, 1, tn), lambda l, j: (l, 0, j)),
        ],
        out_specs=pl.BlockSpec((None, 8, tn), lambda l, j: (l, 0, j)),
        out_shape=jax.ShapeDtypeStruct((depth, 8, n3), F32),
        compiler_params=_params(("arbitrary", "arbitrary")),
        name="ada",
    )(cond_rows, w_ada, b_ada.reshape(depth, 1, n3))


def _inproj_kernel(x_ref, g_ref, ml_ref, mc_ref, w_ref, o_ref, h_ref, *, tm, rc, ctx_len, d):
    i = pl.program_id(1)
    j = pl.program_id(2)

    @pl.when(j == 0)
    def _():
        g = g_ref[...]
        sh_l = ml_ref[:, 0:d]
        sc_l = 1.0 + ml_ref[:, d:2 * d]
        sh_c = mc_ref[:, 0:d]
        sc_c = 1.0 + mc_ref[:, d:2 * d]

        def body(c, carry):
            r0 = pl.multiple_of(c * rc, rc)
            is_ctx = jnp.logical_and(i == 0, r0 < ctx_len)
            sh = jnp.where(is_ctx, sh_c, sh_l)
            sc = jnp.where(is_ctx, sc_c, sc_l)
            x = x_ref[pl.ds(r0, rc), :]
            ms = jnp.mean(x * x, axis=-1, keepdims=True)
            y = x * lax.rsqrt(ms + EPS) * g
            h_ref[pl.ds(r0, rc), :] = (y * sc + sh).astype(BF16)
            return carry

        lax.fori_loop(0, tm // rc, body, 0)

    o_ref[...] = jnp.dot(h_ref[...], w_ref[...], preferred_element_type=F32).astype(o_ref.dtype)


def _inproj(xs, g_pre, mod_lat, mod_ctx, w, layer, tiles, *, ctx_len):
    b, t, d = xs.shape
    n = w.shape[-1]
    tm, rc = 768, 32
    tn = tiles["tn"]
    src = tiles["src"]
    assert t % tm == 0 and n % tn == 0 and ctx_len % rc == 0 and ctx_len <= tm

    def src_tile(j):
        out = j - src[0][0] + src[0][1]
        for start, ref_start in src[1:]:
            out = jnp.where(j >= start, j - start + ref_start, out)
        return out

    return pl.pallas_call(
        functools.partial(_inproj_kernel, tm=tm, rc=rc, ctx_len=ctx_len, d=d),
        grid=(b, t // tm, n // tn),
        in_specs=[
            pl.BlockSpec((None, tm, d), lambda bb, i, j: (bb, i, 0)),
            pl.BlockSpec((1, d), lambda bb, i, j: (0, 0)),
            pl.BlockSpec((None, 1, 3 * d), lambda bb, i, j: (bb, 0, 0)),
            pl.BlockSpec((1, 3 * d), lambda bb, i, j: (0, 0)),
            pl.BlockSpec((None, d, tn), lambda bb, i, j: (layer, 0, src_tile(j))),
        ],
        out_specs=pl.BlockSpec((None, tm, tn), lambda bb, i, j: (bb, i, j)),
        out_shape=jax.ShapeDtypeStruct((b, t, n), BF16),
        scratch_shapes=[pltpu.VMEM((tm, d), BF16)],
        compiler_params=_params(("arbitrary", "arbitrary", "arbitrary")),
        name="inproj",
    )(xs, g_pre, mod_lat, mod_ctx, w)


def _prep_kernel(p_ref, cos_ref, sin_ref, qn_ref, kn_ref, o_ref, *, kinds, tp):
    cos = cos_ref[...]
    sin = sin_ref[...]
    lane = lax.broadcasted_iota(jnp.int32, (tp, HEAD_DIM), 1)
    odd = (lane & 32) != 0
    qn = qn_ref[...]
    kn = kn_ref[...]

    def rope(x):
        r32 = pltpu.roll(x, 32, 1)
        r96 = pltpu.roll(x, 96, 1)
        return x * cos + jnp.where(odd, r32, r96) * sin

    def rms(x, g):
        return x * lax.rsqrt(jnp.mean(x * x, axis=-1, keepdims=True) + EPS) * g

    for t, (norm, scale) in enumerate(kinds):
        sl = slice(t * HEAD_DIM, (t + 1) * HEAD_DIM)
        x = p_ref[:, sl].astype(F32)
        if norm == "q":
            x = rms(x, qn)
        elif norm == "k":
            x = rms(x, kn)
        x = rope(x)
        if scale != 1.0:
            x = x * scale
        o_ref[:, sl] = x.astype(o_ref.dtype)


def _prep(p, cos, sin_s, qn, kn, *, kinds):
    b, t, n = p.shape
    tp = 256
    npre = len(kinds) * HEAD_DIM
    assert t % tp == 0
    return pl.pallas_call(
        functools.partial(_prep_kernel, kinds=kinds, tp=tp),
        grid=(b, t // tp),
        in_specs=[
            pl.BlockSpec((None, tp, npre), lambda bb, i: (bb, i, 0)),
            pl.BlockSpec((tp, HEAD_DIM), lambda bb, i: (i, 0)),
            pl.BlockSpec((tp, HEAD_DIM), lambda bb, i: (i, 0)),
            pl.BlockSpec((1, HEAD_DIM), lambda bb, i: (0, 0)),
            pl.BlockSpec((1, HEAD_DIM), lambda bb, i: (0, 0)),
        ],
        out_specs=pl.BlockSpec((None, tp, npre), lambda bb, i: (bb, i, 0)),
        out_shape=jax.ShapeDtypeStruct(p.shape, p.dtype),
        input_output_aliases={0: 0},
        compiler_params=_params(("arbitrary", "arbitrary")),
        name="prep",
    )(p, cos, sin_s, qn, kn)


def _mixa_kernel(sink_ref, q_ref, z_ref, k_ref, v_ref, o_ref, vt_ref, s_ref, p_ref,
                 *, tq, nsub, ctx_len, s_lat):
    kv = pl.program_id(1)
    blk = pl.program_id(2)
    wl = tq + 2 * WINDOW
    nkeys = ctx_len + wl

    @pl.when(blk == 0)
    def _():
        def tbody(c, carry):
            r0 = pl.multiple_of(c * HEAD_DIM, HEAD_DIM)
            vt_ref[c] = v_ref[pl.ds(r0, HEAD_DIM), :].astype(F32).T.astype(BF16)
            return carry

        lax.fori_loop(0, (ctx_len + s_lat) // HEAD_DIM, tbody, 0)

    def geometry(u):
        q0 = (blk * nsub + u) * tq - ctx_len
        cs = pl.multiple_of(jnp.clip(q0 - WINDOW, 0, s_lat - wl), HEAD_DIM)
        return q0, cs

    def scores(u):
        _, cs = geometry(u)
        keys = jnp.concatenate([k_ref[0:ctx_len, :], k_ref[pl.ds(ctx_len + cs, wl), :]], axis=0)
        for g in range(GROUP):
            q = q_ref[u * tq:(u + 1) * tq, g * HEAD_DIM:(g + 1) * HEAD_DIM]
            s_ref[u, g] = lax.dot_general(keys, q, NT_DIMS, preferred_element_type=F32)

    def softmax(u):
        q0, cs = geometry(u)
        row = lax.broadcasted_iota(jnp.int32, (nkeys, tq), 0)
        col = lax.broadcasted_iota(jnp.int32, (nkeys, tq), 1)
        rel = (cs - ctx_len - q0) + row - col
        local_ok = jnp.logical_and(jnp.abs(rel) <= WINDOW, q0 >= 0)
        valid = jnp.logical_or(row < ctx_len, local_ok)
        dens = []
        for g in range(GROUP):
            sink = sink_ref[kv * GROUP + g]
            s = jnp.where(valid, s_ref[u, g], -jnp.inf)
            m = jnp.maximum(jnp.max(s, axis=0, keepdims=True), sink)
            p = jnp.exp2(s - m)
            dens.append(jnp.sum(p, axis=0, keepdims=True) + jnp.exp2(sink - m))
            p_ref[u, g] = p.astype(BF16)
        return dens

    def output(u, dens):
        _, cs = geometry(u)
        c0 = (ctx_len + cs) // HEAD_DIM
        tiles = [vt_ref[c] for c in range(ctx_len // HEAD_DIM)] + [vt_ref[c0 + c] for c in range(wl // HEAD_DIM)]
        vt = jnp.concatenate(tiles, axis=1)
        for g in range(GROUP):
            o = jnp.dot(vt, p_ref[u, g], preferred_element_type=F32) / dens[g]
            z = z_ref[u * tq:(u + 1) * tq, g * HEAD_DIM:(g + 1) * HEAD_DIM].astype(F32)
            o_ref[u * tq:(u + 1) * tq, g * HEAD_DIM:(g + 1) * HEAD_DIM] = (o.T * _silu(z)).astype(o_ref.dtype)

    dens = {}
    scores(0)
    for u in range(nsub):
        if u + 1 < nsub:
            scores(u + 1)
        dens[u] = softmax(u)
        if u >= 1:
            output(u - 1, dens[u - 1])
    output(nsub - 1, dens[nsub - 1])


def _mixa(sink2, p, *, lay, ctx_len):
    b, t, _ = p.shape
    tq, nsub = 256, 3
    gw = GROUP * HEAD_DIM
    nkv = lay["akv"]
    s_lat = t - ctx_len
    wl = tq + 2 * WINDOW
    tb = tq * nsub
    assert t % tb == 0 and ctx_len % tq == 0 and s_lat >= wl and WINDOW == HEAD_DIM
    qoff, zoff, koff, voff = lay["qa"] // GROUP, lay["za"] // GROUP, lay["ka"], lay["va"]
    return pl.pallas_call(
        functools.partial(_mixa_kernel, tq=tq, nsub=nsub, ctx_len=ctx_len, s_lat=s_lat),
        grid=(b, nkv, t // tb),
        in_specs=[
            pl.BlockSpec(memory_space=pltpu.SMEM),
            pl.BlockSpec((None, tb, gw), lambda bb, k, i: (bb, i, qoff + k)),
            pl.BlockSpec((None, tb, gw), lambda bb, k, i: (bb, i, zoff + k)),
            pl.BlockSpec((None, t, HEAD_DIM), lambda bb, k, i: (bb, 0, koff + k)),
            pl.BlockSpec((None, t, HEAD_DIM), lambda bb, k, i: (bb, 0, voff + k)),
        ],
        out_specs=pl.BlockSpec((None, tb, gw), lambda bb, k, i: (bb, i, k)),
        out_shape=jax.ShapeDtypeStruct((b, t, lay["nh"] * HEAD_DIM), BF16),
        scratch_shapes=[pltpu.VMEM((t // HEAD_DIM, HEAD_DIM, HEAD_DIM), BF16),
                        pltpu.VMEM((nsub, GROUP, ctx_len + wl, tq), F32),
                        pltpu.VMEM((nsub, GROUP, ctx_len + wl, tq), BF16)],
        compiler_params=_params(("arbitrary", "arbitrary", "arbitrary")),
        name="mixa",
    )(sink2, p, p, p, p)


def _mixc_kernel(q_ref, z_ref, k_ref, v_ref, oin_ref, o_ref,
                 vt_ref, s_ref, p_ref, a_ref, m_ref, l_ref, acc_ref, *, tq, tk, ctx_len, nchunk):
    del oin_ref
    qi = pl.program_id(2)

    @pl.when(qi == 0)
    def _():
        def tbody(c, carry):
            r0 = pl.multiple_of(c * tk, tk)
            vt_ref[c] = v_ref[pl.ds(r0, tk), :].astype(F32).T.astype(BF16)
            return carry

        lax.fori_loop(0, nchunk, tbody, 0)

    def q_head(g):
        return q_ref[:, g * HEAD_DIM:(g + 1) * HEAD_DIM]

    def scores(c, buf):
        r0 = pl.multiple_of(c * tk, tk)
        kc = k_ref[pl.ds(r0, tk), :]
        for g in range(GROUP):
            s_ref[buf, g] = lax.dot_general(kc, q_head(g), NT_DIMS, preferred_element_type=F32)

    def softmax(buf, first):
        for g in range(GROUP):
            s = s_ref[buf, g]
            m_cur = jnp.max(s, axis=0, keepdims=True)
            m_new = m_cur if first else jnp.maximum(m_ref[g], m_cur)
            p = jnp.exp2(s - m_new)
            l_cur = jnp.sum(p, axis=0, keepdims=True)
            if first:
                l_ref[g] = l_cur
            else:
                alpha = jnp.exp2(m_ref[g] - m_new)
                a_ref[buf, g] = alpha
                l_ref[g] = alpha * l_ref[g] + l_cur
            p_ref[buf, g] = p.astype(BF16)
            m_ref[g] = m_new

    def output(c, buf, first):
        vtc = vt_ref[c]
        for g in range(GROUP):
            pv = jnp.dot(vtc, p_ref[buf, g], preferred_element_type=F32)
            acc_ref[g] = pv if first else a_ref[buf, g] * acc_ref[g] + pv

    nctx = ctx_len // tq

    @pl.when(qi < nctx)
    def _():
        kc = k_ref[0:ctx_len, :]
        vtc = vt_ref[0, :, 0:ctx_len]
        for g in range(GROUP):
            s = lax.dot_general(kc, q_head(g), NT_DIMS, preferred_element_type=F32)
            p = jnp.exp2(s - jnp.max(s, axis=0, keepdims=True))
            l_ref[g] = jnp.sum(p, axis=0, keepdims=True)
            acc_ref[g] = jnp.dot(vtc, p.astype(BF16), preferred_element_type=F32)

    @pl.when(qi >= nctx)
    def _():
        def step(i, par, first=False):
            scores(i + 1, 1 - par)
            softmax(par, False)
            output(i - 1, 1 - par, first)

        scores(0, 0)
        scores(1, 1)
        softmax(0, True)
        mid = nchunk - 2
        if mid >= 1:
            step(1, 1, first=True)
        pairs = (mid - 1) // 2 if mid >= 1 else 0

        def body(j, carry):
            i = 2 + 2 * j
            step(i, 0)
            step(i + 1, 1)
            return carry

        lax.fori_loop(0, pairs, body, 0)
        if mid >= 1 and (mid - 1) % 2 == 1:
            step(nchunk - 2, (nchunk - 2) % 2)
        last = nchunk - 1
        softmax(last % 2, False)
        output(last - 1, (last - 1) % 2, last == 1)
        output(last, last % 2, False)

    o = jnp.concatenate([(acc_ref[g] / l_ref[g]).T for g in range(GROUP)], axis=1)
    o_ref[...] = (o * _silu(z_ref[...].astype(F32))).astype(o_ref.dtype)


def _mixc(p, o_prev, *, lay, ctx_len):
    b, t, _ = p.shape
    tq, tk = 256, 768
    gw = GROUP * HEAD_DIM
    nkv = lay["ckv"]
    assert t % tq == 0 and ctx_len % tq == 0 and t % tk == 0 and ctx_len <= tk
    qoff, zoff, koff, voff = lay["qc"] // GROUP, lay["zc"] // GROUP, lay["kc"], lay["vc"]
    ooff = lay["oc"] // GROUP
    return pl.pallas_call(
        functools.partial(_mixc_kernel, tq=tq, tk=tk, ctx_len=ctx_len, nchunk=t // tk),
        grid=(b, nkv, t // tq),
        in_specs=[
            pl.BlockSpec((None, tq, gw), lambda bb, k, i: (bb, i, qoff + k)),
            pl.BlockSpec((None, tq, gw), lambda bb, k, i: (bb, i, zoff + k)),
            pl.BlockSpec((None, t, HEAD_DIM), lambda bb, k, i: (bb, 0, koff + k)),
            pl.BlockSpec((None, t, HEAD_DIM), lambda bb, k, i: (bb, 0, voff + k)),
            pl.BlockSpec(memory_space=pl.ANY),
        ],
        out_specs=pl.BlockSpec((None, tq, gw), lambda bb, k, i: (bb, i, ooff + k)),
        out_shape=jax.ShapeDtypeStruct(o_prev.shape, o_prev.dtype),
        input_output_aliases={4: 0},
        scratch_shapes=[pltpu.VMEM((t // tk, HEAD_DIM, tk), BF16),
                        pltpu.VMEM((2, GROUP, tk, tq), F32), pltpu.VMEM((2, GROUP, tk, tq), BF16),
                        pltpu.VMEM((2, GROUP, 1, tq), F32),
                        pltpu.VMEM((GROUP, 1, tq), F32), pltpu.VMEM((GROUP, 1, tq), F32),
                        pltpu.VMEM((GROUP, HEAD_DIM, tq), F32)],
        compiler_params=_params(("arbitrary", "arbitrary", "arbitrary")),
        name="mixc",
    )(p, p, p, p, o_prev)


def _ret_state_kernel(gc_ref, k_ref, v_ref, wk_ref, o_ref, s_ref, *, nh):
    d = pl.program_id(1)
    step = pl.program_id(2)

    @pl.when(step == 0)
    def _():
        s_ref[...] = jnp.zeros(s_ref.shape, F32)

    wk = wk_ref[...]
    for it in range(2):
        ci = d if it == 0 else 1 - d
        r0 = pl.multiple_of(ci * RET_CHUNK, RET_CHUNK)
        k = (k_ref[pl.ds(r0, RET_CHUNK), :].astype(F32) * wk).astype(BF16)
        v = v_ref[pl.ds(r0, RET_CHUNK), :]
        o_ref[ci] = s_ref[...].astype(o_ref.dtype)
        for h in range(nh):
            hs = slice(h * HEAD_DIM, (h + 1) * HEAD_DIM)
            kv = lax.dot_general(k[:, hs], v[:, hs], TN_DIMS, preferred_element_type=F32)
            s_ref[hs, :] = gc_ref[d, h] * s_ref[hs, :] + kv


def _ret_states(gc, p, wk, *, lay):
    b, t, _ = p.shape
    nh = lay["bh"]
    bw = nh * HEAD_DIM
    tb = 2 * RET_CHUNK
    nblk = t // tb
    koff, voff = lay["kb"] // nh, lay["vb"] // nh
    assert t % tb == 0

    def blk(d, s):
        return jnp.where(d == 0, s, jnp.where(s == 0, 0, nblk - s))

    return pl.pallas_call(
        functools.partial(_ret_state_kernel, nh=nh),
        grid=(b, 2, nblk),
        in_specs=[
            pl.BlockSpec(memory_space=pltpu.SMEM),
            pl.BlockSpec((None, tb, bw), lambda bb, d, s: (bb, blk(d, s), koff)),
            pl.BlockSpec((None, tb, bw), lambda bb, d, s: (bb, blk(d, s), voff)),
            pl.BlockSpec((None, RET_CHUNK, bw), lambda bb, d, s: (d, 0, 0)),
        ],
        out_specs=pl.BlockSpec((None, None, 2, bw, HEAD_DIM),
                               lambda bb, d, s: (bb, d, blk(d, s), 0, 0)),
        out_shape=jax.ShapeDtypeStruct((b, 2, t // RET_CHUNK, bw, HEAD_DIM), BF16),
        scratch_shapes=[pltpu.VMEM((bw, HEAD_DIM), F32)],
        compiler_params=_params(("arbitrary", "arbitrary", "arbitrary")),
        name="ret_states",
    )(gc, p, p, wk)


def _ret_out_kernel(q_ref, k_ref, v_ref, z_ref, sp_ref, dm_ref, wq_ref, oin_ref, o_ref, *, nh):
    del oin_ref
    for c in range(2):
        rs = slice(c * RET_CHUNK, (c + 1) * RET_CHUNK)
        for h in range(nh):
            hs = slice(h * HEAD_DIM, (h + 1) * HEAD_DIM)
            q = q_ref[rs, hs]
            a = lax.dot_general(q, k_ref[rs, hs], NT_DIMS, preferred_element_type=F32) * dm_ref[h]
            q32 = q.astype(F32)
            lhs = jnp.concatenate([a.astype(BF16),
                                   (q32 * wq_ref[0, :, hs]).astype(BF16),
                                   (q32 * wq_ref[1, :, hs]).astype(BF16)], axis=1)
            rhs = jnp.concatenate([v_ref[rs, hs], sp_ref[0, c, hs, :], sp_ref[1, c, hs, :]], axis=0)
            o = jnp.dot(lhs, rhs, preferred_element_type=F32)
            mu = jnp.mean(o, axis=-1, keepdims=True)
            oc = o - mu
            var = jnp.mean(oc * oc, axis=-1, keepdims=True)
            y = oc * lax.rsqrt(var + EPS)
            o_ref[rs, hs] = (y * _silu(z_ref[rs, hs].astype(F32))).astype(o_ref.dtype)


def _ret_out(p, sprev, dmat, wq, o_prev, *, lay):
    b, t, _ = p.shape
    nh = lay["bh"]
    bw = nh * HEAD_DIM
    tb = 2 * RET_CHUNK
    qoff, koff, voff, zoff = lay["qb"] // nh, lay["kb"] // nh, lay["vb"] // nh, lay["zb"] // nh
    ooff = lay["ob"] // nh
    return pl.pallas_call(
        functools.partial(_ret_out_kernel, nh=nh),
        grid=(b, t // tb),
        in_specs=[
            pl.BlockSpec((None, tb, bw), lambda bb, i: (bb, i, qoff)),
            pl.BlockSpec((None, tb, bw), lambda bb, i: (bb, i, koff)),
            pl.BlockSpec((None, tb, bw), lambda bb, i: (bb, i, voff)),
            pl.BlockSpec((None, tb, bw), lambda bb, i: (bb, i, zoff)),
            pl.BlockSpec((None, 2, 2, bw, HEAD_DIM), lambda bb, i: (bb, 0, i, 0, 0)),
            pl.BlockSpec((nh, RET_CHUNK, RET_CHUNK), lambda bb, i: (0, 0, 0)),
            pl.BlockSpec((2, RET_CHUNK, bw), lambda bb, i: (0, 0, 0)),
            pl.BlockSpec(memory_space=pl.ANY),
        ],
        out_specs=pl.BlockSpec((None, tb, bw), lambda bb, i: (bb, i, ooff)),
        out_shape=jax.ShapeDtypeStruct(o_prev.shape, o_prev.dtype),
        input_output_aliases={7: 0},
        compiler_params=_params(("arbitrary", "arbitrary")),
        name="ret_out",
    )(p, p, p, p, sprev, dmat, wq, o_prev)


def _outproj_kernel(o_ref, w_ref, x_ref, gl_ref, gc_ref, gp_ref, y_ref, *, tm, rc, first_lat, segs):
    i = pl.program_id(1)
    o = jnp.concatenate([o_ref[:, a:b] for a, b in segs], axis=1)
    y_ref[...] = jnp.dot(o, w_ref[...], preferred_element_type=F32)
    gate = jnp.where(i < first_lat, gc_ref[...], gl_ref[...])
    gp = gp_ref[...]

    def body(c, carry):
        r0 = pl.multiple_of(c * rc, rc)
        y = y_ref[pl.ds(r0, rc), :]
        ms = jnp.mean(y * y, axis=-1, keepdims=True)
        yn = y * lax.rsqrt(ms + EPS) * gp
        y_ref[pl.ds(r0, rc), :] = x_ref[pl.ds(r0, rc), :] + gate * yn
        return carry

    lax.fori_loop(0, tm // rc, body, 0)


def _outproj(o, w, layer, xs, gate_lat, gate_ctx, g_post, *, ctx_len, segs, latent_only):
    b, t, d = xs.shape
    kdim = o.shape[-1]
    tm, rc = 256, 32
    assert t % tm == 0 and ctx_len % tm == 0 and w.shape[1] == kdim
    nctx = ctx_len // tm
    off = nctx if latent_only else 0
    first_lat = 0 if latent_only else nctx
    t_out = t - off * tm
    return pl.pallas_call(
        functools.partial(_outproj_kernel, tm=tm, rc=rc, first_lat=first_lat, segs=segs),
        grid=(b, t_out // tm),
        in_specs=[
            pl.BlockSpec((None, tm, kdim), lambda bb, i: (bb, i + off, 0)),
            pl.BlockSpec((None, kdim, d), lambda bb, i: (layer, 0, 0), pipeline_mode=pl.Buffered(1)),
            pl.BlockSpec((None, tm, d), lambda bb, i: (bb, i + off, 0)),
            pl.BlockSpec((None, 1, d), lambda bb, i: (bb, 0, 0)),
            pl.BlockSpec((1, d), lambda bb, i: (0, 0)),
            pl.BlockSpec((1, d), lambda bb, i: (0, 0)),
        ],
        out_specs=pl.BlockSpec((None, tm, d), lambda bb, i: (bb, i, 0)),
        out_shape=jax.ShapeDtypeStruct((b, t_out, d), xs.dtype),
        compiler_params=pltpu.CompilerParams(dimension_semantics=("arbitrary", "arbitrary"),
                                             vmem_limit_bytes=OUTPROJ_VMEM_LIMIT),
        name="outproj",
    )(o, w, xs, gate_lat, gate_ctx, g_post)


def _layout(d_model):
    nh = d_model // HEAD_DIM
    aq = 3 * nh // 8
    akv = aq // GROUP
    bh = nh // 4
    cq = nh - aq - bh
    ckv = cq // GROUP
    order = [("qa", aq), ("qc", cq), ("ka", akv), ("kc", ckv), ("qb", bh), ("kb", bh),
             ("za", aq), ("zc", cq), ("zb", bh), ("va", akv), ("vc", ckv), ("vb", bh)]
    lay = {"nh": nh, "aq": aq, "akv": akv, "bh": bh, "cq": cq, "ckv": ckv}
    off = 0
    for name, width in order:
        lay[name] = off
        off += width
    lay["nprep"] = lay["za"]
    lay["oa"], lay["oc"], lay["ob"] = 0, aq, aq + cq
    ref_order = [("qa", aq), ("ka", akv), ("va", akv), ("qb", bh), ("kb", bh), ("vb", bh),
                 ("qc", cq), ("kc", ckv), ("vc", ckv), ("za", aq), ("zb", bh), ("zc", cq)]
    ref_off, off = {}, 0
    for name, width in ref_order:
        ref_off[name] = off
        off += width
    lay["o_segs"] = ((0, aq * HEAD_DIM), ((aq + cq) * HEAD_DIM, nh * HEAD_DIM),
                     (aq * HEAD_DIM, (aq + cq) * HEAD_DIM))
    tu = math.gcd(4, *[width for _, width in order])
    lay["tiles"] = {"tn": tu * HEAD_DIM,
                    "src": tuple((lay[name] // tu, ref_off[name] // tu) for name, _ in order)}
    scale = HEAD_DIM ** -0.5
    lay["kinds"] = tuple(
        [(None, scale * LOG2E)] * aq + [("q", scale * LOG2E)] * cq + [(None, 1.0)] * akv
        + [("k", 1.0)] * ckv + [(None, 1.0)] * bh + [(None, scale)] * bh)
    return lay


def _rope_tables(n_lat, ctx_len):
    rows = n_lat // GRID_W
    row = jnp.repeat(jnp.arange(rows, dtype=F32), GRID_W)
    col = jnp.tile(jnp.arange(GRID_W, dtype=F32), rows)
    axis_dim = HEAD_DIM // 2
    inv = ROPE_BASE ** (-jnp.arange(0, axis_dim, 2, dtype=F32) / axis_dim)
    ang_r = row[:, None] * inv
    ang_c = col[:, None] * inv
    ang = jnp.concatenate([ang_r, ang_r, ang_c, ang_c], axis=-1)
    sign = jnp.where((jnp.arange(HEAD_DIM) // 32) % 2 == 1, 1.0, -1.0).astype(F32)
    cos = jnp.concatenate([jnp.ones((ctx_len, HEAD_DIM), F32), jnp.cos(ang)], axis=0)
    sin_s = jnp.concatenate([jnp.zeros((ctx_len, HEAD_DIM), F32), jnp.sin(ang) * sign], axis=0)
    return cos, sin_s


def _decay_tables(ret_decay_l):
    log_g = jax.nn.log_sigmoid(ret_decay_l.astype(F32))
    nh = log_g.shape[1]
    pos = jnp.arange(RET_CHUNK, dtype=F32)
    lane = lambda w: jnp.repeat(w, HEAD_DIM, axis=-1)
    wk_f = jnp.exp((RET_CHUNK - 1 - pos)[:, None] * log_g[0][None])
    wk_b = jnp.exp(pos[:, None] * log_g[1][None])
    wq_f = jnp.exp((pos + 1.0)[:, None] * log_g[0][None])
    wq_b = jnp.exp((RET_CHUNK - pos)[:, None] * log_g[1][None])
    wk = jnp.stack([lane(wk_f), lane(wk_b)])
    wq = jnp.stack([lane(wq_f), lane(wq_b)])
    gc = jnp.exp(RET_CHUNK * log_g)
    diff = pos[:, None] - pos[None, :]
    ef = jnp.where((diff >= 0)[None], diff[None] * log_g[0][:, None, None], -jnp.inf)
    eb = jnp.where((diff <= 0)[None], -diff[None] * log_g[1][:, None, None], -jnp.inf)
    dmat = jnp.exp(ef) + jnp.exp(eb)
    del nh
    return wk, wq, gc, dmat


def kernel(x, c, ctx, c_ctx, w_ada, b_ada, g_pre, g_post, w_in, w_out, sink_a, qnorm_c, knorm_c, ret_decay):
    bsz, n_lat, d = x.shape
    ctx_len = ctx.shape[1]
    depth = w_ada.shape[0]
    lay = _layout(d)

    cond_rows = jnp.zeros((8, d), F32).at[0:bsz].set(c).at[bsz].set(c_ctx)
    mods = _ada(cond_rows, w_ada, b_ada)
    cos, sin_s = _rope_tables(n_lat, ctx_len)
    w_in_p = w_in.astype(BF16)
    w_out_p = w_out.astype(BF16)

    xs = jnp.concatenate([ctx, x], axis=1)
    for l in range(depth):
        mod_lat = mods[l, 0:bsz].reshape(bsz, 1, 3 * d)
        mod_ctx = mods[l, bsz:bsz + 1]
        p = _inproj(xs, g_pre[l].reshape(1, d), mod_lat, mod_ctx, w_in_p, l, lay["tiles"], ctx_len=ctx_len)
        p = _prep(p, cos, sin_s, qnorm_c[l].reshape(1, HEAD_DIM), knorm_c[l].reshape(1, HEAD_DIM),
                  kinds=lay["kinds"])
        o = _mixa(sink_a[l] * LOG2E, p, lay=lay, ctx_len=ctx_len)
        o = _mixc(p, o, lay=lay, ctx_len=ctx_len)
        wk, wq, gc, dmat = _decay_tables(ret_decay[l])
        sprev = _ret_states(gc, p, wk, lay=lay)
        o = _ret_out(p, sprev, dmat, wq, o, lay=lay)
        xs = _outproj(o, w_out_p, l, xs, mod_lat[:, :, 2 * d:], mod_ctx[:, 2 * d:],
                      g_post[l].reshape(1, d), ctx_len=ctx_len, segs=lay["o_segs"],
                      latent_only=(l == depth - 1))
    return xs
```

```python
import functools
import math

import jax
import jax.numpy as jnp
from jax import lax
from jax.experimental import pallas as pl
from jax.experimental.pallas import tpu as pltpu

HEAD_DIM = 128
GRID_W = 64
WINDOW = 128
RET_CHUNK = 128
ROPE_BASE = 10000.0
EPS = 1e-6
LOG2E = math.log2(math.e)
GROUP = 3
VMEM_LIMIT = 56 * 1024 * 1024
OUTPROJ_VMEM_LIMIT = 60 * 1024 * 1024

F32 = jnp.float32
BF16 = jnp.bfloat16
NT_DIMS = (((1,), (1,)), ((), ()))
TN_DIMS = (((0,), (0,)), ((), ()))


def _params(sem):
    return pltpu.CompilerParams(dimension_semantics=sem, vmem_limit_bytes=VMEM_LIMIT)


def _silu(z):
    return z / (1.0 + jnp.exp(-z))


def _ada_kernel(c_ref, w_ref, b_ref, o_ref):
    cond = _silu(c_ref[...])
    o_ref[...] = jnp.dot(cond.astype(BF16), w_ref[...].astype(BF16),
                         preferred_element_type=F32) + b_ref[...]


def _ada(cond_rows, w_ada, b_ada):
    depth, d, n3 = w_ada.shape
    tn = 512
    return pl.pallas_call(
        _ada_kernel,
        grid=(depth, n3 // tn),
        in_specs=[
            pl.BlockSpec((8, d), lambda l, j: (0, 0)),
            pl.BlockSpec((None, d, tn), lambda l, j: (l, 0, j)),
            pl.BlockSpec((None, 1, tn), lambda l, j: (l, 0, j)),
        ],
        out_specs=pl.BlockSpec((None, 8, tn), lambda l, j: (l, 0, j)),
        out_shape=jax.ShapeDtypeStruct((depth, 8, n3), F32),
        compiler_params=_params(("arbitrary", "arbitrary")),
        name="ada",
    )(cond_rows, w_ada, b_ada.reshape(depth, 1, n3))


def _inproj_kernel(x_ref, g_ref, ml_ref, mc_ref, *refs, tm, tn, rc, ctx_len, d):
    w_refs, o_ref, h_ref = refs[:-2], refs[-2], refs[-1]
    i = pl.program_id(1)
    j = pl.program_id(2)

    @pl.when(j == 0)
    def _():
        g = g_ref[...]
        sh_l = ml_ref[:, 0:d]
        sc_l = 1.0 + ml_ref[:, d:2 * d]
        sh_c = mc_ref[:, 0:d]
        sc_c = 1.0 + mc_ref[:, d:2 * d]

        def body(c, carry):
            r0 = pl.multiple_of(c * rc, rc)
            is_ctx = jnp.logical_and(i == 0, r0 < ctx_len)
            sh = jnp.where(is_ctx, sh_c, sh_l)
            sc = jnp.where(is_ctx, sc_c, sc_l)
            x = x_ref[pl.ds(r0, rc), :]
            ms = jnp.mean(x * x, axis=-1, keepdims=True)
            y = x * lax.rsqrt(ms + EPS) * g
            h_ref[pl.ds(r0, rc), :] = (y * sc + sh).astype(BF16)
            return carry

        lax.fori_loop(0, tm // rc, body, 0)

    for u, w_ref in enumerate(w_refs):
        o_ref[:, u * tn:(u + 1) * tn] = jnp.dot(
            h_ref[...], w_ref[...], preferred_element_type=F32).astype(o_ref.dtype)


def _inproj(xs, g_pre, mod_lat, mod_ctx, w, layer, tiles, *, ctx_len):
    b, t, d = xs.shape
    n = w.shape[-1]
    tm, rc = 768, 32
    tn = tiles["tn"]
    src = tiles["src"]
    nw = 2
    assert t % tm == 0 and n % (nw * tn) == 0 and ctx_len % rc == 0 and ctx_len <= tm

    def src_tile(j):
        out = j - src[0][0] + src[0][1]
        for start, ref_start in src[1:]:
            out = jnp.where(j >= start, j - start + ref_start, out)
        return out

    def w_spec(u):
        return pl.BlockSpec((None, d, tn), lambda bb, i, j: (layer, 0, src_tile(nw * j + u)))

    return pl.pallas_call(
        functools.partial(_inproj_kernel, tm=tm, tn=tn, rc=rc, ctx_len=ctx_len, d=d),
        grid=(b, t // tm, n // (nw * tn)),
        in_specs=[
            pl.BlockSpec((None, tm, d), lambda bb, i, j: (bb, i, 0)),
            pl.BlockSpec((1, d), lambda bb, i, j: (0, 0)),
            pl.BlockSpec((None, 1, 3 * d), lambda bb, i, j: (bb, 0, 0)),
            pl.BlockSpec((1, 3 * d), lambda bb, i, j: (0, 0)),
        ] + [w_spec(u) for u in range(nw)],
        out_specs=pl.BlockSpec((None, tm, nw * tn), lambda bb, i, j: (bb, i, j)),
        out_shape=jax.ShapeDtypeStruct((b, t, n), BF16),
        scratch_shapes=[pltpu.VMEM((tm, d), BF16)],
        compiler_params=_params(("arbitrary", "arbitrary", "arbitrary")),
        name="inproj",
    )(xs, g_pre, mod_lat, mod_ctx, *([w] * nw))


def _prep_kernel(p_ref, cos_ref, sin_ref, qn_ref, kn_ref, o_ref, *, kinds, tp):
    cos = cos_ref[...]
    sin = sin_ref[...]
    lane = lax.broadcasted_iota(jnp.int32, (tp, HEAD_DIM), 1)
    odd = (lane & 32) != 0
    qn = qn_ref[...]
    kn = kn_ref[...]

    def rope(x):
        r32 = pltpu.roll(x, 32, 1)
        r96 = pltpu.roll(x, 96, 1)
        return x * cos + jnp.where(odd, r32, r96) * sin

    def rms(x, g):
        return x * lax.rsqrt(jnp.mean(x * x, axis=-1, keepdims=True) + EPS) * g

    for t, (norm, scale) in enumerate(kinds):
        sl = slice(t * HEAD_DIM, (t + 1) * HEAD_DIM)
        x = p_ref[:, sl].astype(F32)
        if norm == "q":
            x = rms(x, qn)
        elif norm == "k":
            x = rms(x, kn)
        x = rope(x)
        if scale != 1.0:
            x = x * scale
        o_ref[:, sl] = x.astype(o_ref.dtype)


def _prep(p, cos, sin_s, qn, kn, *, kinds):
    b, t, n = p.shape
    tp = 256
    npre = len(kinds) * HEAD_DIM
    assert t % tp == 0
    return pl.pallas_call(
        functools.partial(_prep_kernel, kinds=kinds, tp=tp),
        grid=(b, t // tp),
        in_specs=[
            pl.BlockSpec((None, tp, npre), lambda bb, i: (bb, i, 0)),
            pl.BlockSpec((tp, HEAD_DIM), lambda bb, i: (i, 0)),
            pl.BlockSpec((tp, HEAD_DIM), lambda bb, i: (i, 0)),
            pl.BlockSpec((1, HEAD_DIM), lambda bb, i: (0, 0)),
            pl.BlockSpec((1, HEAD_DIM), lambda bb, i: (0, 0)),
        ],
        out_specs=pl.BlockSpec((None, tp, npre), lambda bb, i: (bb, i, 0)),
        out_shape=jax.ShapeDtypeStruct(p.shape, p.dtype),
        input_output_aliases={0: 0},
        compiler_params=_params(("arbitrary", "arbitrary")),
        name="prep",
    )(p, cos, sin_s, qn, kn)


VT_ROWS = HEAD_DIM + 16
SOFTMAX_ROWS = 128


def _ones_row_block(width):
    first = lax.broadcasted_iota(jnp.int32, (VT_ROWS - HEAD_DIM, width), 0) == 0
    return jnp.where(first, 1.0, 0.0).astype(BF16)


def _mixa_kernel(sink_ref, q_ref, z_ref, k_ref, v_ref, o_ref, vt_ref, s_ref, p_ref,
                 *, tq, nsub, ctx_len, s_lat):
    kv = pl.program_id(1)
    blk = pl.program_id(2)
    wl = tq + 2 * WINDOW
    nkeys = ctx_len + wl

    @pl.when(blk == 0)
    def _():
        def tbody(c, carry):
            r0 = pl.multiple_of(c * HEAD_DIM, HEAD_DIM)
            vt_ref[c, 0:HEAD_DIM, :] = v_ref[pl.ds(r0, HEAD_DIM), :].astype(F32).T.astype(BF16)
            vt_ref[c, HEAD_DIM:VT_ROWS, :] = _ones_row_block(HEAD_DIM)
            return carry

        lax.fori_loop(0, (ctx_len + s_lat) // HEAD_DIM, tbody, 0)

    def geometry(u):
        q0 = (blk * nsub + u) * tq - ctx_len
        cs = pl.multiple_of(jnp.clip(q0 - WINDOW, 0, s_lat - wl), HEAD_DIM)
        return q0, cs

    def scores(u):
        q0, cs = geometry(u)
        keys = jnp.concatenate([k_ref[0:ctx_len, :], k_ref[pl.ds(ctx_len + cs, wl), :]], axis=0)
        ss = [lax.dot_general(keys, q_ref[u * tq:(u + 1) * tq, g * HEAD_DIM:(g + 1) * HEAD_DIM], NT_DIMS,
                              preferred_element_type=F32) for g in range(GROUP)]
        ms = []
        for g in range(GROUP):
            s_ref[u, g, 0:ctx_len, :] = ss[g][0:ctx_len]
            ms.append(jnp.maximum(jnp.max(ss[g][0:ctx_len], axis=0, keepdims=True), sink_ref[kv * GROUP + g]))
        row = lax.broadcasted_iota(jnp.int32, (SOFTMAX_ROWS, tq), 0)
        col = lax.broadcasted_iota(jnp.int32, (SOFTMAX_ROWS, tq), 1)
        diff = row - col
        for r in range(0, wl, SOFTMAX_ROWS):
            rel = diff + (cs + r - q0)
            valid = jnp.logical_and(jnp.abs(rel) <= WINDOW, q0 >= 0)
            for g in range(GROUP):
                piece = jnp.where(valid, ss[g][ctx_len + r:ctx_len + r + SOFTMAX_ROWS], -jnp.inf)
                s_ref[u, g, ctx_len + r:ctx_len + r + SOFTMAX_ROWS, :] = piece
                ms[g] = jnp.maximum(ms[g], jnp.max(piece, axis=0, keepdims=True))
        return ms

    def softmax(u, ms):
        for g in range(GROUP):
            for r in range(0, nkeys, SOFTMAX_ROWS):
                p_ref[u, g, r:r + SOFTMAX_ROWS, :] = jnp.exp2(
                    s_ref[u, g, r:r + SOFTMAX_ROWS, :] - ms[g]).astype(BF16)

    def output(u, ms):
        _, cs = geometry(u)
        c0 = (ctx_len + cs) // HEAD_DIM
        tiles = [vt_ref[c] for c in range(ctx_len // HEAD_DIM)] + [vt_ref[c0 + c] for c in range(wl // HEAD_DIM)]
        vt = jnp.concatenate(tiles, axis=1)
        for g in range(GROUP):
            pv = jnp.dot(vt, p_ref[u, g], preferred_element_type=F32)
            den = pv[HEAD_DIM:HEAD_DIM + 1] + jnp.exp2(sink_ref[kv * GROUP + g] - ms[g])
            o = (pv[0:HEAD_DIM] / den).T
            z = z_ref[u * tq:(u + 1) * tq, g * HEAD_DIM:(g + 1) * HEAD_DIM].astype(F32)
            o_ref[u * tq:(u + 1) * tq, g * HEAD_DIM:(g + 1) * HEAD_DIM] = (o * _silu(z)).astype(o_ref.dtype)

    ms = {0: scores(0)}
    for u in range(nsub):
        if u + 1 < nsub:
            ms[u + 1] = scores(u + 1)
        softmax(u, ms[u])
        if u >= 1:
            output(u - 1, ms[u - 1])
    output(nsub - 1, ms[nsub - 1])


def _mixa(sink2, p, *, lay, ctx_len):
    b, t, _ = p.shape
    tq, nsub = 256, 3
    gw = GROUP * HEAD_DIM
    nkv = lay["akv"]
    s_lat = t - ctx_len
    wl = tq + 2 * WINDOW
    tb = tq * nsub
    assert t % tb == 0 and ctx_len % tq == 0 and s_lat >= wl and WINDOW == HEAD_DIM
    qoff, zoff, koff, voff = lay["qa"] // GROUP, lay["za"] // GROUP, lay["ka"], lay["va"]
    return pl.pallas_call(
        functools.partial(_mixa_kernel, tq=tq, nsub=nsub, ctx_len=ctx_len, s_lat=s_lat),
        grid=(b, nkv, t // tb),
        in_specs=[
            pl.BlockSpec(memory_space=pltpu.SMEM),
            pl.BlockSpec((None, tb, gw), lambda bb, k, i: (bb, i, qoff + k)),
            pl.BlockSpec((None, tb, gw), lambda bb, k, i: (bb, i, zoff + k)),
            pl.BlockSpec((None, t, HEAD_DIM), lambda bb, k, i: (bb, 0, koff + k)),
            pl.BlockSpec((None, t, HEAD_DIM), lambda bb, k, i: (bb, 0, voff + k)),
        ],
        out_specs=pl.BlockSpec((None, tb, gw), lambda bb, k, i: (bb, i, k)),
        out_shape=jax.ShapeDtypeStruct((b, t, lay["nh"] * HEAD_DIM), BF16),
        scratch_shapes=[pltpu.VMEM((t // HEAD_DIM, VT_ROWS, HEAD_DIM), BF16),
                        pltpu.VMEM((nsub, GROUP, ctx_len + wl, tq), F32),
                        pltpu.VMEM((nsub, GROUP, ctx_len + wl, tq), BF16)],
        compiler_params=_params(("arbitrary", "arbitrary", "arbitrary")),
        name="mixa",
    )(sink2, p, p, p, p)


def _mixc_kernel(q_ref, z_ref, k_ref, v_ref, oin_ref, o_ref,
                 vt_ref, s_ref, p_ref, a_ref, mc_ref, m_ref, acc_ref, *, tq, tk, ctx_len, nchunk):
    del oin_ref
    qi = pl.program_id(2)

    @pl.when(qi == 0)
    def _():
        def tbody(c, carry):
            r0 = pl.multiple_of(c * tk, tk)
            vt_ref[c, 0:HEAD_DIM, :] = v_ref[pl.ds(r0, tk), :].astype(F32).T.astype(BF16)
            vt_ref[c, HEAD_DIM:VT_ROWS, :] = _ones_row_block(tk)
            return carry

        lax.fori_loop(0, nchunk, tbody, 0)

    def q_head(g):
        return q_ref[:, g * HEAD_DIM:(g + 1) * HEAD_DIM]

    def scores(c, buf):
        r0 = pl.multiple_of(c * tk, tk)
        kc = k_ref[pl.ds(r0, tk), :]
        for g in range(GROUP):
            s = lax.dot_general(kc, q_head(g), NT_DIMS, preferred_element_type=F32)
            s_ref[buf, g] = s
            mc_ref[buf, g] = jnp.max(s, axis=0, keepdims=True)

    def softmax(buf, first):
        for g in range(GROUP):
            m_cur = mc_ref[buf, g]
            if first:
                m_new = m_cur
            else:
                m_prev = m_ref[g]
                m_new = jnp.maximum(m_prev, m_cur)
                a_ref[buf, g] = jnp.exp2(m_prev - m_new)
            m_ref[g] = m_new
            for r in range(0, tk, SOFTMAX_ROWS):
                p_ref[buf, g, r:r + SOFTMAX_ROWS, :] = jnp.exp2(
                    s_ref[buf, g, r:r + SOFTMAX_ROWS, :] - m_new).astype(BF16)

    def output(c, buf, first):
        vtc = vt_ref[c]
        for g in range(GROUP):
            pv = jnp.dot(vtc, p_ref[buf, g], preferred_element_type=F32)
            acc_ref[g] = pv if first else a_ref[buf, g] * acc_ref[g] + pv

    nctx = ctx_len // tq

    @pl.when(qi < nctx)
    def _():
        kc = k_ref[0:ctx_len, :]
        vtc = vt_ref[0, :, 0:ctx_len]
        for g in range(GROUP):
            s = lax.dot_general(kc, q_head(g), NT_DIMS, preferred_element_type=F32)
            p = jnp.exp2(s - jnp.max(s, axis=0, keepdims=True))
            acc_ref[g] = jnp.dot(vtc, p.astype(BF16), preferred_element_type=F32)

    @pl.when(qi >= nctx)
    def _():
        def step(i, par, first=False):
            scores(i + 1, 1 - par)
            softmax(par, False)
            output(i - 1, 1 - par, first)

        scores(0, 0)
        scores(1, 1)
        softmax(0, True)
        mid = nchunk - 2
        if mid >= 1:
            step(1, 1, first=True)
        pairs = (mid - 1) // 2 if mid >= 1 else 0

        def body(j, carry):
            i = 2 + 2 * j
            step(i, 0)
            step(i + 1, 1)
            return carry

        lax.fori_loop(0, pairs, body, 0)
        if mid >= 1 and (mid - 1) % 2 == 1:
            step(nchunk - 2, (nchunk - 2) % 2)
        last = nchunk - 1
        softmax(last % 2, False)
        output(last - 1, (last - 1) % 2, last == 1)
        output(last, last % 2, False)

    o = jnp.concatenate([(acc_ref[g, 0:HEAD_DIM, :] / acc_ref[g, HEAD_DIM:HEAD_DIM + 1, :]).T
                         for g in range(GROUP)], axis=1)
    o_ref[...] = (o * _silu(z_ref[...].astype(F32))).astype(o_ref.dtype)


def _mixc(p, o_prev, *, lay, ctx_len):
    b, t, _ = p.shape
    tq, tk = 256, 768
    gw = GROUP * HEAD_DIM
    nkv = lay["ckv"]
    assert t % tq == 0 and ctx_len % tq == 0 and t % tk == 0 and ctx_len <= tk
    qoff, zoff, koff, voff = lay["qc"] // GROUP, lay["zc"] // GROUP, lay["kc"], lay["vc"]
    ooff = lay["oc"] // GROUP
    return pl.pallas_call(
        functools.partial(_mixc_kernel, tq=tq, tk=tk, ctx_len=ctx_len, nchunk=t // tk),
        grid=(b, nkv, t // tq),
        in_specs=[
            pl.BlockSpec((None, tq, gw), lambda bb, k, i: (bb, i, qoff + k)),
            pl.BlockSpec((None, tq, gw), lambda bb, k, i: (bb, i, zoff + k)),
            pl.BlockSpec((None, t, HEAD_DIM), lambda bb, k, i: (bb, 0, koff + k)),
            pl.BlockSpec((None, t, HEAD_DIM), lambda bb, k, i: (bb, 0, voff + k)),
            pl.BlockSpec(memory_space=pl.ANY),
        ],
        out_specs=pl.BlockSpec((None, tq, gw), lambda bb, k, i: (bb, i, ooff + k)),
        out_shape=jax.ShapeDtypeStruct(o_prev.shape, o_prev.dtype),
        input_output_aliases={4: 0},
        scratch_shapes=[pltpu.VMEM((t // tk, VT_ROWS, tk), BF16),
                        pltpu.VMEM((2, GROUP, tk, tq), F32), pltpu.VMEM((2, GROUP, tk, tq), BF16),
                        pltpu.VMEM((2, GROUP, 1, tq), F32), pltpu.VMEM((2, GROUP, 1, tq), F32),
                        pltpu.VMEM((GROUP, 1, tq), F32),
                        pltpu.VMEM((GROUP, VT_ROWS, tq), F32)],
        compiler_params=_params(("arbitrary", "arbitrary", "arbitrary")),
        name="mixc",
    )(p, p, p, p, o_prev)


def _ret_state_kernel(gc_ref, k_ref, v_ref, wk_ref, o_ref, s_ref, *, nh):
    d = pl.program_id(1)
    step = pl.program_id(2)

    @pl.when(step == 0)
    def _():
        s_ref[...] = jnp.zeros(s_ref.shape, F32)

    wk = wk_ref[...]
    for it in range(2):
        ci = d if it == 0 else 1 - d
        r0 = pl.multiple_of(ci * RET_CHUNK, RET_CHUNK)
        k = (k_ref[pl.ds(r0, RET_CHUNK), :].astype(F32) * wk).astype(BF16)
        v = v_ref[pl.ds(r0, RET_CHUNK), :]
        o_ref[ci] = s_ref[...].astype(o_ref.dtype)
        for h in range(nh):
            hs = slice(h * HEAD_DIM, (h + 1) * HEAD_DIM)
            kv = lax.dot_general(k[:, hs], v[:, hs], TN_DIMS, preferred_element_type=F32)
            s_ref[hs, :] = gc_ref[d, h] * s_ref[hs, :] + kv


def _ret_states(gc, p, wk, *, lay):
    b, t, _ = p.shape
    nh = lay["bh"]
    bw = nh * HEAD_DIM
    tb = 2 * RET_CHUNK
    nblk = t // tb
    koff, voff = lay["kb"] // nh, lay["vb"] // nh
    assert t % tb == 0

    def blk(d, s):
        return jnp.where(d == 0, s, jnp.where(s == 0, 0, nblk - s))

    return pl.pallas_call(
        functools.partial(_ret_state_kernel, nh=nh),
        grid=(b, 2, nblk),
        in_specs=[
            pl.BlockSpec(memory_space=pltpu.SMEM),
            pl.BlockSpec((None, tb, bw), lambda bb, d, s: (bb, blk(d, s), koff)),
            pl.BlockSpec((None, tb, bw), lambda bb, d, s: (bb, blk(d, s), voff)),
            pl.BlockSpec((None, RET_CHUNK, bw), lambda bb, d, s: (d, 0, 0)),
        ],
        out_specs=pl.BlockSpec((None, None, 2, bw, HEAD_DIM),
                               lambda bb, d, s: (bb, d, blk(d, s), 0, 0)),
        out_shape=jax.ShapeDtypeStruct((b, 2, t // RET_CHUNK, bw, HEAD_DIM), BF16),
        scratch_shapes=[pltpu.VMEM((bw, HEAD_DIM), F32)],
        compiler_params=_params(("arbitrary", "arbitrary", "arbitrary")),
        name="ret_states",
    )(gc, p, p, wk)


def _ret_out_kernel(q_ref, k_ref, v_ref, z_ref, sp_ref, dm_ref, wq_ref, oin_ref, o_ref, *, nh):
    del oin_ref
    for c in range(2):
        rs = slice(c * RET_CHUNK, (c + 1) * RET_CHUNK)
        for h in range(nh):
            hs = slice(h * HEAD_DIM, (h + 1) * HEAD_DIM)
            q = q_ref[rs, hs]
            a = lax.dot_general(q, k_ref[rs, hs], NT_DIMS, preferred_element_type=F32) * dm_ref[h]
            q32 = q.astype(F32)
            lhs = jnp.concatenate([a.astype(BF16),
                                   (q32 * wq_ref[0, :, hs]).astype(BF16),
                                   (q32 * wq_ref[1, :, hs]).astype(BF16)], axis=1)
            rhs = jnp.concatenate([v_ref[rs, hs], sp_ref[0, c, hs, :], sp_ref[1, c, hs, :]], axis=0)
            o = jnp.dot(lhs, rhs, preferred_element_type=F32)
            mu = jnp.mean(o, axis=-1, keepdims=True)
            oc = o - mu
            var = jnp.mean(oc * oc, axis=-1, keepdims=True)
            y = oc * lax.rsqrt(var + EPS)
            o_ref[rs, hs] = (y * _silu(z_ref[rs, hs].astype(F32))).astype(o_ref.dtype)


def _ret_out(p, sprev, dmat, wq, o_prev, *, lay):
    b, t, _ = p.shape
    nh = lay["bh"]
    bw = nh * HEAD_DIM
    tb = 2 * RET_CHUNK
    qoff, koff, voff, zoff = lay["qb"] // nh, lay["kb"] // nh, lay["vb"] // nh, lay["zb"] // nh
    ooff = lay["ob"] // nh
    return pl.pallas_call(
        functools.partial(_ret_out_kernel, nh=nh),
        grid=(b, t // tb),
        in_specs=[
            pl.BlockSpec((None, tb, bw), lambda bb, i: (bb, i, qoff)),
            pl.BlockSpec((None, tb, bw), lambda bb, i: (bb, i, koff)),
            pl.BlockSpec((None, tb, bw), lambda bb, i: (bb, i, voff)),
            pl.BlockSpec((None, tb, bw), lambda bb, i: (bb, i, zoff)),
            pl.BlockSpec((None, 2, 2, bw, HEAD_DIM), lambda bb, i: (bb, 0, i, 0, 0)),
            pl.BlockSpec((nh, RET_CHUNK, RET_CHUNK), lambda bb, i: (0, 0, 0)),
            pl.BlockSpec((2, RET_CHUNK, bw), lambda bb, i: (0, 0, 0)),
            pl.BlockSpec(memory_space=pl.ANY),
        ],
        out_specs=pl.BlockSpec((None, tb, bw), lambda bb, i: (bb, i, ooff)),
        out_shape=jax.ShapeDtypeStruct(o_prev.shape, o_prev.dtype),
        input_output_aliases={7: 0},
        compiler_params=_params(("arbitrary", "arbitrary")),
        name="ret_out",
    )(p, p, p, p, sprev, dmat, wq, o_prev)


def _outproj_kernel(o_ref, w_ref, x_ref, gl_ref, gc_ref, gp_ref, y_ref, *, tm, rc, first_lat, segs):
    i = pl.program_id(1)
    o = jnp.concatenate([o_ref[:, a:b] for a, b in segs], axis=1)
    y_ref[...] = jnp.dot(o, w_ref[...], preferred_element_type=F32)
    gate = jnp.where(i < first_lat, gc_ref[...], gl_ref[...])
    gp = gp_ref[...]

    def body(c, carry):
        r0 = pl.multiple_of(c * rc, rc)
        y = y_ref[pl.ds(r0, rc), :]
        ms = jnp.mean(y * y, axis=-1, keepdims=True)
        yn = y * lax.rsqrt(ms + EPS) * gp
        y_ref[pl.ds(r0, rc), :] = x_ref[pl.ds(r0, rc), :] + gate * yn
        return carry

    lax.fori_loop(0, tm // rc, body, 0)


def _outproj(o, w, layer, xs, gate_lat, gate_ctx, g_post, *, ctx_len, segs, latent_only):
    b, t, d = xs.shape
    kdim = o.shape[-1]
    tm, rc = 256, 32
    assert t % tm == 0 and ctx_len % tm == 0 and w.shape[1] == kdim
    nctx = ctx_len // tm
    off = nctx if latent_only else 0
    first_lat = 0 if latent_only else nctx
    t_out = t - off * tm
    return pl.pallas_call(
        functools.partial(_outproj_kernel, tm=tm, rc=rc, first_lat=first_lat, segs=segs),
        grid=(b, t_out // tm),
        in_specs=[
            pl.BlockSpec((None, tm, kdim), lambda bb, i: (bb, i + off, 0)),
            pl.BlockSpec((None, kdim, d), lambda bb, i: (layer, 0, 0), pipeline_mode=pl.Buffered(1)),
            pl.BlockSpec((None, tm, d), lambda bb, i: (bb, i + off, 0)),
            pl.BlockSpec((None, 1, d), lambda bb, i: (bb, 0, 0)),
            pl.BlockSpec((1, d), lambda bb, i: (0, 0)),
            pl.BlockSpec((1, d), lambda bb, i: (0, 0)),
        ],
        out_specs=pl.BlockSpec((None, tm, d), lambda bb, i: (bb, i, 0)),
        out_shape=jax.ShapeDtypeStruct((b, t_out, d), xs.dtype),
        compiler_params=pltpu.CompilerParams(dimension_semantics=("arbitrary", "arbitrary"),
                                             vmem_limit_bytes=OUTPROJ_VMEM_LIMIT),
        name="outproj",
    )(o, w, xs, gate_lat, gate_ctx, g_post)


def _layout(d_model):
    nh = d_model // HEAD_DIM
    aq = 3 * nh // 8
    akv = aq // GROUP
    bh = nh // 4
    cq = nh - aq - bh
    ckv = cq // GROUP
    order = [("qa", aq), ("qc", cq), ("ka", akv), ("kc", ckv), ("qb", bh), ("kb", bh),
             ("za", aq), ("zc", cq), ("zb", bh), ("va", akv), ("vc", ckv), ("vb", bh)]
    lay = {"nh": nh, "aq": aq, "akv": akv, "bh": bh, "cq": cq, "ckv": ckv}
    off = 0
    for name, width in order:
        lay[name] = off
        off += width
    lay["nprep"] = lay["za"]
    lay["oa"], lay["oc"], lay["ob"] = 0, aq, aq + cq
    ref_order = [("qa", aq), ("ka", akv), ("va", akv), ("qb", bh), ("kb", bh), ("vb", bh),
                 ("qc", cq), ("kc", ckv), ("vc", ckv), ("za", aq), ("zb", bh), ("zc", cq)]
    ref_off, off = {}, 0
    for name, width in ref_order:
        ref_off[name] = off
        off += width
    lay["o_segs"] = ((0, aq * HEAD_DIM), ((aq + cq) * HEAD_DIM, nh * HEAD_DIM),
                     (aq * HEAD_DIM, (aq + cq) * HEAD_DIM))
    tu = math.gcd(4, *[width for _, width in order])
    lay["tiles"] = {"tn": tu * HEAD_DIM,
                    "src": tuple((lay[name] // tu, ref_off[name] // tu) for name, _ in order)}
    scale = HEAD_DIM ** -0.5
    lay["kinds"] = tuple(
        [(None, scale * LOG2E)] * aq + [("q", scale * LOG2E)] * cq + [(None, 1.0)] * akv
        + [("k", 1.0)] * ckv + [(None, 1.0)] * bh + [(None, scale)] * bh)
    return lay


def _rope_tables(n_lat, ctx_len):
    rows = n_lat // GRID_W
    row = jnp.repeat(jnp.arange(rows, dtype=F32), GRID_W)
    col = jnp.tile(jnp.arange(GRID_W, dtype=F32), rows)
    axis_dim = HEAD_DIM // 2
    inv = ROPE_BASE ** (-jnp.arange(0, axis_dim, 2, dtype=F32) / axis_dim)
    ang_r = row[:, None] * inv
    ang_c = col[:, None] * inv
    ang = jnp.concatenate([ang_r, ang_r, ang_c, ang_c], axis=-1)
    sign = jnp.where((jnp.arange(HEAD_DIM) // 32) % 2 == 1, 1.0, -1.0).astype(F32)
    cos = jnp.concatenate([jnp.ones((ctx_len, HEAD_DIM), F32), jnp.cos(ang)], axis=0)
    sin_s = jnp.concatenate([jnp.zeros((ctx_len, HEAD_DIM), F32), jnp.sin(ang) * sign], axis=0)
    return cos, sin_s


def _decay_tables(ret_decay_l):
    log_g = jax.nn.log_sigmoid(ret_decay_l.astype(F32))
    nh = log_g.shape[1]
    pos = jnp.arange(RET_CHUNK, dtype=F32)
    lane = lambda w: jnp.repeat(w, HEAD_DIM, axis=-1)
    wk_f = jnp.exp((RET_CHUNK - 1 - pos)[:, None] * log_g[0][None])
    wk_b = jnp.exp(pos[:, None] * log_g[1][None])
    wq_f = jnp.exp((pos + 1.0)[:, None] * log_g[0][None])
    wq_b = jnp.exp((RET_CHUNK - pos)[:, None] * log_g[1][None])
    wk = jnp.stack([lane(wk_f), lane(wk_b)])
    wq = jnp.stack([lane(wq_f), lane(wq_b)])
    gc = jnp.exp(RET_CHUNK * log_g)
    diff = pos[:, None] - pos[None, :]
    ef = jnp.where((diff >= 0)[None], diff[None] * log_g[0][:, None, None], -jnp.inf)
    eb = jnp.where((diff <= 0)[None], -diff[None] * log_g[1][:, None, None], -jnp.inf)
    dmat = jnp.exp(ef) + jnp.exp(eb)
    del nh
    return wk, wq, gc, dmat


def kernel(x, c, ctx, c_ctx, w_ada, b_ada, g_pre, g_post, w_in, w_out, sink_a, qnorm_c, knorm_c, ret_decay):
    bsz, n_lat, d = x.shape
    ctx_len = ctx.shape[1]
    depth = w_ada.shape[0]
    lay = _layout(d)

    cond_rows = jnp.zeros((8, d), F32).at[0:bsz].set(c).at[bsz].set(c_ctx)
    mods = _ada(cond_rows, w_ada, b_ada)
    cos, sin_s = _rope_tables(n_lat, ctx_len)
    w_in_p = w_in.astype(BF16)
    w_out_p = w_out.astype(BF16)

    xs = jnp.concatenate([ctx, x], axis=1)
    for l in range(depth):
        mod_lat = mods[l, 0:bsz].reshape(bsz, 1, 3 * d)
        mod_ctx = mods[l, bsz:bsz + 1]
        p = _inproj(xs, g_pre[l].reshape(1, d), mod_lat, mod_ctx, w_in_p, l, lay["tiles"], ctx_len=ctx_len)
        p = _prep(p, cos, sin_s, qnorm_c[l].reshape(1, HEAD_DIM), knorm_c[l].reshape(1, HEAD_DIM),
                  kinds=lay["kinds"])
        o = _mixa(sink_a[l] * LOG2E, p, lay=lay, ctx_len=ctx_len)
        o = _mixc(p, o, lay=lay, ctx_len=ctx_len)
        wk, wq, gc, dmat = _decay_tables(ret_decay[l])
        sprev = _ret_states(gc, p, wk, lay=lay)
        o = _ret_out(p, sprev, dmat, wq, o, lay=lay)
        xs = _outproj(o, w_out_p, l, xs, mod_lat[:, :, 2 * d:], mod_ctx[:, 2 * d:],
                      g_post[l].reshape(1, d), ctx_len=ctx_len, segs=lay["o_segs"],
                      latent_only=(l == depth - 1))
    return xs
```

```python
import functools
import math

import jax
import jax.numpy as jnp
from jax import lax
from jax.experimental import pallas as pl
from jax.experimental.pallas import tpu as pltpu

HEAD_DIM = 128
GRID_W = 64
WINDOW = 128
RET_CHUNK = 128
ROPE_BASE = 10000.0
EPS = 1e-6
LOG2E = math.log2(math.e)
GROUP = 3
VMEM_LIMIT = 56 * 1024 * 1024
OUTPROJ_VMEM_LIMIT = 60 * 1024 * 1024

F32 = jnp.float32
BF16 = jnp.bfloat16
NT_DIMS = (((1,), (1,)), ((), ()))
TN_DIMS = (((0,), (0,)), ((), ()))


def _params(sem):
    return pltpu.CompilerParams(dimension_semantics=sem, vmem_limit_bytes=VMEM_LIMIT)


def _silu(z):
    return z / (1.0 + jnp.exp(-z))


def _ada_kernel(c_ref, w_ref, b_ref, o_ref):
    cond = _silu(c_ref[...])
    o_ref[...] = jnp.dot(cond.astype(BF16), w_ref[...].astype(BF16),
                         preferred_element_type=F32) + b_ref[...]


def _ada(cond_rows, w_ada, b_ada):
    depth, d, n3 = w_ada.shape
    tn = 512
    return pl.pallas_call(
        _ada_kernel,
        grid=(depth, n3 // tn),
        in_specs=[
            pl.BlockSpec((8, d), lambda l, j: (0, 0)),
            pl.BlockSpec((None, d, tn), lambda l, j: (l, 0, j)),
            pl.BlockSpec((None, 1, tn), lambda l, j: (l, 0, j)),
        ],
        out_specs=pl.BlockSpec((None, 8, tn), lambda l, j: (l, 0, j)),
        out_shape=jax.ShapeDtypeStruct((depth, 8, n3), F32),
        compiler_params=_params(("arbitrary", "arbitrary")),
        name="ada",
    )(cond_rows, w_ada, b_ada.reshape(depth, 1, n3))


def _inproj_kernel(x_ref, g_ref, ml_ref, mc_ref, *refs, tm, tn, rc, ctx_len, d):
    w_refs, o_ref, h_ref = refs[:-2], refs[-2], refs[-1]
    i = pl.program_id(1)
    j = pl.program_id(2)

    @pl.when(j == 0)
    def _():
        g = g_ref[...]
        sh_l = ml_ref[:, 0:d]
        sc_l = 1.0 + ml_ref[:, d:2 * d]
        sh_c = mc_ref[:, 0:d]
        sc_c = 1.0 + mc_ref[:, d:2 * d]

        def body(c, carry):
            r0 = pl.multiple_of(c * rc, rc)
            is_ctx = jnp.logical_and(i == 0, r0 < ctx_len)
            sh = jnp.where(is_ctx, sh_c, sh_l)
            sc = jnp.where(is_ctx, sc_c, sc_l)
            x = x_ref[pl.ds(r0, rc), :]
            ms = jnp.mean(x * x, axis=-1, keepdims=True)
            y = x * lax.rsqrt(ms + EPS) * g
            h_ref[pl.ds(r0, rc), :] = (y * sc + sh).astype(BF16)
            return carry

        lax.fori_loop(0, tm // rc, body, 0, unroll=2)

    for u, w_ref in enumerate(w_refs):
        o_ref[:, u * tn:(u + 1) * tn] = jnp.dot(
            h_ref[...], w_ref[...], preferred_element_type=F32).astype(o_ref.dtype)


def _inproj(xs, g_pre, mod_lat, mod_ctx, w, layer, tiles, *, ctx_len):
    b, t, d = xs.shape
    n = w.shape[-1]
    tm, rc = 768, 32
    tn = tiles["tn"]
    src = tiles["src"]
    nw = 2
    assert t % tm == 0 and n % (nw * tn) == 0 and ctx_len % rc == 0 and ctx_len <= tm

    def src_tile(j):
        out = j - src[0][0] + src[0][1]
        for start, ref_start in src[1:]:
            out = jnp.where(j >= start, j - start + ref_start, out)
        return out

    def w_spec(u):
        return pl.BlockSpec((None, d, tn), lambda bb, i, j: (layer, 0, src_tile(nw * j + u)))

    return pl.pallas_call(
        functools.partial(_inproj_kernel, tm=tm, tn=tn, rc=rc, ctx_len=ctx_len, d=d),
        grid=(b, t // tm, n // (nw * tn)),
        in_specs=[
            pl.BlockSpec((None, tm, d), lambda bb, i, j: (bb, i, 0)),
            pl.BlockSpec((1, d), lambda bb, i, j: (0, 0)),
            pl.BlockSpec((None, 1, 3 * d), lambda bb, i, j: (bb, 0, 0)),
            pl.BlockSpec((1, 3 * d), lambda bb, i, j: (0, 0)),
        ] + [w_spec(u) for u in range(nw)],
        out_specs=pl.BlockSpec((None, tm, nw * tn), lambda bb, i, j: (bb, i, j)),
        out_shape=jax.ShapeDtypeStruct((b, t, n), BF16),
        scratch_shapes=[pltpu.VMEM((tm, d), BF16)],
        compiler_params=_params(("arbitrary", "arbitrary", "arbitrary")),
        name="inproj",
    )(xs, g_pre, mod_lat, mod_ctx, *([w] * nw))


def _prep_kernel(p_ref, cos_ref, sin_ref, qn_ref, kn_ref, swap_ref, o_ref, *, kinds):
    cos = cos_ref[...]
    sin = sin_ref[...]
    first = {None: cos, "q": cos * qn_ref[0:1, :], "k": cos * kn_ref[0:1, :]}
    second = {None: sin, "q": sin * qn_ref[1:2, :], "k": sin * kn_ref[1:2, :]}
    swap = swap_ref[...]
    for t2 in range(len(kinds) // 2):
        pair = p_ref[:, 2 * t2 * HEAD_DIM:(2 * t2 + 2) * HEAD_DIM]
        swapped = jnp.dot(pair, swap, preferred_element_type=F32)
        for h in range(2):
            norm, scale = kinds[2 * t2 + h]
            hs = slice(h * HEAD_DIM, (h + 1) * HEAD_DIM)
            x = pair[:, hs].astype(F32)
            y = x * first[norm] + swapped[:, hs] * second[norm]
            if norm is not None:
                y = y * lax.rsqrt(jnp.mean(x * x, axis=-1, keepdims=True) + EPS)
            if scale != 1.0:
                y = y * scale
            t = 2 * t2 + h
            o_ref[:, t * HEAD_DIM:(t + 1) * HEAD_DIM] = y.astype(o_ref.dtype)


def _prep(p, cos, sin_s, qn, kn, *, kinds):
    b, t, n = p.shape
    tp = 256
    npre = len(kinds) * HEAD_DIM
    assert t % tp == 0 and len(kinds) % 2 == 0
    lanes = jnp.arange(2 * HEAD_DIM)
    swap = (lanes[:, None] == (lanes[None, :] ^ 32)).astype(BF16)
    with_swapped = lambda g: jnp.stack([g, g[jnp.arange(HEAD_DIM) ^ 32]])
    return pl.pallas_call(
        functools.partial(_prep_kernel, kinds=kinds),
        grid=(b, t // tp),
        in_specs=[
            pl.BlockSpec((None, tp, npre), lambda bb, i: (bb, i, 0)),
            pl.BlockSpec((tp, HEAD_DIM), lambda bb, i: (i, 0)),
            pl.BlockSpec((tp, HEAD_DIM), lambda bb, i: (i, 0)),
            pl.BlockSpec((2, HEAD_DIM), lambda bb, i: (0, 0)),
            pl.BlockSpec((2, HEAD_DIM), lambda bb, i: (0, 0)),
            pl.BlockSpec((2 * HEAD_DIM, 2 * HEAD_DIM), lambda bb, i: (0, 0)),
        ],
        out_specs=pl.BlockSpec((None, tp, npre), lambda bb, i: (bb, i, 0)),
        out_shape=jax.ShapeDtypeStruct(p.shape, p.dtype),
        input_output_aliases={0: 0},
        compiler_params=_params(("arbitrary", "arbitrary")),
        name="prep",
    )(p, cos, sin_s, with_swapped(qn), with_swapped(kn), swap)


VT_ROWS = HEAD_DIM + 16
SOFTMAX_ROWS = 128
STEPS_PER_TRIP = 2


def _ones_row_block(width):
    first = lax.broadcasted_iota(jnp.int32, (VT_ROWS - HEAD_DIM, width), 0) == 0
    return jnp.where(first, 1.0, 0.0).astype(BF16)


def _mixa_kernel(sink_ref, q_ref, z_ref, k_ref, v_ref, o_ref, vt_ref, s_ref, p_ref,
                 *, tq, nsub, ctx_len, s_lat):
    kv = pl.program_id(1)
    blk = pl.program_id(2)
    wl = tq + 2 * WINDOW
    nkeys = ctx_len + wl

    @pl.when(blk == 0)
    def _():
        def tbody(c, carry):
            r0 = pl.multiple_of(c * HEAD_DIM, HEAD_DIM)
            vt_ref[c, 0:HEAD_DIM, :] = v_ref[pl.ds(r0, HEAD_DIM), :].astype(F32).T.astype(BF16)
            vt_ref[c, HEAD_DIM:VT_ROWS, :] = _ones_row_block(HEAD_DIM)
            return carry

        lax.fori_loop(0, (ctx_len + s_lat) // HEAD_DIM, tbody, 0)

    def geometry(u):
        q0 = (blk * nsub + u) * tq - ctx_len
        cs = pl.multiple_of(jnp.clip(q0 - WINDOW, 0, s_lat - wl), HEAD_DIM)
        return q0, cs

    def scores(u):
        q0, cs = geometry(u)
        keys = jnp.concatenate([k_ref[0:ctx_len, :], k_ref[pl.ds(ctx_len + cs, wl), :]], axis=0)
        ss = [lax.dot_general(keys, q_ref[u * tq:(u + 1) * tq, g * HEAD_DIM:(g + 1) * HEAD_DIM], NT_DIMS,
                              preferred_element_type=F32) for g in range(GROUP)]
        ms = []
        for g in range(GROUP):
            s_ref[u, g, 0:ctx_len, :] = ss[g][0:ctx_len]
            ms.append(jnp.maximum(jnp.max(ss[g][0:ctx_len], axis=0, keepdims=True), sink_ref[kv * GROUP + g]))
        row = lax.broadcasted_iota(jnp.int32, (SOFTMAX_ROWS, tq), 0)
        col = lax.broadcasted_iota(jnp.int32, (SOFTMAX_ROWS, tq), 1)
        diff = row - col
        for r in range(0, wl, SOFTMAX_ROWS):
            rel = diff + (cs + r - q0)
            valid = jnp.logical_and(jnp.abs(rel) <= WINDOW, q0 >= 0)
            for g in range(GROUP):
                piece = jnp.where(valid, ss[g][ctx_len + r:ctx_len + r + SOFTMAX_ROWS], -jnp.inf)
                s_ref[u, g, ctx_len + r:ctx_len + r + SOFTMAX_ROWS, :] = piece
                ms[g] = jnp.maximum(ms[g], jnp.max(piece, axis=0, keepdims=True))
        return ms

    def softmax(u, ms):
        for g in range(GROUP):
            for r in range(0, nkeys, SOFTMAX_ROWS):
                p_ref[u, g, r:r + SOFTMAX_ROWS, :] = jnp.exp2(
                    s_ref[u, g, r:r + SOFTMAX_ROWS, :] - ms[g]).astype(BF16)

    def output(u, ms):
        _, cs = geometry(u)
        c0 = (ctx_len + cs) // HEAD_DIM
        tiles = [vt_ref[c] for c in range(ctx_len // HEAD_DIM)] + [vt_ref[c0 + c] for c in range(wl // HEAD_DIM)]
        vt = jnp.concatenate(tiles, axis=1)
        for g in range(GROUP):
            pv = jnp.dot(vt, p_ref[u, g], preferred_element_type=F32)
            den = pv[HEAD_DIM:HEAD_DIM + 1] + jnp.exp2(sink_ref[kv * GROUP + g] - ms[g])
            o = (pv[0:HEAD_DIM] / den).T
            z = z_ref[u * tq:(u + 1) * tq, g * HEAD_DIM:(g + 1) * HEAD_DIM].astype(F32)
            o_ref[u * tq:(u + 1) * tq, g * HEAD_DIM:(g + 1) * HEAD_DIM] = (o * _silu(z)).astype(o_ref.dtype)

    ms = {0: scores(0)}
    for u in range(nsub):
        if u + 1 < nsub:
            ms[u + 1] = scores(u + 1)
        softmax(u, ms[u])
        if u >= 1:
            output(u - 1, ms[u - 1])
    output(nsub - 1, ms[nsub - 1])


def _mixa(sink2, p, *, lay, ctx_len):
    b, t, _ = p.shape
    tq, nsub = 256, 3
    gw = GROUP * HEAD_DIM
    nkv = lay["akv"]
    s_lat = t - ctx_len
    wl = tq + 2 * WINDOW
    tb = tq * nsub
    assert t % tb == 0 and ctx_len % tq == 0 and s_lat >= wl and WINDOW == HEAD_DIM
    qoff, zoff, koff, voff = lay["qa"] // GROUP, lay["za"] // GROUP, lay["ka"], lay["va"]
    return pl.pallas_call(
        functools.partial(_mixa_kernel, tq=tq, nsub=nsub, ctx_len=ctx_len, s_lat=s_lat),
        grid=(b, nkv, t // tb),
        in_specs=[
            pl.BlockSpec(memory_space=pltpu.SMEM),
            pl.BlockSpec((None, tb, gw), lambda bb, k, i: (bb, i, qoff + k)),
            pl.BlockSpec((None, tb, gw), lambda bb, k, i: (bb, i, zoff + k)),
            pl.BlockSpec((None, t, HEAD_DIM), lambda bb, k, i: (bb, 0, koff + k)),
            pl.BlockSpec((None, t, HEAD_DIM), lambda bb, k, i: (bb, 0, voff + k)),
        ],
        out_specs=pl.BlockSpec((None, tb, gw), lambda bb, k, i: (bb, i, k)),
        out_shape=jax.ShapeDtypeStruct((b, t, lay["nh"] * HEAD_DIM), BF16),
        scratch_shapes=[pltpu.VMEM((t // HEAD_DIM, VT_ROWS, HEAD_DIM), BF16),
                        pltpu.VMEM((nsub, GROUP, ctx_len + wl, tq), F32),
                        pltpu.VMEM((nsub, GROUP, ctx_len + wl, tq), BF16)],
        compiler_params=_params(("arbitrary", "arbitrary", "arbitrary")),
        name="mixa",
    )(sink2, p, p, p, p)


def _mixc_kernel(q_ref, z_ref, k_ref, v_ref, oin_ref, o_ref,
                 vt_ref, s_ref, p_ref, a_ref, mc_ref, m_ref, acc_ref, *, tq, tk, ctx_len, nchunk):
    del oin_ref
    qi = pl.program_id(2)

    @pl.when(qi == 0)
    def _():
        def tbody(c, carry):
            r0 = pl.multiple_of(c * tk, tk)
            vt_ref[c, 0:HEAD_DIM, :] = v_ref[pl.ds(r0, tk), :].astype(F32).T.astype(BF16)
            vt_ref[c, HEAD_DIM:VT_ROWS, :] = _ones_row_block(tk)
            return carry

        lax.fori_loop(0, nchunk, tbody, 0)

    def q_head(g):
        return q_ref[:, g * HEAD_DIM:(g + 1) * HEAD_DIM]

    def scores(c, buf):
        r0 = pl.multiple_of(c * tk, tk)
        kc = k_ref[pl.ds(r0, tk), :]
        for g in range(GROUP):
            s = lax.dot_general(kc, q_head(g), NT_DIMS, preferred_element_type=F32)
            s_ref[buf, g] = s
            mc_ref[buf, g] = jnp.max(s, axis=0, keepdims=True)

    def softmax(buf, first):
        for g in range(GROUP):
            m_cur = mc_ref[buf, g]
            if first:
                m_new = m_cur
            else:
                m_prev = m_ref[g]
                m_new = jnp.maximum(m_prev, m_cur)
                a_ref[buf, g] = jnp.exp2(m_prev - m_new)
            m_ref[g] = m_new
            for r in range(0, tk, SOFTMAX_ROWS):
                p_ref[buf, g, r:r + SOFTMAX_ROWS, :] = jnp.exp2(
                    s_ref[buf, g, r:r + SOFTMAX_ROWS, :] - m_new).astype(BF16)

    def output(c, buf, first):
        vtc = vt_ref[c]
        for g in range(GROUP):
            pv = jnp.dot(vtc, p_ref[buf, g], preferred_element_type=F32)
            acc_ref[g] = pv if first else a_ref[buf, g] * acc_ref[g] + pv

    nctx = ctx_len // tq

    @pl.when(qi < nctx)
    def _():
        kc = k_ref[0:ctx_len, :]
        vtc = vt_ref[0, :, 0:ctx_len]
        for g in range(GROUP):
            s = lax.dot_general(kc, q_head(g), NT_DIMS, preferred_element_type=F32)
            p = jnp.exp2(s - jnp.max(s, axis=0, keepdims=True))
            acc_ref[g] = jnp.dot(vtc, p.astype(BF16), preferred_element_type=F32)

    @pl.when(qi >= nctx)
    def _():
        def step(i, par, first=False):
            scores(i + 1, 1 - par)
            softmax(par, False)
            output(i - 1, 1 - par, first)

        scores(0, 0)
        scores(1, 1)
        softmax(0, True)
        mid = nchunk - 2
        if mid >= 1:
            step(1, 1, first=True)
        rest = max(mid - 1, 0)
        trips = rest // STEPS_PER_TRIP

        def body(j, carry):
            i = 2 + STEPS_PER_TRIP * j
            for k in range(STEPS_PER_TRIP):
                step(i + k, k % 2)
            return carry

        lax.fori_loop(0, trips, body, 0)
        for i in range(2 + trips * STEPS_PER_TRIP, nchunk - 1):
            step(i, i % 2)
        last = nchunk - 1
        softmax(last % 2, False)
        output(last - 1, (last - 1) % 2, last == 1)
        output(last, last % 2, False)

    o = jnp.concatenate([(acc_ref[g, 0:HEAD_DIM, :] / acc_ref[g, HEAD_DIM:HEAD_DIM + 1, :]).T
                         for g in range(GROUP)], axis=1)
    o_ref[...] = (o * _silu(z_ref[...].astype(F32))).astype(o_ref.dtype)


def _mixc(p, o_prev, *, lay, ctx_len):
    b, t, _ = p.shape
    tq, tk = 256, 768
    gw = GROUP * HEAD_DIM
    nkv = lay["ckv"]
    assert t % tq == 0 and ctx_len % tq == 0 and t % tk == 0 and ctx_len <= tk
    qoff, zoff, koff, voff = lay["qc"] // GROUP, lay["zc"] // GROUP, lay["kc"], lay["vc"]
    ooff = lay["oc"] // GROUP
    return pl.pallas_call(
        functools.partial(_mixc_kernel, tq=tq, tk=tk, ctx_len=ctx_len, nchunk=t // tk),
        grid=(b, nkv, t // tq),
        in_specs=[
            pl.BlockSpec((None, tq, gw), lambda bb, k, i: (bb, i, qoff + k)),
            pl.BlockSpec((None, tq, gw), lambda bb, k, i: (bb, i, zoff + k)),
            pl.BlockSpec((None, t, HEAD_DIM), lambda bb, k, i: (bb, 0, koff + k)),
            pl.BlockSpec((None, t, HEAD_DIM), lambda bb, k, i: (bb, 0, voff + k)),
            pl.BlockSpec(memory_space=pl.ANY),
        ],
        out_specs=pl.BlockSpec((None, tq, gw), lambda bb, k, i: (bb, i, ooff + k)),
        out_shape=jax.ShapeDtypeStruct(o_prev.shape, o_prev.dtype),
        input_output_aliases={4: 0},
        scratch_shapes=[pltpu.VMEM((t // tk, VT_ROWS, tk), BF16),
                        pltpu.VMEM((2, GROUP, tk, tq), F32), pltpu.VMEM((2, GROUP, tk, tq), BF16),
                        pltpu.VMEM((2, GROUP, 1, tq), F32), pltpu.VMEM((2, GROUP, 1, tq), F32),
                        pltpu.VMEM((GROUP, 1, tq), F32),
                        pltpu.VMEM((GROUP, VT_ROWS, tq), F32)],
        compiler_params=_params(("arbitrary", "arbitrary", "arbitrary")),
        name="mixc",
    )(p, p, p, p, o_prev)


def _ret_state_kernel(gc_ref, k_ref, v_ref, wk_ref, o_ref, s_ref, *, nh):
    d = pl.program_id(1)
    step = pl.program_id(2)

    @pl.when(step == 0)
    def _():
        s_ref[...] = jnp.zeros(s_ref.shape, F32)

    wk = wk_ref[...]
    for it in range(2):
        ci = d if it == 0 else 1 - d
        r0 = pl.multiple_of(ci * RET_CHUNK, RET_CHUNK)
        k = (k_ref[pl.ds(r0, RET_CHUNK), :].astype(F32) * wk).astype(BF16)
        v = v_ref[pl.ds(r0, RET_CHUNK), :]
        o_ref[ci] = s_ref[...].astype(o_ref.dtype)
        for h in range(nh):
            hs = slice(h * HEAD_DIM, (h + 1) * HEAD_DIM)
            kv = lax.dot_general(k[:, hs], v[:, hs], TN_DIMS, preferred_element_type=F32)
            s_ref[hs, :] = gc_ref[d, h] * s_ref[hs, :] + kv


def _ret_states(gc, p, wk, *, lay):
    b, t, _ = p.shape
    nh = lay["bh"]
    bw = nh * HEAD_DIM
    tb = 2 * RET_CHUNK
    nblk = t // tb
    koff, voff = lay["kb"] // nh, lay["vb"] // nh
    assert t % tb == 0

    def blk(d, s):
        return jnp.where(d == 0, s, jnp.where(s == 0, 0, nblk - s))

    return pl.pallas_call(
        functools.partial(_ret_state_kernel, nh=nh),
        grid=(b, 2, nblk),
        in_specs=[
            pl.BlockSpec(memory_space=pltpu.SMEM),
            pl.BlockSpec((None, tb, bw), lambda bb, d, s: (bb, blk(d, s), koff)),
            pl.BlockSpec((None, tb, bw), lambda bb, d, s: (bb, blk(d, s), voff)),
            pl.BlockSpec((None, RET_CHUNK, bw), lambda bb, d, s: (d, 0, 0)),
        ],
        out_specs=pl.BlockSpec((None, None, 2, bw, HEAD_DIM),
                               lambda bb, d, s: (bb, d, blk(d, s), 0, 0)),
        out_shape=jax.ShapeDtypeStruct((b, 2, t // RET_CHUNK, bw, HEAD_DIM), BF16),
        scratch_shapes=[pltpu.VMEM((bw, HEAD_DIM), F32)],
        compiler_params=_params(("arbitrary", "arbitrary", "arbitrary")),
        name="ret_states",
    )(gc, p, p, wk)


def _ret_out_kernel(q_ref, k_ref, v_ref, z_ref, sp_ref, dm_ref, wq_ref, oin_ref, o_ref, *, nh):
    del oin_ref
    for c in range(2):
        rs = slice(c * RET_CHUNK, (c + 1) * RET_CHUNK)
        for h in range(nh):
            hs = slice(h * HEAD_DIM, (h + 1) * HEAD_DIM)
            q = q_ref[rs, hs]
            a = lax.dot_general(q, k_ref[rs, hs], NT_DIMS, preferred_element_type=F32) * dm_ref[h]
            q32 = q.astype(F32)
            lhs = jnp.concatenate([a.astype(BF16),
                                   (q32 * wq_ref[0, :, hs]).astype(BF16),
                                   (q32 * wq_ref[1, :, hs]).astype(BF16)], axis=1)
            rhs = jnp.concatenate([v_ref[rs, hs], sp_ref[0, c, hs, :], sp_ref[1, c, hs, :]], axis=0)
            o = jnp.dot(lhs, rhs, preferred_element_type=F32)
            mu = jnp.mean(o, axis=-1, keepdims=True)
            oc = o - mu
            var = jnp.mean(oc * oc, axis=-1, keepdims=True)
            y = oc * lax.rsqrt(var + EPS)
            o_ref[rs, hs] = (y * _silu(z_ref[rs, hs].astype(F32))).astype(o_ref.dtype)


def _ret_out(p, sprev, dmat, wq, o_prev, *, lay):
    b, t, _ = p.shape
    nh = lay["bh"]
    bw = nh * HEAD_DIM
    tb = 2 * RET_CHUNK
    qoff, koff, voff, zoff = lay["qb"] // nh, lay["kb"] // nh, lay["vb"] // nh, lay["zb"] // nh
    ooff = lay["ob"] // nh
    return pl.pallas_call(
        functools.partial(_ret_out_kernel, nh=nh),
        grid=(b, t // tb),
        in_specs=[
            pl.BlockSpec((None, tb, bw), lambda bb, i: (bb, i, qoff)),
            pl.BlockSpec((None, tb, bw), lambda bb, i: (bb, i, koff)),
            pl.BlockSpec((None, tb, bw), lambda bb, i: (bb, i, voff)),
            pl.BlockSpec((None, tb, bw), lambda bb, i: (bb, i, zoff)),
            pl.BlockSpec((None, 2, 2, bw, HEAD_DIM), lambda bb, i: (bb, 0, i, 0, 0)),
            pl.BlockSpec((nh, RET_CHUNK, RET_CHUNK), lambda bb, i: (0, 0, 0)),
            pl.BlockSpec((2, RET_CHUNK, bw), lambda bb, i: (0, 0, 0)),
            pl.BlockSpec(memory_space=pl.ANY),
        ],
        out_specs=pl.BlockSpec((None, tb, bw), lambda bb, i: (bb, i, ooff)),
        out_shape=jax.ShapeDtypeStruct(o_prev.shape, o_prev.dtype),
        input_output_aliases={7: 0},
        compiler_params=_params(("arbitrary", "arbitrary")),
        name="ret_out",
    )(p, p, p, p, sprev, dmat, wq, o_prev)


def _outproj_kernel(o_ref, w_ref, x_ref, gl_ref, gc_ref, gp_ref, y_ref, *, tm, rc, first_lat, segs):
    i = pl.program_id(1)
    o = jnp.concatenate([o_ref[:, a:b] for a, b in segs], axis=1)
    y_ref[...] = jnp.dot(o, w_ref[...], preferred_element_type=F32)
    gain = jnp.where(i < first_lat, gc_ref[...], gl_ref[...]) * gp_ref[...]

    def body(c, carry):
        r0 = pl.multiple_of(c * rc, rc)
        y = y_ref[pl.ds(r0, rc), :]
        ms = jnp.mean(y * y, axis=-1, keepdims=True)
        y_ref[pl.ds(r0, rc), :] = x_ref[pl.ds(r0, rc), :] + (y * lax.rsqrt(ms + EPS)) * gain
        return carry

    lax.fori_loop(0, tm // rc, body, 0)


def _outproj(o, w, layer, xs, gate_lat, gate_ctx, g_post, *, ctx_len, segs, latent_only):
    b, t, d = xs.shape
    kdim = o.shape[-1]
    tm, rc = 256, 64
    assert t % tm == 0 and ctx_len % tm == 0 and w.shape[1] == kdim
    nctx = ctx_len // tm
    off = nctx if latent_only else 0
    first_lat = 0 if latent_only else nctx
    t_out = t - off * tm
    return pl.pallas_call(
        functools.partial(_outproj_kernel, tm=tm, rc=rc, first_lat=first_lat, segs=segs),
        grid=(b, t_out // tm),
        in_specs=[
            pl.BlockSpec((None, tm, kdim), lambda bb, i: (bb, i + off, 0)),
            pl.BlockSpec((None, kdim, d), lambda bb, i: (layer, 0, 0), pipeline_mode=pl.Buffered(1)),
            pl.BlockSpec((None, tm, d), lambda bb, i: (bb, i + off, 0)),
            pl.BlockSpec((None, 1, d), lambda bb, i: (bb, 0, 0)),
            pl.BlockSpec((1, d), lambda bb, i: (0, 0)),
            pl.BlockSpec((1, d), lambda bb, i: (0, 0)),
        ],
        out_specs=pl.BlockSpec((None, tm, d), lambda bb, i: (bb, i, 0)),
        out_shape=jax.ShapeDtypeStruct((b, t_out, d), xs.dtype),
        compiler_params=pltpu.CompilerParams(dimension_semantics=("arbitrary", "arbitrary"),
                                             vmem_limit_bytes=OUTPROJ_VMEM_LIMIT),
        name="outproj",
    )(o, w, xs, gate_lat, gate_ctx, g_post)


def _layout(d_model):
    nh = d_model // HEAD_DIM
    aq = 3 * nh // 8
    akv = aq // GROUP
    bh = nh // 4
    cq = nh - aq - bh
    ckv = cq // GROUP
    order = [("qa", aq), ("qc", cq), ("ka", akv), ("kc", ckv), ("qb", bh), ("kb", bh),
             ("za", aq), ("zc", cq), ("zb", bh), ("va", akv), ("vc", ckv), ("vb", bh)]
    lay = {"nh": nh, "aq": aq, "akv": akv, "bh": bh, "cq": cq, "ckv": ckv}
    off = 0
    for name, width in order:
        lay[name] = off
        off += width
    lay["nprep"] = lay["za"]
    lay["oa"], lay["oc"], lay["ob"] = 0, aq, aq + cq
    ref_order = [("qa", aq), ("ka", akv), ("va", akv), ("qb", bh), ("kb", bh), ("vb", bh),
                 ("qc", cq), ("kc", ckv), ("vc", ckv), ("za", aq), ("zb", bh), ("zc", cq)]
    ref_off, off = {}, 0
    for name, width in ref_order:
        ref_off[name] = off
        off += width
    lay["o_segs"] = ((0, aq * HEAD_DIM), ((aq + cq) * HEAD_DIM, nh * HEAD_DIM),
                     (aq * HEAD_DIM, (aq + cq) * HEAD_DIM))
    tu = math.gcd(4, *[width for _, width in order])
    lay["tiles"] = {"tn": tu * HEAD_DIM,
                    "src": tuple((lay[name] // tu, ref_off[name] // tu) for name, _ in order)}
    scale = HEAD_DIM ** -0.5
    lay["kinds"] = tuple(
        [(None, scale * LOG2E)] * aq + [("q", scale * LOG2E)] * cq + [(None, 1.0)] * akv
        + [("k", 1.0)] * ckv + [(None, 1.0)] * bh + [(None, scale)] * bh)
    return lay


def _rope_tables(n_lat, ctx_len):
    rows = n_lat // GRID_W
    row = jnp.repeat(jnp.arange(rows, dtype=F32), GRID_W)
    col = jnp.tile(jnp.arange(GRID_W, dtype=F32), rows)
    axis_dim = HEAD_DIM // 2
    inv = ROPE_BASE ** (-jnp.arange(0, axis_dim, 2, dtype=F32) / axis_dim)
    ang_r = row[:, None] * inv
    ang_c = col[:, None] * inv
    ang = jnp.concatenate([ang_r, ang_r, ang_c, ang_c], axis=-1)
    sign = jnp.where((jnp.arange(HEAD_DIM) // 32) % 2 == 1, 1.0, -1.0).astype(F32)
    cos = jnp.concatenate([jnp.ones((ctx_len, HEAD_DIM), F32), jnp.cos(ang)], axis=0)
    sin_s = jnp.concatenate([jnp.zeros((ctx_len, HEAD_DIM), F32), jnp.sin(ang) * sign], axis=0)
    return cos, sin_s


def _decay_tables(ret_decay_l):
    log_g = jax.nn.log_sigmoid(ret_decay_l.astype(F32))
    nh = log_g.shape[1]
    pos = jnp.arange(RET_CHUNK, dtype=F32)
    lane = lambda w: jnp.repeat(w, HEAD_DIM, axis=-1)
    wk_f = jnp.exp((RET_CHUNK - 1 - pos)[:, None] * log_g[0][None])
    wk_b = jnp.exp(pos[:, None] * log_g[1][None])
    wq_f = jnp.exp((pos + 1.0)[:, None] * log_g[0][None])
    wq_b = jnp.exp((RET_CHUNK - pos)[:, None] * log_g[1][None])
    wk = jnp.stack([lane(wk_f), lane(wk_b)])
    wq = jnp.stack([lane(wq_f), lane(wq_b)])
    gc = jnp.exp(RET_CHUNK * log_g)
    diff = pos[:, None] - pos[None, :]
    ef = jnp.where((diff >= 0)[None], diff[None] * log_g[0][:, None, None], -jnp.inf)
    eb = jnp.where((diff <= 0)[None], -diff[None] * log_g[1][:, None, None], -jnp.inf)
    dmat = jnp.exp(ef) + jnp.exp(eb)
    del nh
    return wk, wq, gc, dmat


def kernel(x, c, ctx, c_ctx, w_ada, b_ada, g_pre, g_post, w_in, w_out, sink_a, qnorm_c, knorm_c, ret_decay):
    bsz, n_lat, d = x.shape
    ctx_len = ctx.shape[1]
    depth = w_ada.shape[0]
    lay = _layout(d)

    cond_rows = jnp.zeros((8, d), F32).at[0:bsz].set(c).at[bsz].set(c_ctx)
    mods = _ada(cond_rows, w_ada, b_ada)
    cos, sin_s = _rope_tables(n_lat, ctx_len)
    w_in_p = w_in.astype(BF16)
    w_out_p = w_out.astype(BF16)

    xs = jnp.concatenate([ctx, x], axis=1)
    for l in range(depth):
        mod_lat = mods[l, 0:bsz].reshape(bsz, 1, 3 * d)
        mod_ctx = mods[l, bsz:bsz + 1]
        p = _inproj(xs, g_pre[l].reshape(1, d), mod_lat, mod_ctx, w_in_p, l, lay["tiles"], ctx_len=ctx_len)
        p = _prep(p, cos, sin_s, qnorm_c[l], knorm_c[l], kinds=lay["kinds"])
        o = _mixa(sink_a[l] * LOG2E, p, lay=lay, ctx_len=ctx_len)
        o = _mixc(p, o, lay=lay, ctx_len=ctx_len)
        wk, wq, gc, dmat = _decay_tables(ret_decay[l])
        sprev = _ret_states(gc, p, wk, lay=lay)
        o = _ret_out(p, sprev, dmat, wq, o, lay=lay)
        xs = _outproj(o, w_out_p, l, xs, mod_lat[:, :, 2 * d:], mod_ctx[:, 2 * d:],
                      g_post[l].reshape(1, d), ctx_len=ctx_len, segs=lay["o_segs"],
                      latent_only=(l == depth - 1))
    return xs
```

```python
import functools
import math

import jax
import jax.numpy as jnp
from jax import lax
from jax.experimental import pallas as pl
from jax.experimental.pallas import tpu as pltpu

HEAD_DIM = 128
GRID_W = 64
WINDOW = 128
RET_CHUNK = 128
ROPE_BASE = 10000.0
EPS = 1e-6
LOG2E = math.log2(math.e)
GROUP = 3
VMEM_LIMIT = 56 * 1024 * 1024
OUTPROJ_VMEM_LIMIT = 60 * 1024 * 1024

F32 = jnp.float32
BF16 = jnp.bfloat16
NT_DIMS = (((1,), (1,)), ((), ()))
TN_DIMS = (((0,), (0,)), ((), ()))


def _params(sem):
    return pltpu.CompilerParams(dimension_semantics=sem, vmem_limit_bytes=VMEM_LIMIT)


def _silu(z):
    return z / (1.0 + jnp.exp(-z))


def _ada_kernel(c_ref, w_ref, b_ref, o_ref):
    cond = _silu(c_ref[...])
    o_ref[...] = jnp.dot(cond.astype(BF16), w_ref[...].astype(BF16),
                         preferred_element_type=F32) + b_ref[...]


def _ada(cond_rows, w_ada, b_ada):
    depth, d, n3 = w_ada.shape
    tn = 512
    return pl.pallas_call(
        _ada_kernel,
        grid=(depth, n3 // tn),
        in_specs=[
            pl.BlockSpec((8, d), lambda l, j: (0, 0)),
            pl.BlockSpec((None, d, tn), lambda l, j: (l, 0, j)),
            pl.BlockSpec((None, 1, tn), lambda l, j: (l, 0, j)),
        ],
        out_specs=pl.BlockSpec((None, 8, tn), lambda l, j: (l, 0, j)),
        out_shape=jax.ShapeDtypeStruct((depth, 8, n3), F32),
        compiler_params=_params(("arbitrary", "arbitrary")),
        name="ada",
    )(cond_rows, w_ada, b_ada.reshape(depth, 1, n3))


def _inproj_kernel(x_ref, g_ref, ml_ref, mc_ref, *refs, tm, tn, rc, ctx_len, d):
    w_refs, o_ref, h_ref = refs[:-2], refs[-2], refs[-1]
    i = pl.program_id(1)
    j = pl.program_id(2)

    @pl.when(j == 0)
    def _():
        g = g_ref[...]
        sh_l = ml_ref[:, 0:d]
        sc_l = 1.0 + ml_ref[:, d:2 * d]
        sh_c = mc_ref[:, 0:d]
        sc_c = 1.0 + mc_ref[:, d:2 * d]

        def body(c, carry):
            r0 = pl.multiple_of(c * rc, rc)
            is_ctx = jnp.logical_and(i == 0, r0 < ctx_len)
            sh = jnp.where(is_ctx, sh_c, sh_l)
            sc = jnp.where(is_ctx, sc_c, sc_l)
            x = x_ref[pl.ds(r0, rc), :]
            ms = jnp.mean(x * x, axis=-1, keepdims=True)
            y = x * lax.rsqrt(ms + EPS) * g
            h_ref[pl.ds(r0, rc), :] = (y * sc + sh).astype(BF16)
            return carry

        lax.fori_loop(0, tm // rc, body, 0, unroll=2)

    for u, w_ref in enumerate(w_refs):
        o_ref[:, u * tn:(u + 1) * tn] = jnp.dot(
            h_ref[...], w_ref[...], preferred_element_type=F32).astype(o_ref.dtype)


def _inproj(xs, g_pre, mod_lat, mod_ctx, w, layer, tiles, *, ctx_len):
    b, t, d = xs.shape
    n = w.shape[-1]
    tm, rc = 768, 32
    tn = tiles["tn"]
    src = tiles["src"]
    nw = 2
    assert t % tm == 0 and n % (nw * tn) == 0 and ctx_len % rc == 0 and ctx_len <= tm

    def src_tile(j):
        out = j - src[0][0] + src[0][1]
        for start, ref_start in src[1:]:
            out = jnp.where(j >= start, j - start + ref_start, out)
        return out

    def w_spec(u):
        return pl.BlockSpec((None, d, tn), lambda bb, i, j: (layer, 0, src_tile(nw * j + u)))

    return pl.pallas_call(
        functools.partial(_inproj_kernel, tm=tm, tn=tn, rc=rc, ctx_len=ctx_len, d=d),
        grid=(b, t // tm, n // (nw * tn)),
        in_specs=[
            pl.BlockSpec((None, tm, d), lambda bb, i, j: (bb, i, 0)),
            pl.BlockSpec((1, d), lambda bb, i, j: (0, 0)),
            pl.BlockSpec((None, 1, 3 * d), lambda bb, i, j: (bb, 0, 0)),
            pl.BlockSpec((1, 3 * d), lambda bb, i, j: (0, 0)),
        ] + [w_spec(u) for u in range(nw)],
        out_specs=pl.BlockSpec((None, tm, nw * tn), lambda bb, i, j: (bb, i, j)),
        out_shape=jax.ShapeDtypeStruct((b, t, n), BF16),
        scratch_shapes=[pltpu.VMEM((tm, d), BF16)],
        compiler_params=_params(("arbitrary", "arbitrary", "arbitrary")),
        name="inproj",
    )(xs, g_pre, mod_lat, mod_ctx, *([w] * nw))


def _prep_kernel(p_ref, cos_ref, sin_ref, qn_ref, kn_ref, swap_ref, o_ref, *, kinds):
    cos = cos_ref[...]
    sin = sin_ref[...]
    first = {None: cos, "q": cos * qn_ref[0:1, :], "k": cos * kn_ref[0:1, :]}
    second = {None: sin, "q": sin * qn_ref[1:2, :], "k": sin * kn_ref[1:2, :]}
    swap = swap_ref[...]
    for t2 in range(len(kinds) // 2):
        pair = p_ref[:, 2 * t2 * HEAD_DIM:(2 * t2 + 2) * HEAD_DIM]
        swapped = jnp.dot(pair, swap, preferred_element_type=F32)
        for h in range(2):
            norm, scale = kinds[2 * t2 + h]
            hs = slice(h * HEAD_DIM, (h + 1) * HEAD_DIM)
            x = pair[:, hs].astype(F32)
            y = x * first[norm] + swapped[:, hs] * second[norm]
            if norm is not None:
                y = y * lax.rsqrt(jnp.mean(x * x, axis=-1, keepdims=True) + EPS)
            if scale != 1.0:
                y = y * scale
            t = 2 * t2 + h
            o_ref[:, t * HEAD_DIM:(t + 1) * HEAD_DIM] = y.astype(o_ref.dtype)


def _prep(p, cos, sin_s, qn, kn, *, kinds):
    b, t, n = p.shape
    tp = 256
    npre = len(kinds) * HEAD_DIM
    assert t % tp == 0 and len(kinds) % 2 == 0
    lanes = jnp.arange(2 * HEAD_DIM)
    swap = (lanes[:, None] == (lanes[None, :] ^ 32)).astype(BF16)
    with_swapped = lambda g: jnp.stack([g, g[jnp.arange(HEAD_DIM) ^ 32]])
    return pl.pallas_call(
        functools.partial(_prep_kernel, kinds=kinds),
        grid=(b, t // tp),
        in_specs=[
            pl.BlockSpec((None, tp, npre), lambda bb, i: (bb, i, 0)),
            pl.BlockSpec((tp, HEAD_DIM), lambda bb, i: (i, 0)),
            pl.BlockSpec((tp, HEAD_DIM), lambda bb, i: (i, 0)),
            pl.BlockSpec((2, HEAD_DIM), lambda bb, i: (0, 0)),
            pl.BlockSpec((2, HEAD_DIM), lambda bb, i: (0, 0)),
            pl.BlockSpec((2 * HEAD_DIM, 2 * HEAD_DIM), lambda bb, i: (0, 0)),
        ],
        out_specs=pl.BlockSpec((None, tp, npre), lambda bb, i: (bb, i, 0)),
        out_shape=jax.ShapeDtypeStruct(p.shape, p.dtype),
        input_output_aliases={0: 0},
        compiler_params=_params(("arbitrary", "arbitrary")),
        name="prep",
    )(p, cos, sin_s, with_swapped(qn), with_swapped(kn), swap)


VT_ROWS = HEAD_DIM + 16
SOFTMAX_ROWS = 128
STEPS_PER_TRIP = 2
ALL_HEADS = tuple(range(GROUP))


def _ones_row_block(width):
    first = lax.broadcasted_iota(jnp.int32, (VT_ROWS - HEAD_DIM, width), 0) == 0
    return jnp.where(first, 1.0, 0.0).astype(BF16)


def _mixa_kernel(sink_ref, q_ref, z_ref, k_ref, v_ref, o_ref, vt_ref, s_ref, p_ref,
                 *, tq, nsub, ctx_len, s_lat):
    kv = pl.program_id(1)
    blk = pl.program_id(2)
    wl = tq + 2 * WINDOW
    nkeys = ctx_len + wl

    @pl.when(blk == 0)
    def _():
        def tbody(c, carry):
            r0 = pl.multiple_of(c * HEAD_DIM, HEAD_DIM)
            vt_ref[c, 0:HEAD_DIM, :] = v_ref[pl.ds(r0, HEAD_DIM), :].astype(F32).T.astype(BF16)
            vt_ref[c, HEAD_DIM:VT_ROWS, :] = _ones_row_block(HEAD_DIM)
            return carry

        lax.fori_loop(0, (ctx_len + s_lat) // HEAD_DIM, tbody, 0)

    def geometry(u):
        q0 = (blk * nsub + u) * tq - ctx_len
        cs = pl.multiple_of(jnp.clip(q0 - WINDOW, 0, s_lat - wl), HEAD_DIM)
        return q0, cs

    def scores(u):
        q0, cs = geometry(u)
        keys = jnp.concatenate([k_ref[0:ctx_len, :], k_ref[pl.ds(ctx_len + cs, wl), :]], axis=0)
        ss = [lax.dot_general(keys, q_ref[u * tq:(u + 1) * tq, g * HEAD_DIM:(g + 1) * HEAD_DIM], NT_DIMS,
                              preferred_element_type=F32) for g in range(GROUP)]
        ms = []
        for g in range(GROUP):
            s_ref[u, g, 0:ctx_len, :] = ss[g][0:ctx_len]
            ms.append(jnp.maximum(jnp.max(ss[g][0:ctx_len], axis=0, keepdims=True), sink_ref[kv * GROUP + g]))
        row = lax.broadcasted_iota(jnp.int32, (SOFTMAX_ROWS, tq), 0)
        col = lax.broadcasted_iota(jnp.int32, (SOFTMAX_ROWS, tq), 1)
        diff = row - col
        for r in range(0, wl, SOFTMAX_ROWS):
            rel = diff + (cs + r - q0)
            valid = jnp.logical_and(jnp.abs(rel) <= WINDOW, q0 >= 0)
            for g in range(GROUP):
                piece = jnp.where(valid, ss[g][ctx_len + r:ctx_len + r + SOFTMAX_ROWS], -jnp.inf)
                s_ref[u, g, ctx_len + r:ctx_len + r + SOFTMAX_ROWS, :] = piece
                ms[g] = jnp.maximum(ms[g], jnp.max(piece, axis=0, keepdims=True))
        return ms

    def softmax(u, ms):
        for g in range(GROUP):
            for r in range(0, nkeys, SOFTMAX_ROWS):
                p_ref[u, g, r:r + SOFTMAX_ROWS, :] = jnp.exp2(
                    s_ref[u, g, r:r + SOFTMAX_ROWS, :] - ms[g]).astype(BF16)

    def output(u, ms):
        _, cs = geometry(u)
        c0 = (ctx_len + cs) // HEAD_DIM
        tiles = [vt_ref[c] for c in range(ctx_len // HEAD_DIM)] + [vt_ref[c0 + c] for c in range(wl // HEAD_DIM)]
        vt = jnp.concatenate(tiles, axis=1)
        for g in range(GROUP):
            pv = jnp.dot(vt, p_ref[u, g], preferred_element_type=F32)
            den = pv[HEAD_DIM:HEAD_DIM + 1] + jnp.exp2(sink_ref[kv * GROUP + g] - ms[g])
            o = (pv[0:HEAD_DIM] / den).T
            z = z_ref[u * tq:(u + 1) * tq, g * HEAD_DIM:(g + 1) * HEAD_DIM].astype(F32)
            o_ref[u * tq:(u + 1) * tq, g * HEAD_DIM:(g + 1) * HEAD_DIM] = (o * _silu(z)).astype(o_ref.dtype)

    ms = {0: scores(0)}
    for u in range(nsub):
        if u + 1 < nsub:
            ms[u + 1] = scores(u + 1)
        softmax(u, ms[u])
        if u >= 1:
            output(u - 1, ms[u - 1])
    output(nsub - 1, ms[nsub - 1])


def _mixa(sink2, p, *, lay, ctx_len):
    b, t, _ = p.shape
    tq, nsub = 256, 3
    gw = GROUP * HEAD_DIM
    nkv = lay["akv"]
    s_lat = t - ctx_len
    wl = tq + 2 * WINDOW
    tb = tq * nsub
    assert t % tb == 0 and ctx_len % tq == 0 and s_lat >= wl and WINDOW == HEAD_DIM
    qoff, zoff, koff, voff = lay["qa"] // GROUP, lay["za"] // GROUP, lay["ka"], lay["va"]
    return pl.pallas_call(
        functools.partial(_mixa_kernel, tq=tq, nsub=nsub, ctx_len=ctx_len, s_lat=s_lat),
        grid=(b, nkv, t // tb),
        in_specs=[
            pl.BlockSpec(memory_space=pltpu.SMEM),
            pl.BlockSpec((None, tb, gw), lambda bb, k, i: (bb, i, qoff + k)),
            pl.BlockSpec((None, tb, gw), lambda bb, k, i: (bb, i, zoff + k)),
            pl.BlockSpec((None, t, HEAD_DIM), lambda bb, k, i: (bb, 0, koff + k)),
            pl.BlockSpec((None, t, HEAD_DIM), lambda bb, k, i: (bb, 0, voff + k)),
        ],
        out_specs=pl.BlockSpec((None, tb, gw), lambda bb, k, i: (bb, i, k)),
        out_shape=jax.ShapeDtypeStruct((b, t, lay["nh"] * HEAD_DIM), BF16),
        scratch_shapes=[pltpu.VMEM((t // HEAD_DIM, VT_ROWS, HEAD_DIM), BF16),
                        pltpu.VMEM((nsub, GROUP, ctx_len + wl, tq), F32),
                        pltpu.VMEM((nsub, GROUP, ctx_len + wl, tq), BF16)],
        compiler_params=_params(("arbitrary", "arbitrary", "arbitrary")),
        name="mixa",
    )(sink2, p, p, p, p)


def _mixc_kernel(q_ref, z_ref, k_ref, v_ref, oin_ref, o_ref,
                 vt_ref, qt_ref, s_ref, p_ref, a_ref, mc_ref, m_ref, acc_ref, *, tq, tk, ctx_len, nchunk):
    del oin_ref
    qi = pl.program_id(2)

    @pl.when(qi == 0)
    def _():
        def tbody(c, carry):
            r0 = pl.multiple_of(c * tk, tk)
            vt_ref[c, 0:HEAD_DIM, :] = v_ref[pl.ds(r0, tk), :].astype(F32).T.astype(BF16)
            vt_ref[c, HEAD_DIM:VT_ROWS, :] = _ones_row_block(tk)
            return carry

        lax.fori_loop(0, nchunk, tbody, 0)

    for g in range(GROUP):
        qt_ref[g] = q_ref[:, g * HEAD_DIM:(g + 1) * HEAD_DIM].astype(F32).T.astype(BF16)

    def scores(c, buf, gs=ALL_HEADS):
        r0 = pl.multiple_of(c * tk, tk)
        kc = k_ref[pl.ds(r0, tk), :]
        for g in gs:
            s = jnp.dot(kc, qt_ref[g], preferred_element_type=F32)
            s_ref[buf, g] = s
            mc_ref[buf, g] = jnp.max(s, axis=0, keepdims=True)

    def softmax(buf, first, gs=ALL_HEADS):
        for g in gs:
            m_cur = mc_ref[buf, g]
            if first:
                m_new = m_cur
            else:
                m_prev = m_ref[g]
                m_new = jnp.maximum(m_prev, m_cur)
                a_ref[buf, g] = jnp.exp2(m_prev - m_new)
            m_ref[g] = m_new
            for r in range(0, tk, SOFTMAX_ROWS):
                p_ref[buf, g, r:r + SOFTMAX_ROWS, :] = jnp.exp2(
                    s_ref[buf, g, r:r + SOFTMAX_ROWS, :] - m_new).astype(BF16)

    def output(c, buf, first, gs=ALL_HEADS):
        vtc = vt_ref[c]
        for g in gs:
            pv = jnp.dot(vtc, p_ref[buf, g], preferred_element_type=F32)
            acc_ref[g] = pv if first else a_ref[buf, g] * acc_ref[g] + pv

    nctx = ctx_len // tq

    @pl.when(qi < nctx)
    def _():
        kc = k_ref[0:ctx_len, :]
        vtc = vt_ref[0, :, 0:ctx_len]
        for g in range(GROUP):
            s = jnp.dot(kc, qt_ref[g], preferred_element_type=F32)
            p = jnp.exp2(s - jnp.max(s, axis=0, keepdims=True))
            acc_ref[g] = jnp.dot(vtc, p.astype(BF16), preferred_element_type=F32)

    @pl.when(qi >= nctx)
    def _():
        def step(i, par, first=False):
            softmax(par, False)
            scores(i + 1, 1 - par)
            output(i - 1, 1 - par, first)

        scores(0, 0)
        scores(1, 1)
        softmax(0, True)
        mid = nchunk - 2
        if mid >= 1:
            step(1, 1, first=True)
        rest = max(mid - 1, 0)
        trips = rest // STEPS_PER_TRIP

        def body(j, carry):
            i = 2 + STEPS_PER_TRIP * j
            for k in range(STEPS_PER_TRIP):
                step(i + k, k % 2)
            return carry

        lax.fori_loop(0, trips, body, 0)
        for i in range(2 + trips * STEPS_PER_TRIP, nchunk - 1):
            step(i, i % 2)
        last = nchunk - 1
        softmax(last % 2, False)
        output(last - 1, (last - 1) % 2, last == 1)
        output(last, last % 2, False)

    o = jnp.concatenate([(acc_ref[g, 0:HEAD_DIM, :] / acc_ref[g, HEAD_DIM:HEAD_DIM + 1, :]).T
                         for g in range(GROUP)], axis=1)
    o_ref[...] = (o * _silu(z_ref[...].astype(F32))).astype(o_ref.dtype)


def _mixc(p, o_prev, *, lay, ctx_len):
    b, t, _ = p.shape
    tq, tk = 256, 768
    gw = GROUP * HEAD_DIM
    nkv = lay["ckv"]
    assert t % tq == 0 and ctx_len % tq == 0 and t % tk == 0 and ctx_len <= tk
    qoff, zoff, koff, voff = lay["qc"] // GROUP, lay["zc"] // GROUP, lay["kc"], lay["vc"]
    ooff = lay["oc"] // GROUP
    return pl.pallas_call(
        functools.partial(_mixc_kernel, tq=tq, tk=tk, ctx_len=ctx_len, nchunk=t // tk),
        grid=(b, nkv, t // tq),
        in_specs=[
            pl.BlockSpec((None, tq, gw), lambda bb, k, i: (bb, i, qoff + k)),
            pl.BlockSpec((None, tq, gw), lambda bb, k, i: (bb, i, zoff + k)),
            pl.BlockSpec((None, t, HEAD_DIM), lambda bb, k, i: (bb, 0, koff + k)),
            pl.BlockSpec((None, t, HEAD_DIM), lambda bb, k, i: (bb, 0, voff + k)),
            pl.BlockSpec(memory_space=pl.ANY),
        ],
        out_specs=pl.BlockSpec((None, tq, gw), lambda bb, k, i: (bb, i, ooff + k)),
        out_shape=jax.ShapeDtypeStruct(o_prev.shape, o_prev.dtype),
        input_output_aliases={4: 0},
        scratch_shapes=[pltpu.VMEM((t // tk, VT_ROWS, tk), BF16), pltpu.VMEM((GROUP, HEAD_DIM, tq), BF16),
                        pltpu.VMEM((2, GROUP, tk, tq), F32), pltpu.VMEM((2, GROUP, tk, tq), BF16),
                        pltpu.VMEM((2, GROUP, 1, tq), F32), pltpu.VMEM((2, GROUP, 1, tq), F32),
                        pltpu.VMEM((GROUP, 1, tq), F32),
                        pltpu.VMEM((GROUP, VT_ROWS, tq), F32)],
        compiler_params=_params(("arbitrary", "arbitrary", "arbitrary")),
        name="mixc",
    )(p, p, p, p, o_prev)


def _ret_state_kernel(gc_ref, kf_ref, vf_ref, kb_ref, vb_ref, wk_ref, of_ref, ob_ref, sf_ref, sb_ref, *, nh):
    step = pl.program_id(1)

    @pl.when(step == 0)
    def _():
        sf_ref[...] = jnp.zeros(sf_ref.shape, F32)
        sb_ref[...] = jnp.zeros(sb_ref.shape, F32)

    heads = [slice(h * HEAD_DIM, (h + 1) * HEAD_DIM) for h in range(nh)]
    for d, (k_ref, v_ref, o_ref, s_ref, order) in enumerate(
            ((kf_ref, vf_ref, of_ref, sf_ref, (0, 1)), (kb_ref, vb_ref, ob_ref, sb_ref, (1, 0)))):
        wk = wk_ref[d]
        for ci in order:
            rs = slice(ci * RET_CHUNK, (ci + 1) * RET_CHUNK)
            k = (k_ref[rs, :].astype(F32) * wk).astype(BF16)
            v = v_ref[rs, :]
            o_ref[ci] = s_ref[...].astype(o_ref.dtype)
            kvs = [lax.dot_general(k[:, hs], v[:, hs], TN_DIMS, preferred_element_type=F32) for hs in heads]
            for h, hs in enumerate(heads):
                s_ref[hs, :] = gc_ref[d, h] * s_ref[hs, :] + kvs[h]


def _ret_states(gc, p, wk, *, lay):
    b, t, _ = p.shape
    nh = lay["bh"]
    bw = nh * HEAD_DIM
    tb = 2 * RET_CHUNK
    nblk = t // tb
    koff, voff = lay["kb"] // nh, lay["vb"] // nh
    assert t % tb == 0

    def bwd_blk(s):
        return jnp.where(s == 0, 0, nblk - s)

    state = jax.ShapeDtypeStruct((b, t // RET_CHUNK, bw, HEAD_DIM), BF16)
    return pl.pallas_call(
        functools.partial(_ret_state_kernel, nh=nh),
        grid=(b, nblk),
        in_specs=[
            pl.BlockSpec(memory_space=pltpu.SMEM),
            pl.BlockSpec((None, tb, bw), lambda bb, s: (bb, s, koff)),
            pl.BlockSpec((None, tb, bw), lambda bb, s: (bb, s, voff)),
            pl.BlockSpec((None, tb, bw), lambda bb, s: (bb, bwd_blk(s), koff)),
            pl.BlockSpec((None, tb, bw), lambda bb, s: (bb, bwd_blk(s), voff)),
            pl.BlockSpec((2, RET_CHUNK, bw), lambda bb, s: (0, 0, 0)),
        ],
        out_specs=[pl.BlockSpec((None, 2, bw, HEAD_DIM), lambda bb, s: (bb, s, 0, 0)),
                   pl.BlockSpec((None, 2, bw, HEAD_DIM), lambda bb, s: (bb, bwd_blk(s), 0, 0))],
        out_shape=[state, state],
        scratch_shapes=[pltpu.VMEM((bw, HEAD_DIM), F32), pltpu.VMEM((bw, HEAD_DIM), F32)],
        compiler_params=_params(("arbitrary", "arbitrary")),
        name="ret_states",
    )(gc, p, p, p, p, wk)


def _ret_out_kernel(q_ref, k_ref, v_ref, z_ref, spf_ref, spb_ref, dm_ref, wq_ref, oin_ref, o_ref, *, nh, nc):
    del oin_ref
    heads = [slice(h * HEAD_DIM, (h + 1) * HEAD_DIM) for h in range(nh)]
    for c in range(nc):
        rs = slice(c * RET_CHUNK, (c + 1) * RET_CHUNK)
        qs = [q_ref[rs, hs] for hs in heads]
        intra = [lax.dot_general(qs[h], k_ref[rs, heads[h]], NT_DIMS, preferred_element_type=F32) * dm_ref[h]
                 for h in range(nh)]
        outs = []
        for h, hs in enumerate(heads):
            q32 = qs[h].astype(F32)
            lhs = jnp.concatenate([intra[h].astype(BF16),
                                   (q32 * wq_ref[0, :, hs]).astype(BF16),
                                   (q32 * wq_ref[1, :, hs]).astype(BF16)], axis=1)
            rhs = jnp.concatenate([v_ref[rs, hs], spf_ref[c, hs, :], spb_ref[c, hs, :]], axis=0)
            outs.append(jnp.dot(lhs, rhs, preferred_element_type=F32))
        for h, hs in enumerate(heads):
            o = outs[h]
            mu = jnp.mean(o, axis=-1, keepdims=True)
            oc = o - mu
            var = jnp.mean(oc * oc, axis=-1, keepdims=True)
            y = oc * lax.rsqrt(var + EPS)
            o_ref[rs, hs] = (y * _silu(z_ref[rs, hs].astype(F32))).astype(o_ref.dtype)


def _ret_out(p, sprev_f, sprev_b, dmat, wq, o_prev, *, lay):
    b, t, _ = p.shape
    nh = lay["bh"]
    bw = nh * HEAD_DIM
    nc = 6 if t % (6 * RET_CHUNK) == 0 else 2
    tb = nc * RET_CHUNK
    assert t % tb == 0
    qoff, koff, voff, zoff = lay["qb"] // nh, lay["kb"] // nh, lay["vb"] // nh, lay["zb"] // nh
    ooff = lay["ob"] // nh
    return pl.pallas_call(
        functools.partial(_ret_out_kernel, nh=nh, nc=nc),
        grid=(b, t // tb),
        in_specs=[
            pl.BlockSpec((None, tb, bw), lambda bb, i: (bb, i, qoff)),
            pl.BlockSpec((None, tb, bw), lambda bb, i: (bb, i, koff)),
            pl.BlockSpec((None, tb, bw), lambda bb, i: (bb, i, voff)),
            pl.BlockSpec((None, tb, bw), lambda bb, i: (bb, i, zoff)),
            pl.BlockSpec((None, nc, bw, HEAD_DIM), lambda bb, i: (bb, i, 0, 0)),
            pl.BlockSpec((None, nc, bw, HEAD_DIM), lambda bb, i: (bb, i, 0, 0)),
            pl.BlockSpec((nh, RET_CHUNK, RET_CHUNK), lambda bb, i: (0, 0, 0)),
            pl.BlockSpec((2, RET_CHUNK, bw), lambda bb, i: (0, 0, 0)),
            pl.BlockSpec(memory_space=pl.ANY),
        ],
        out_specs=pl.BlockSpec((None, tb, bw), lambda bb, i: (bb, i, ooff)),
        out_shape=jax.ShapeDtypeStruct(o_prev.shape, o_prev.dtype),
        input_output_aliases={8: 0},
        compiler_params=_params(("arbitrary", "arbitrary")),
        name="ret_out",
    )(p, p, p, p, sprev_f, sprev_b, dmat, wq, o_prev)


def _outproj_kernel(o_ref, w_ref, x_ref, gl_ref, gc_ref, gp_ref, y_ref, *, tm, rc, first_lat, segs):
    i = pl.program_id(1)
    o = jnp.concatenate([o_ref[:, a:b] for a, b in segs], axis=1)
    y_ref[...] = jnp.dot(o, w_ref[...], preferred_element_type=F32)
    gain = jnp.where(i < first_lat, gc_ref[...], gl_ref[...]) * gp_ref[...]

    def body(c, carry):
        r0 = pl.multiple_of(c * rc, rc)
        y = y_ref[pl.ds(r0, rc), :]
        ms = jnp.mean(y * y, axis=-1, keepdims=True)
        y_ref[pl.ds(r0, rc), :] = x_ref[pl.ds(r0, rc), :] + (y * lax.rsqrt(ms + EPS)) * gain
        return carry

    lax.fori_loop(0, tm // rc, body, 0)


def _outproj(o, w, layer, xs, gate_lat, gate_ctx, g_post, *, ctx_len, segs, latent_only):
    b, t, d = xs.shape
    kdim = o.shape[-1]
    tm, rc = 256, 64
    assert t % tm == 0 and ctx_len % tm == 0 and w.shape[1] == kdim
    nctx = ctx_len // tm
    off = nctx if latent_only else 0
    first_lat = 0 if latent_only else nctx
    t_out = t - off * tm
    return pl.pallas_call(
        functools.partial(_outproj_kernel, tm=tm, rc=rc, first_lat=first_lat, segs=segs),
        grid=(b, t_out // tm),
        in_specs=[
            pl.BlockSpec((None, tm, kdim), lambda bb, i: (bb, i + off, 0)),
            pl.BlockSpec((None, kdim, d), lambda bb, i: (layer, 0, 0), pipeline_mode=pl.Buffered(1)),
            pl.BlockSpec((None, tm, d), lambda bb, i: (bb, i + off, 0)),
            pl.BlockSpec((None, 1, d), lambda bb, i: (bb, 0, 0)),
            pl.BlockSpec((1, d), lambda bb, i: (0, 0)),
            pl.BlockSpec((1, d), lambda bb, i: (0, 0)),
        ],
        out_specs=pl.BlockSpec((None, tm, d), lambda bb, i: (bb, i, 0)),
        out_shape=jax.ShapeDtypeStruct((b, t_out, d), xs.dtype),
        compiler_params=pltpu.CompilerParams(dimension_semantics=("arbitrary", "arbitrary"),
                                             vmem_limit_bytes=OUTPROJ_VMEM_LIMIT),
        name="outproj",
    )(o, w, xs, gate_lat, gate_ctx, g_post)


def _layout(d_model):
    nh = d_model // HEAD_DIM
    aq = 3 * nh // 8
    akv = aq // GROUP
    bh = nh // 4
    cq = nh - aq - bh
    ckv = cq // GROUP
    order = [("qa", aq), ("qc", cq), ("ka", akv), ("kc", ckv), ("qb", bh), ("kb", bh),
             ("za", aq), ("zc", cq), ("zb", bh), ("va", akv), ("vc", ckv), ("vb", bh)]
    lay = {"nh": nh, "aq": aq, "akv": akv, "bh": bh, "cq": cq, "ckv": ckv}
    off = 0
    for name, width in order:
        lay[name] = off
        off += width
    lay["nprep"] = lay["za"]
    lay["oa"], lay["oc"], lay["ob"] = 0, aq, aq + cq
    ref_order = [("qa", aq), ("ka", akv), ("va", akv), ("qb", bh), ("kb", bh), ("vb", bh),
                 ("qc", cq), ("kc", ckv), ("vc", ckv), ("za", aq), ("zb", bh), ("zc", cq)]
    ref_off, off = {}, 0
    for name, width in ref_order:
        ref_off[name] = off
        off += width
    lay["o_segs"] = ((0, aq * HEAD_DIM), ((aq + cq) * HEAD_DIM, nh * HEAD_DIM),
                     (aq * HEAD_DIM, (aq + cq) * HEAD_DIM))
    tu = math.gcd(4, *[width for _, width in order])
    lay["tiles"] = {"tn": tu * HEAD_DIM,
                    "src": tuple((lay[name] // tu, ref_off[name] // tu) for name, _ in order)}
    scale = HEAD_DIM ** -0.5
    lay["kinds"] = tuple(
        [(None, scale * LOG2E)] * aq + [("q", scale * LOG2E)] * cq + [(None, 1.0)] * akv
        + [("k", 1.0)] * ckv + [(None, 1.0)] * bh + [(None, scale)] * bh)
    return lay


def _rope_tables(n_lat, ctx_len):
    rows = n_lat // GRID_W
    row = jnp.repeat(jnp.arange(rows, dtype=F32), GRID_W)
    col = jnp.tile(jnp.arange(GRID_W, dtype=F32), rows)
    axis_dim = HEAD_DIM // 2
    inv = ROPE_BASE ** (-jnp.arange(0, axis_dim, 2, dtype=F32) / axis_dim)
    ang_r = row[:, None] * inv
    ang_c = col[:, None] * inv
    ang = jnp.concatenate([ang_r, ang_r, ang_c, ang_c], axis=-1)
    sign = jnp.where((jnp.arange(HEAD_DIM) // 32) % 2 == 1, 1.0, -1.0).astype(F32)
    cos = jnp.concatenate([jnp.ones((ctx_len, HEAD_DIM), F32), jnp.cos(ang)], axis=0)
    sin_s = jnp.concatenate([jnp.zeros((ctx_len, HEAD_DIM), F32), jnp.sin(ang) * sign], axis=0)
    return cos, sin_s


def _decay_tables(ret_decay_l):
    log_g = jax.nn.log_sigmoid(ret_decay_l.astype(F32))
    nh = log_g.shape[1]
    pos = jnp.arange(RET_CHUNK, dtype=F32)
    lane = lambda w: jnp.repeat(w, HEAD_DIM, axis=-1)
    wk_f = jnp.exp((RET_CHUNK - 1 - pos)[:, None] * log_g[0][None])
    wk_b = jnp.exp(pos[:, None] * log_g[1][None])
    wq_f = jnp.exp((pos + 1.0)[:, None] * log_g[0][None])
    wq_b = jnp.exp((RET_CHUNK - pos)[:, None] * log_g[1][None])
    wk = jnp.stack([lane(wk_f), lane(wk_b)])
    wq = jnp.stack([lane(wq_f), lane(wq_b)])
    gc = jnp.exp(RET_CHUNK * log_g)
    diff = pos[:, None] - pos[None, :]
    ef = jnp.where((diff >= 0)[None], diff[None] * log_g[0][:, None, None], -jnp.inf)
    eb = jnp.where((diff <= 0)[None], -diff[None] * log_g[1][:, None, None], -jnp.inf)
    dmat = jnp.exp(ef) + jnp.exp(eb)
    del nh
    return wk, wq, gc, dmat


def kernel(x, c, ctx, c_ctx, w_ada, b_ada, g_pre, g_post, w_in, w_out, sink_a, qnorm_c, knorm_c, ret_decay):
    bsz, n_lat, d = x.shape
    ctx_len = ctx.shape[1]
    depth = w_ada.shape[0]
    lay = _layout(d)

    cond_rows = jnp.zeros((8, d), F32).at[0:bsz].set(c).at[bsz].set(c_ctx)
    mods = _ada(cond_rows, w_ada, b_ada)
    cos, sin_s = _rope_tables(n_lat, ctx_len)
    w_in_p = w_in.astype(BF16)
    w_out_p = w_out.astype(BF16)

    xs = jnp.concatenate([ctx, x], axis=1)
    for l in range(depth):
        mod_lat = mods[l, 0:bsz].reshape(bsz, 1, 3 * d)
        mod_ctx = mods[l, bsz:bsz + 1]
        p = _inproj(xs, g_pre[l].reshape(1, d), mod_lat, mod_ctx, w_in_p, l, lay["tiles"], ctx_len=ctx_len)
        p = _prep(p, cos, sin_s, qnorm_c[l], knorm_c[l], kinds=lay["kinds"])
        o = _mixa(sink_a[l] * LOG2E, p, lay=lay, ctx_len=ctx_len)
        o = _mixc(p, o, lay=lay, ctx_len=ctx_len)
        wk, wq, gc, dmat = _decay_tables(ret_decay[l])
        sprev_f, sprev_b = _ret_states(gc, p, wk, lay=lay)
        o = _ret_out(p, sprev_f, sprev_b, dmat, wq, o, lay=lay)
        xs = _outproj(o, w_out_p, l, xs, mod_lat[:, :, 2 * d:], mod_ctx[:, 2 * d:],
                      g_post[l].reshape(1, d), ctx_len=ctx_len, segs=lay["o_segs"],
                      latent_only=(l == depth - 1))
    return xs
```

```python
import functools
import math

import jax
import jax.numpy as jnp
from jax import lax
from jax.experimental import pallas as pl
from jax.experimental.pallas import tpu as pltpu

HEAD_DIM = 128
GRID_W = 64
WINDOW = 128
RET_CHUNK = 128
ROPE_BASE = 10000.0
EPS = 1e-6
LOG2E = math.log2(math.e)
GROUP = 3
VMEM_LIMIT = 56 * 1024 * 1024
OUTPROJ_VMEM_LIMIT = 60 * 1024 * 1024

F32 = jnp.float32
BF16 = jnp.bfloat16
NT_DIMS = (((1,), (1,)), ((), ()))
TN_DIMS = (((0,), (0,)), ((), ()))


def _params(sem):
    return pltpu.CompilerParams(dimension_semantics=sem, vmem_limit_bytes=VMEM_LIMIT)


def _silu(z):
    return z / (1.0 + jnp.exp(-z))


def _ada_kernel(c_ref, w_ref, b_ref, o_ref):
    cond = _silu(c_ref[...])
    o_ref[...] = jnp.dot(cond.astype(BF16), w_ref[...].astype(BF16),
                         preferred_element_type=F32) + b_ref[...]


def _ada(cond_rows, w_ada, b_ada):
    depth, d, n3 = w_ada.shape
    tn = 512
    return pl.pallas_call(
        _ada_kernel,
        grid=(depth, n3 // tn),
        in_specs=[
            pl.BlockSpec((8, d), lambda l, j: (0, 0)),
            pl.BlockSpec((None, d, tn), lambda l, j: (l, 0, j)),
            pl.BlockSpec((None, 1, tn), lambda l, j: (l, 0, j)),
        ],
        out_specs=pl.BlockSpec((None, 8, tn), lambda l, j: (l, 0, j)),
        out_shape=jax.ShapeDtypeStruct((depth, 8, n3), F32),
        compiler_params=_params(("arbitrary", "arbitrary")),
        name="ada",
    )(cond_rows, w_ada, b_ada.reshape(depth, 1, n3))


def _inproj_kernel(x_ref, g_ref, ml_ref, mc_ref, *refs, tm, tn, rc, ctx_len, d):
    w_refs, o_ref, h_ref = refs[:-2], refs[-2], refs[-1]
    i = pl.program_id(1)
    j = pl.program_id(2)

    @pl.when(j == 0)
    def _():
        g = g_ref[...]
        sh_l = ml_ref[:, 0:d]
        sc_l = 1.0 + ml_ref[:, d:2 * d]
        sh_c = mc_ref[:, 0:d]
        sc_c = 1.0 + mc_ref[:, d:2 * d]

        def body(c, carry):
            r0 = pl.multiple_of(c * rc, rc)
            is_ctx = jnp.logical_and(i == 0, r0 < ctx_len)
            sh = jnp.where(is_ctx, sh_c, sh_l)
            sc = jnp.where(is_ctx, sc_c, sc_l)
            x = x_ref[pl.ds(r0, rc), :]
            ms = jnp.mean(x * x, axis=-1, keepdims=True)
            y = x * lax.rsqrt(ms + EPS) * g
            h_ref[pl.ds(r0, rc), :] = (y * sc + sh).astype(BF16)
            return carry

        lax.fori_loop(0, tm // rc, body, 0, unroll=2)

    for u, w_ref in enumerate(w_refs):
        o_ref[:, u * tn:(u + 1) * tn] = jnp.dot(
            h_ref[...], w_ref[...], preferred_element_type=F32).astype(o_ref.dtype)


def _inproj(xs, g_pre, mod_lat, mod_ctx, w, layer, tiles, *, ctx_len):
    b, t, d = xs.shape
    n = w.shape[-1]
    tm, rc = 768, 32
    tn = tiles["tn"]
    src = tiles["src"]
    nw = 2
    assert t % tm == 0 and n % (nw * tn) == 0 and ctx_len % rc == 0 and ctx_len <= tm

    def src_tile(j):
        out = j - src[0][0] + src[0][1]
        for start, ref_start in src[1:]:
            out = jnp.where(j >= start, j - start + ref_start, out)
        return out

    def w_spec(u):
        return pl.BlockSpec((None, d, tn), lambda bb, i, j: (layer, 0, src_tile(nw * j + u)))

    return pl.pallas_call(
        functools.partial(_inproj_kernel, tm=tm, tn=tn, rc=rc, ctx_len=ctx_len, d=d),
        grid=(b, t // tm, n // (nw * tn)),
        in_specs=[
            pl.BlockSpec((None, tm, d), lambda bb, i, j: (bb, i, 0)),
            pl.BlockSpec((1, d), lambda bb, i, j: (0, 0)),
            pl.BlockSpec((None, 1, 3 * d), lambda bb, i, j: (bb, 0, 0)),
            pl.BlockSpec((1, 3 * d), lambda bb, i, j: (0, 0)),
        ] + [w_spec(u) for u in range(nw)],
        out_specs=pl.BlockSpec((None, tm, nw * tn), lambda bb, i, j: (bb, i, j)),
        out_shape=jax.ShapeDtypeStruct((b, t, n), BF16),
        scratch_shapes=[pltpu.VMEM((tm, d), BF16)],
        compiler_params=_params(("arbitrary", "arbitrary", "arbitrary")),
        name="inproj",
    )(xs, g_pre, mod_lat, mod_ctx, *([w] * nw))


def _prep_kernel(p_ref, cos_ref, sin_ref, qn_ref, kn_ref, swap_ref, o_ref, *, kinds):
    cos = cos_ref[...]
    sin = sin_ref[...]
    first = {None: cos, "q": cos * qn_ref[0:1, :], "k": cos * kn_ref[0:1, :]}
    second = {None: sin, "q": sin * qn_ref[1:2, :], "k": sin * kn_ref[1:2, :]}
    swap = swap_ref[...]
    for t2 in range(len(kinds) // 2):
        pair = p_ref[:, 2 * t2 * HEAD_DIM:(2 * t2 + 2) * HEAD_DIM]
        swapped = jnp.dot(pair, swap, preferred_element_type=F32)
        for h in range(2):
            norm, scale = kinds[2 * t2 + h]
            hs = slice(h * HEAD_DIM, (h + 1) * HEAD_DIM)
            x = pair[:, hs].astype(F32)
            y = x * first[norm] + swapped[:, hs] * second[norm]
            if norm is not None:
                y = y * lax.rsqrt(jnp.mean(x * x, axis=-1, keepdims=True) + EPS)
            if scale != 1.0:
                y = y * scale
            t = 2 * t2 + h
            o_ref[:, t * HEAD_DIM:(t + 1) * HEAD_DIM] = y.astype(o_ref.dtype)


def _prep(p, cos, sin_s, qn, kn, *, kinds):
    b, t, n = p.shape
    tp = 256
    npre = len(kinds) * HEAD_DIM
    assert t % tp == 0 and len(kinds) % 2 == 0
    lanes = jnp.arange(2 * HEAD_DIM)
    swap = (lanes[:, None] == (lanes[None, :] ^ 32)).astype(BF16)
    with_swapped = lambda g: jnp.stack([g, g[jnp.arange(HEAD_DIM) ^ 32]])
    return pl.pallas_call(
        functools.partial(_prep_kernel, kinds=kinds),
        grid=(b, t // tp),
        in_specs=[
            pl.BlockSpec((None, tp, npre), lambda bb, i: (bb, i, 0)),
            pl.BlockSpec((tp, HEAD_DIM), lambda bb, i: (i, 0)),
            pl.BlockSpec((tp, HEAD_DIM), lambda bb, i: (i, 0)),
            pl.BlockSpec((2, HEAD_DIM), lambda bb, i: (0, 0)),
            pl.BlockSpec((2, HEAD_DIM), lambda bb, i: (0, 0)),
            pl.BlockSpec((2 * HEAD_DIM, 2 * HEAD_DIM), lambda bb, i: (0, 0)),
        ],
        out_specs=pl.BlockSpec((None, tp, npre), lambda bb, i: (bb, i, 0)),
        out_shape=jax.ShapeDtypeStruct(p.shape, p.dtype),
        input_output_aliases={0: 0},
        compiler_params=_params(("arbitrary", "arbitrary")),
        name="prep",
    )(p, cos, sin_s, with_swapped(qn), with_swapped(kn), swap)


VT_ROWS = HEAD_DIM + 16
SOFTMAX_ROWS = 128
STEPS_PER_TRIP = 2


def _ones_row_block(width):
    first = lax.broadcasted_iota(jnp.int32, (VT_ROWS - HEAD_DIM, width), 0) == 0
    return jnp.where(first, 1.0, 0.0).astype(BF16)


def _mixa_kernel(sink_ref, q_ref, z_ref, k_ref, v_ref, o_ref, vt_ref, s_ref, p_ref,
                 *, tq, nsub, ctx_len, s_lat):
    kv = pl.program_id(1)
    blk = pl.program_id(2)
    wl = tq + 2 * WINDOW
    nkeys = ctx_len + wl

    @pl.when(blk == 0)
    def _():
        def tbody(c, carry):
            r0 = pl.multiple_of(c * HEAD_DIM, HEAD_DIM)
            vt_ref[c, 0:HEAD_DIM, :] = v_ref[pl.ds(r0, HEAD_DIM), :].astype(F32).T.astype(BF16)
            vt_ref[c, HEAD_DIM:VT_ROWS, :] = _ones_row_block(HEAD_DIM)
            return carry

        lax.fori_loop(0, (ctx_len + s_lat) // HEAD_DIM, tbody, 0)

    def geometry(u):
        q0 = (blk * nsub + u) * tq - ctx_len
        cs = pl.multiple_of(jnp.clip(q0 - WINDOW, 0, s_lat - wl), HEAD_DIM)
        return q0, cs

    def scores(u):
        q0, cs = geometry(u)
        keys = jnp.concatenate([k_ref[0:ctx_len, :], k_ref[pl.ds(ctx_len + cs, wl), :]], axis=0)
        ss = [lax.dot_general(keys, q_ref[u * tq:(u + 1) * tq, g * HEAD_DIM:(g + 1) * HEAD_DIM], NT_DIMS,
                              preferred_element_type=F32) for g in range(GROUP)]
        ms = []
        for g in range(GROUP):
            s_ref[u, g, 0:ctx_len, :] = ss[g][0:ctx_len]
            ms.append(jnp.maximum(jnp.max(ss[g][0:ctx_len], axis=0, keepdims=True), sink_ref[kv * GROUP + g]))
        row = lax.broadcasted_iota(jnp.int32, (SOFTMAX_ROWS, tq), 0)
        col = lax.broadcasted_iota(jnp.int32, (SOFTMAX_ROWS, tq), 1)
        diff = row - col
        for r in range(0, wl, SOFTMAX_ROWS):
            rel = diff + (cs + r - q0)
            valid = jnp.logical_and(jnp.abs(rel) <= WINDOW, q0 >= 0)
            for g in range(GROUP):
                piece = jnp.where(valid, ss[g][ctx_len + r:ctx_len + r + SOFTMAX_ROWS], -jnp.inf)
                s_ref[u, g, ctx_len + r:ctx_len + r + SOFTMAX_ROWS, :] = piece
                ms[g] = jnp.maximum(ms[g], jnp.max(piece, axis=0, keepdims=True))
        return ms

    def softmax(u, ms):
        for g in range(GROUP):
            for r in range(0, nkeys, SOFTMAX_ROWS):
                p_ref[u, g, r:r + SOFTMAX_ROWS, :] = jnp.exp2(
                    s_ref[u, g, r:r + SOFTMAX_ROWS, :] - ms[g]).astype(BF16)

    def output(u, ms):
        _, cs = geometry(u)
        c0 = (ctx_len + cs) // HEAD_DIM
        tiles = [vt_ref[c] for c in range(ctx_len // HEAD_DIM)] + [vt_ref[c0 + c] for c in range(wl // HEAD_DIM)]
        vt = jnp.concatenate(tiles, axis=1)
        for g in range(GROUP):
            pv = jnp.dot(vt, p_ref[u, g], preferred_element_type=F32)
            den = pv[HEAD_DIM:HEAD_DIM + 1] + jnp.exp2(sink_ref[kv * GROUP + g] - ms[g])
            o = (pv[0:HEAD_DIM] / den).T
            z = z_ref[u * tq:(u + 1) * tq, g * HEAD_DIM:(g + 1) * HEAD_DIM].astype(F32)
            o_ref[u * tq:(u + 1) * tq, g * HEAD_DIM:(g + 1) * HEAD_DIM] = (o * _silu(z)).astype(o_ref.dtype)

    ms = {0: scores(0)}
    for u in range(nsub):
        if u + 1 < nsub:
            ms[u + 1] = scores(u + 1)
        softmax(u, ms[u])
        if u >= 1:
            output(u - 1, ms[u - 1])
    output(nsub - 1, ms[nsub - 1])


def _mixa(sink2, p, *, lay, ctx_len):
    b, t, _ = p.shape
    tq, nsub = 256, 3
    gw = GROUP * HEAD_DIM
    nkv = lay["akv"]
    s_lat = t - ctx_len
    wl = tq + 2 * WINDOW
    tb = tq * nsub
    assert t % tb == 0 and ctx_len % tq == 0 and s_lat >= wl and WINDOW == HEAD_DIM
    qoff, zoff, koff, voff = lay["qa"] // GROUP, lay["za"] // GROUP, lay["ka"], lay["va"]
    return pl.pallas_call(
        functools.partial(_mixa_kernel, tq=tq, nsub=nsub, ctx_len=ctx_len, s_lat=s_lat),
        grid=(b, nkv, t // tb),
        in_specs=[
            pl.BlockSpec(memory_space=pltpu.SMEM),
            pl.BlockSpec((None, tb, gw), lambda bb, k, i: (bb, i, qoff + k)),
            pl.BlockSpec((None, tb, gw), lambda bb, k, i: (bb, i, zoff + k)),
            pl.BlockSpec((None, t, HEAD_DIM), lambda bb, k, i: (bb, 0, koff + k)),
            pl.BlockSpec((None, t, HEAD_DIM), lambda bb, k, i: (bb, 0, voff + k)),
        ],
        out_specs=pl.BlockSpec((None, tb, gw), lambda bb, k, i: (bb, i, k)),
        out_shape=jax.ShapeDtypeStruct((b, t, lay["nh"] * HEAD_DIM), BF16),
        scratch_shapes=[pltpu.VMEM((t // HEAD_DIM, VT_ROWS, HEAD_DIM), BF16),
                        pltpu.VMEM((nsub, GROUP, ctx_len + wl, tq), F32),
                        pltpu.VMEM((nsub, GROUP, ctx_len + wl, tq), BF16)],
        compiler_params=_params(("arbitrary", "arbitrary", "arbitrary")),
        name="mixa",
    )(sink2, p, p, p, p)


def _mixc_kernel(q_ref, z_ref, k_ref, v_ref, oin_ref, o_ref,
                 vt_ref, qt_ref, s_ref, p_ref, a_ref, mc_ref, m_ref, acc_ref, *, tq, tk, ctx_len, nchunk, kvs):
    del oin_ref
    qi = pl.program_id(2)
    units = [(kvi, kvi * GROUP + g) for kvi in range(kvs) for g in range(GROUP)]

    def kv_cols(kvi):
        return slice(kvi * HEAD_DIM, (kvi + 1) * HEAD_DIM)

    @pl.when(qi == 0)
    def _():
        def tbody(c, carry):
            r0 = pl.multiple_of(c * tk, tk)
            for kvi in range(kvs):
                vt_ref[kvi, c, 0:HEAD_DIM, :] = v_ref[pl.ds(r0, tk), kv_cols(kvi)].astype(F32).T.astype(BF16)
                vt_ref[kvi, c, HEAD_DIM:VT_ROWS, :] = _ones_row_block(tk)
            return carry

        lax.fori_loop(0, nchunk, tbody, 0)

    for _, u in units:
        qt_ref[u] = q_ref[:, u * HEAD_DIM:(u + 1) * HEAD_DIM].astype(F32).T.astype(BF16)

    def scores(c, buf, units=units):
        r0 = pl.multiple_of(c * tk, tk)
        for kvi, u in units:
            s = jnp.dot(k_ref[pl.ds(r0, tk), kv_cols(kvi)], qt_ref[u], preferred_element_type=F32)
            s_ref[buf, u] = s
            mc_ref[buf, u] = jnp.max(s, axis=0, keepdims=True)

    def softmax(buf, first):
        for _, u in units:
            m_cur = mc_ref[buf, u]
            if first:
                m_new = m_cur
            else:
                m_prev = m_ref[u]
                m_new = jnp.maximum(m_prev, m_cur)
                a_ref[buf, u] = jnp.exp2(m_prev - m_new)
            m_ref[u] = m_new
            for r in range(0, tk, SOFTMAX_ROWS):
                p_ref[buf, u, r:r + SOFTMAX_ROWS, :] = jnp.exp2(
                    s_ref[buf, u, r:r + SOFTMAX_ROWS, :] - m_new).astype(BF16)

    def output(c, buf, first, units=units):
        for kvi, u in units:
            pv = jnp.dot(vt_ref[kvi, c], p_ref[buf, u], preferred_element_type=F32)
            acc_ref[u] = pv if first else a_ref[buf, u] * acc_ref[u] + pv

    nctx = ctx_len // tq

    @pl.when(qi < nctx)
    def _():
        for kvi, u in units:
            s = jnp.dot(k_ref[0:ctx_len, kv_cols(kvi)], qt_ref[u], preferred_element_type=F32)
            p = jnp.exp2(s - jnp.max(s, axis=0, keepdims=True))
            acc_ref[u] = jnp.dot(vt_ref[kvi, 0, :, 0:ctx_len], p.astype(BF16), preferred_element_type=F32)

    @pl.when(qi >= nctx)
    def _():
        def step(i, par, first=False):
            softmax(par, False)
            for lo in range(0, len(units), GROUP):
                scores(i + 1, 1 - par, units[lo:lo + GROUP])
                output(i - 1, 1 - par, first, units[lo:lo + GROUP])

        scores(0, 0)
        scores(1, 1)
        softmax(0, True)
        mid = nchunk - 2
        if mid >= 1:
            step(1, 1, first=True)
        rest = max(mid - 1, 0)
        trips = rest // STEPS_PER_TRIP

        def body(j, carry):
            i = 2 + STEPS_PER_TRIP * j
            for k in range(STEPS_PER_TRIP):
                step(i + k, k % 2)
            return carry

        lax.fori_loop(0, trips, body, 0)
        for i in range(2 + trips * STEPS_PER_TRIP, nchunk - 1):
            step(i, i % 2)
        last = nchunk - 1
        softmax(last % 2, False)
        output(last - 1, (last - 1) % 2, last == 1)
        output(last, last % 2, False)

    o = jnp.concatenate([(acc_ref[u, 0:HEAD_DIM, :] / acc_ref[u, HEAD_DIM:HEAD_DIM + 1, :]).T
                         for _, u in units], axis=1)
    o_ref[...] = (o * _silu(z_ref[...].astype(F32))).astype(o_ref.dtype)


def _mixc(p, o_prev, *, lay, ctx_len):
    b, t, _ = p.shape
    tq, tk = 256, 768
    nkv = lay["ckv"]
    kvs = 2 if nkv % 2 == 0 else 1
    nu = kvs * GROUP
    gw = nu * HEAD_DIM
    kw = kvs * HEAD_DIM
    assert t % tq == 0 and ctx_len % tq == 0 and t % tk == 0 and ctx_len <= tk
    assert all(lay[name] % nu == 0 for name in ("qc", "zc", "oc")) and lay["kc"] % kvs == 0 and lay["vc"] % kvs == 0
    qoff, zoff, ooff = lay["qc"] // nu, lay["zc"] // nu, lay["oc"] // nu
    koff, voff = lay["kc"] // kvs, lay["vc"] // kvs
    return pl.pallas_call(
        functools.partial(_mixc_kernel, tq=tq, tk=tk, ctx_len=ctx_len, nchunk=t // tk, kvs=kvs),
        grid=(b, nkv // kvs, t // tq),
        in_specs=[
            pl.BlockSpec((None, tq, gw), lambda bb, k, i: (bb, i, qoff + k)),
            pl.BlockSpec((None, tq, gw), lambda bb, k, i: (bb, i, zoff + k)),
            pl.BlockSpec((None, t, kw), lambda bb, k, i: (bb, 0, koff + k)),
            pl.BlockSpec((None, t, kw), lambda bb, k, i: (bb, 0, voff + k)),
            pl.BlockSpec(memory_space=pl.ANY),
        ],
        out_specs=pl.BlockSpec((None, tq, gw), lambda bb, k, i: (bb, i, ooff + k)),
        out_shape=jax.ShapeDtypeStruct(o_prev.shape, o_prev.dtype),
        input_output_aliases={4: 0},
        scratch_shapes=[pltpu.VMEM((kvs, t // tk, VT_ROWS, tk), BF16), pltpu.VMEM((nu, HEAD_DIM, tq), BF16),
                        pltpu.VMEM((2, nu, tk, tq), F32), pltpu.VMEM((2, nu, tk, tq), BF16),
                        pltpu.VMEM((2, nu, 1, tq), F32), pltpu.VMEM((2, nu, 1, tq), F32),
                        pltpu.VMEM((nu, 1, tq), F32),
                        pltpu.VMEM((nu, VT_ROWS, tq), F32)],
        compiler_params=_params(("arbitrary", "arbitrary", "arbitrary")),
        name="mixc",
    )(p, p, p, p, o_prev)


def _ret_state_kernel(gc_ref, kf_ref, vf_ref, kb_ref, vb_ref, wk_ref, of_ref, ob_ref, sf_ref, sb_ref, *, nh):
    step = pl.program_id(1)

    @pl.when(step == 0)
    def _():
        sf_ref[...] = jnp.zeros(sf_ref.shape, F32)
        sb_ref[...] = jnp.zeros(sb_ref.shape, F32)

    heads = [slice(h * HEAD_DIM, (h + 1) * HEAD_DIM) for h in range(nh)]
    for d, (k_ref, v_ref, o_ref, s_ref, order) in enumerate(
            ((kf_ref, vf_ref, of_ref, sf_ref, (0, 1)), (kb_ref, vb_ref, ob_ref, sb_ref, (1, 0)))):
        wk = wk_ref[d]
        for ci in order:
            rs = slice(ci * RET_CHUNK, (ci + 1) * RET_CHUNK)
            k = (k_ref[rs, :].astype(F32) * wk).astype(BF16)
            v = v_ref[rs, :]
            o_ref[ci] = s_ref[...].astype(o_ref.dtype)
            kvs = [lax.dot_general(k[:, hs], v[:, hs], TN_DIMS, preferred_element_type=F32) for hs in heads]
            for h, hs in enumerate(heads):
                s_ref[hs, :] = gc_ref[d, h] * s_ref[hs, :] + kvs[h]


def _ret_states(gc, p, wk, *, lay):
    b, t, _ = p.shape
    nh = lay["bh"]
    bw = nh * HEAD_DIM
    tb = 2 * RET_CHUNK
    nblk = t // tb
    koff, voff = lay["kb"] // nh, lay["vb"] // nh
    assert t % tb == 0

    def bwd_blk(s):
        return jnp.where(s == 0, 0, nblk - s)

    state = jax.ShapeDtypeStruct((b, t // RET_CHUNK, bw, HEAD_DIM), BF16)
    return pl.pallas_call(
        functools.partial(_ret_state_kernel, nh=nh),
        grid=(b, nblk),
        in_specs=[
            pl.BlockSpec(memory_space=pltpu.SMEM),
            pl.BlockSpec((None, tb, bw), lambda bb, s: (bb, s, koff)),
            pl.BlockSpec((None, tb, bw), lambda bb, s: (bb, s, voff)),
            pl.BlockSpec((None, tb, bw), lambda bb, s: (bb, bwd_blk(s), koff)),
            pl.BlockSpec((None, tb, bw), lambda bb, s: (bb, bwd_blk(s), voff)),
            pl.BlockSpec((2, RET_CHUNK, bw), lambda bb, s: (0, 0, 0)),
        ],
        out_specs=[pl.BlockSpec((None, 2, bw, HEAD_DIM), lambda bb, s: (bb, s, 0, 0)),
                   pl.BlockSpec((None, 2, bw, HEAD_DIM), lambda bb, s: (bb, bwd_blk(s), 0, 0))],
        out_shape=[state, state],
        scratch_shapes=[pltpu.VMEM((bw, HEAD_DIM), F32), pltpu.VMEM((bw, HEAD_DIM), F32)],
        compiler_params=_params(("arbitrary", "arbitrary")),
        name="ret_states",
    )(gc, p, p, p, p, wk)


def _ret_out_kernel(q_ref, k_ref, v_ref, z_ref, spf_ref, spb_ref, dm_ref, wq_ref, oin_ref, o_ref, *, nh, nc):
    del oin_ref
    heads = [slice(h * HEAD_DIM, (h + 1) * HEAD_DIM) for h in range(nh)]
    for c in range(nc):
        rs = slice(c * RET_CHUNK, (c + 1) * RET_CHUNK)
        qs = [q_ref[rs, hs] for hs in heads]
        intra = [lax.dot_general(qs[h], k_ref[rs, heads[h]], NT_DIMS, preferred_element_type=F32) * dm_ref[h]
                 for h in range(nh)]
        outs = []
        for h, hs in enumerate(heads):
            q32 = qs[h].astype(F32)
            lhs = jnp.concatenate([intra[h].astype(BF16),
                                   (q32 * wq_ref[0, :, hs]).astype(BF16),
                                   (q32 * wq_ref[1, :, hs]).astype(BF16)], axis=1)
            rhs = jnp.concatenate([v_ref[rs, hs], spf_ref[c, hs, :], spb_ref[c, hs, :]], axis=0)
            outs.append(jnp.dot(lhs, rhs, preferred_element_type=F32))
        for h, hs in enumerate(heads):
            o = outs[h]
            mu = jnp.mean(o, axis=-1, keepdims=True)
            oc = o - mu
            var = jnp.mean(oc * oc, axis=-1, keepdims=True)
            y = oc * lax.rsqrt(var + EPS)
            o_ref[rs, hs] = (y * _silu(z_ref[rs, hs].astype(F32))).astype(o_ref.dtype)


def _ret_out(p, sprev_f, sprev_b, dmat, wq, o_prev, *, lay):
    b, t, _ = p.shape
    nh = lay["bh"]
    bw = nh * HEAD_DIM
    nc = 6 if t % (6 * RET_CHUNK) == 0 else 2
    tb = nc * RET_CHUNK
    assert t % tb == 0
    qoff, koff, voff, zoff = lay["qb"] // nh, lay["kb"] // nh, lay["vb"] // nh, lay["zb"] // nh
    ooff = lay["ob"] // nh
    return pl.pallas_call(
        functools.partial(_ret_out_kernel, nh=nh, nc=nc),
        grid=(b, t // tb),
        in_specs=[
            pl.BlockSpec((None, tb, bw), lambda bb, i: (bb, i, qoff)),
            pl.BlockSpec((None, tb, bw), lambda bb, i: (bb, i, koff)),
            pl.BlockSpec((None, tb, bw), lambda bb, i: (bb, i, voff)),
            pl.BlockSpec((None, tb, bw), lambda bb, i: (bb, i, zoff)),
            pl.BlockSpec((None, nc, bw, HEAD_DIM), lambda bb, i: (bb, i, 0, 0)),
            pl.BlockSpec((None, nc, bw, HEAD_DIM), lambda bb, i: (bb, i, 0, 0)),
            pl.BlockSpec((nh, RET_CHUNK, RET_CHUNK), lambda bb, i: (0, 0, 0)),
            pl.BlockSpec((2, RET_CHUNK, bw), lambda bb, i: (0, 0, 0)),
            pl.BlockSpec(memory_space=pl.ANY),
        ],
        out_specs=pl.BlockSpec((None, tb, bw), lambda bb, i: (bb, i, ooff)),
        out_shape=jax.ShapeDtypeStruct(o_prev.shape, o_prev.dtype),
        input_output_aliases={8: 0},
        compiler_params=_params(("arbitrary", "arbitrary")),
        name="ret_out",
    )(p, p, p, p, sprev_f, sprev_b, dmat, wq, o_prev)


def _outproj_kernel(o_ref, w_ref, x_ref, gl_ref, gc_ref, gp_ref, y_ref, *, tm, rc, first_lat, segs):
    i = pl.program_id(1)
    o = jnp.concatenate([o_ref[:, a:b] for a, b in segs], axis=1)
    y_ref[...] = jnp.dot(o, w_ref[...], preferred_element_type=F32)
    gain = jnp.where(i < first_lat, gc_ref[...], gl_ref[...]) * gp_ref[...]

    def body(c, carry):
        r0 = pl.multiple_of(c * rc, rc)
        y = y_ref[pl.ds(r0, rc), :]
        ms = jnp.mean(y * y, axis=-1, keepdims=True)
        y_ref[pl.ds(r0, rc), :] = x_ref[pl.ds(r0, rc), :] + (y * lax.rsqrt(ms + EPS)) * gain
        return carry

    lax.fori_loop(0, tm // rc, body, 0)


def _outproj(o, w, layer, xs, gate_lat, gate_ctx, g_post, *, ctx_len, segs, latent_only):
    b, t, d = xs.shape
    kdim = o.shape[-1]
    tm, rc = 256, 64
    assert t % tm == 0 and ctx_len % tm == 0 and w.shape[1] == kdim
    nctx = ctx_len // tm
    off = nctx if latent_only else 0
    first_lat = 0 if latent_only else nctx
    t_out = t - off * tm
    return pl.pallas_call(
        functools.partial(_outproj_kernel, tm=tm, rc=rc, first_lat=first_lat, segs=segs),
        grid=(b, t_out // tm),
        in_specs=[
            pl.BlockSpec((None, tm, kdim), lambda bb, i: (bb, i + off, 0)),
            pl.BlockSpec((None, kdim, d), lambda bb, i: (layer, 0, 0), pipeline_mode=pl.Buffered(1)),
            pl.BlockSpec((None, tm, d), lambda bb, i: (bb, i + off, 0)),
            pl.BlockSpec((None, 1, d), lambda bb, i: (bb, 0, 0)),
            pl.BlockSpec((1, d), lambda bb, i: (0, 0)),
            pl.BlockSpec((1, d), lambda bb, i: (0, 0)),
        ],
        out_specs=pl.BlockSpec((None, tm, d), lambda bb, i: (bb, i, 0)),
        out_shape=jax.ShapeDtypeStruct((b, t_out, d), xs.dtype),
        compiler_params=pltpu.CompilerParams(dimension_semantics=("arbitrary", "arbitrary"),
                                             vmem_limit_bytes=OUTPROJ_VMEM_LIMIT),
        name="outproj",
    )(o, w, xs, gate_lat, gate_ctx, g_post)


def _layout(d_model):
    nh = d_model // HEAD_DIM
    aq = 3 * nh // 8
    akv = aq // GROUP
    bh = nh // 4
    cq = nh - aq - bh
    ckv = cq // GROUP
    order = [("qa", aq), ("qc", cq), ("ka", akv), ("kc", ckv), ("qb", bh), ("kb", bh),
             ("za", aq), ("zc", cq), ("zb", bh), ("va", akv), ("vc", ckv), ("vb", bh)]
    lay = {"nh": nh, "aq": aq, "akv": akv, "bh": bh, "cq": cq, "ckv": ckv}
    off = 0
    for name, width in order:
        lay[name] = off
        off += width
    lay["nprep"] = lay["za"]
    lay["oa"], lay["oc"], lay["ob"] = 0, aq, aq + cq
    ref_order = [("qa", aq), ("ka", akv), ("va", akv), ("qb", bh), ("kb", bh), ("vb", bh),
                 ("qc", cq), ("kc", ckv), ("vc", ckv), ("za", aq), ("zb", bh), ("zc", cq)]
    ref_off, off = {}, 0
    for name, width in ref_order:
        ref_off[name] = off
        off += width
    lay["o_segs"] = ((0, aq * HEAD_DIM), ((aq + cq) * HEAD_DIM, nh * HEAD_DIM),
                     (aq * HEAD_DIM, (aq + cq) * HEAD_DIM))
    tu = math.gcd(4, *[width for _, width in order])
    lay["tiles"] = {"tn": tu * HEAD_DIM,
                    "src": tuple((lay[name] // tu, ref_off[name] // tu) for name, _ in order)}
    scale = HEAD_DIM ** -0.5
    lay["kinds"] = tuple(
        [(None, scale * LOG2E)] * aq + [("q", scale * LOG2E)] * cq + [(None, 1.0)] * akv
        + [("k", 1.0)] * ckv + [(None, 1.0)] * bh + [(None, scale)] * bh)
    return lay


def _rope_tables(n_lat, ctx_len):
    rows = n_lat // GRID_W
    row = jnp.repeat(jnp.arange(rows, dtype=F32), GRID_W)
    col = jnp.tile(jnp.arange(GRID_W, dtype=F32), rows)
    axis_dim = HEAD_DIM // 2
    inv = ROPE_BASE ** (-jnp.arange(0, axis_dim, 2, dtype=F32) / axis_dim)
    ang_r = row[:, None] * inv
    ang_c = col[:, None] * inv
    ang = jnp.concatenate([ang_r, ang_r, ang_c, ang_c], axis=-1)
    sign = jnp.where((jnp.arange(HEAD_DIM) // 32) % 2 == 1, 1.0, -1.0).astype(F32)
    cos = jnp.concatenate([jnp.ones((ctx_len, HEAD_DIM), F32), jnp.cos(ang)], axis=0)
    sin_s = jnp.concatenate([jnp.zeros((ctx_len, HEAD_DIM), F32), jnp.sin(ang) * sign], axis=0)
    return cos, sin_s


def _decay_tables(ret_decay_l):
    log_g = jax.nn.log_sigmoid(ret_decay_l.astype(F32))
    nh = log_g.shape[1]
    pos = jnp.arange(RET_CHUNK, dtype=F32)
    lane = lambda w: jnp.repeat(w, HEAD_DIM, axis=-1)
    wk_f = jnp.exp((RET_CHUNK - 1 - pos)[:, None] * log_g[0][None])
    wk_b = jnp.exp(pos[:, None] * log_g[1][None])
    wq_f = jnp.exp((pos + 1.0)[:, None] * log_g[0][None])
    wq_b = jnp.exp((RET_CHUNK - pos)[:, None] * log_g[1][None])
    wk = jnp.stack([lane(wk_f), lane(wk_b)])
    wq = jnp.stack([lane(wq_f), lane(wq_b)])
    gc = jnp.exp(RET_CHUNK * log_g)
    diff = pos[:, None] - pos[None, :]
    ef = jnp.where((diff >= 0)[None], diff[None] * log_g[0][:, None, None], -jnp.inf)
    eb = jnp.where((diff <= 0)[None], -diff[None] * log_g[1][:, None, None], -jnp.inf)
    dmat = jnp.exp(ef) + jnp.exp(eb)
    del nh
    return wk, wq, gc, dmat


def kernel(x, c, ctx, c_ctx, w_ada, b_ada, g_pre, g_post, w_in, w_out, sink_a, qnorm_c, knorm_c, ret_decay):
    bsz, n_lat, d = x.shape
    ctx_len = ctx.shape[1]
    depth = w_ada.shape[0]
    lay = _layout(d)

    cond_rows = jnp.zeros((8, d), F32).at[0:bsz].set(c).at[bsz].set(c_ctx)
    mods = _ada(cond_rows, w_ada, b_ada)
    cos, sin_s = _rope_tables(n_lat, ctx_len)
    w_in_p = w_in.astype(BF16)
    w_out_p = w_out.astype(BF16)

    xs = jnp.concatenate([ctx, x], axis=1)
    for l in range(depth):
        mod_lat = mods[l, 0:bsz].reshape(bsz, 1, 3 * d)
        mod_ctx = mods[l, bsz:bsz + 1]
        p = _inproj(xs, g_pre[l].reshape(1, d), mod_lat, mod_ctx, w_in_p, l, lay["tiles"], ctx_len=ctx_len)
        p = _prep(p, cos, sin_s, qnorm_c[l], knorm_c[l], kinds=lay["kinds"])
        o = _mixa(sink_a[l] * LOG2E, p, lay=lay, ctx_len=ctx_len)
        o = _mixc(p, o, lay=lay, ctx_len=ctx_len)
        wk, wq, gc, dmat = _decay_tables(ret_decay[l])
        sprev_f, sprev_b = _ret_states(gc, p, wk, lay=lay)
        o = _ret_out(p, sprev_f, sprev_b, dmat, wq, o, lay=lay)
        xs = _outproj(o, w_out_p, l, xs, mod_lat[:, :, 2 * d:], mod_ctx[:, 2 * d:],
                      g_post[l].reshape(1, d), ctx_len=ctx_len, segs=lay["o_segs"],
                      latent_only=(l == depth - 1))
    return xs
```

```python
import functools
import math

import jax
import jax.numpy as jnp
from jax import lax
from jax.experimental import pallas as pl
from jax.experimental.pallas import tpu as pltpu

HEAD_DIM = 128
GRID_W = 64
WINDOW = 128
RET_CHUNK = 128
ROPE_BASE = 10000.0
EPS = 1e-6
LOG2E = math.log2(math.e)
GROUP = 3
VMEM_LIMIT = 56 * 1024 * 1024
OUTPROJ_VMEM_LIMIT = 60 * 1024 * 1024

F32 = jnp.float32
BF16 = jnp.bfloat16
NT_DIMS = (((1,), (1,)), ((), ()))
TN_DIMS = (((0,), (0,)), ((), ()))


def _params(sem):
    return pltpu.CompilerParams(dimension_semantics=sem, vmem_limit_bytes=VMEM_LIMIT)


def _silu(z):
    return z / (1.0 + jnp.exp(-z))


def _ada_kernel(c_ref, w_ref, b_ref, o_ref):
    cond = _silu(c_ref[...])
    o_ref[...] = jnp.dot(cond.astype(BF16), w_ref[...].astype(BF16),
                         preferred_element_type=F32) + b_ref[...]


def _ada(cond_rows, w_ada, b_ada):
    depth, d, n3 = w_ada.shape
    tn = 512
    return pl.pallas_call(
        _ada_kernel,
        grid=(depth, n3 // tn),
        in_specs=[
            pl.BlockSpec((8, d), lambda l, j: (0, 0)),
            pl.BlockSpec((None, d, tn), lambda l, j: (l, 0, j)),
            pl.BlockSpec((None, 1, tn), lambda l, j: (l, 0, j)),
        ],
        out_specs=pl.BlockSpec((None, 8, tn), lambda l, j: (l, 0, j)),
        out_shape=jax.ShapeDtypeStruct((depth, 8, n3), F32),
        compiler_params=_params(("arbitrary", "arbitrary")),
        name="ada",
    )(cond_rows, w_ada, b_ada.reshape(depth, 1, n3))


def _inproj_kernel(x_ref, g_ref, ml_ref, mc_ref, *refs, tm, tn, rc, ctx_len, d):
    w_refs, o_ref, h_ref = refs[:-2], refs[-2], refs[-1]
    i = pl.program_id(1)
    j = pl.program_id(2)

    @pl.when(j == 0)
    def _():
        g = g_ref[...]
        sh_l = ml_ref[:, 0:d]
        sc_l = 1.0 + ml_ref[:, d:2 * d]
        sh_c = mc_ref[:, 0:d]
        sc_c = 1.0 + mc_ref[:, d:2 * d]

        def body(c, carry):
            r0 = pl.multiple_of(c * rc, rc)
            is_ctx = jnp.logical_and(i == 0, r0 < ctx_len)
            sh = jnp.where(is_ctx, sh_c, sh_l)
            sc = jnp.where(is_ctx, sc_c, sc_l)
            x = x_ref[pl.ds(r0, rc), :]
            ms = jnp.mean(x * x, axis=-1, keepdims=True)
            y = x * lax.rsqrt(ms + EPS) * g
            h_ref[pl.ds(r0, rc), :] = (y * sc + sh).astype(BF16)
            return carry

        lax.fori_loop(0, tm // rc, body, 0, unroll=2)

    for u, w_ref in enumerate(w_refs):
        o_ref[:, u * tn:(u + 1) * tn] = jnp.dot(
            h_ref[...], w_ref[...], preferred_element_type=F32).astype(o_ref.dtype)


def _inproj(xs, g_pre, mod_lat, mod_ctx, w, layer, tiles, *, ctx_len):
    b, t, d = xs.shape
    n = w.shape[-1]
    tm, rc = 768, 32
    tn = tiles["tn"]
    src = tiles["src"]
    nw = 2
    assert t % tm == 0 and n % (nw * tn) == 0 and ctx_len % rc == 0 and ctx_len <= tm

    def src_tile(j):
        out = j - src[0][0] + src[0][1]
        for start, ref_start in src[1:]:
            out = jnp.where(j >= start, j - start + ref_start, out)
        return out

    def w_spec(u):
        return pl.BlockSpec((None, d, tn), lambda bb, i, j: (layer, 0, src_tile(nw * j + u)))

    return pl.pallas_call(
        functools.partial(_inproj_kernel, tm=tm, tn=tn, rc=rc, ctx_len=ctx_len, d=d),
        grid=(b, t // tm, n // (nw * tn)),
        in_specs=[
            pl.BlockSpec((None, tm, d), lambda bb, i, j: (bb, i, 0)),
            pl.BlockSpec((1, d), lambda bb, i, j: (0, 0)),
            pl.BlockSpec((None, 1, 3 * d), lambda bb, i, j: (bb, 0, 0)),
            pl.BlockSpec((1, 3 * d), lambda bb, i, j: (0, 0)),
        ] + [w_spec(u) for u in range(nw)],
        out_specs=pl.BlockSpec((None, tm, nw * tn), lambda bb, i, j: (bb, i, j)),
        out_shape=jax.ShapeDtypeStruct((b, t, n), BF16),
        scratch_shapes=[pltpu.VMEM((tm, d), BF16)],
        compiler_params=_params(("arbitrary", "arbitrary", "arbitrary")),
        name="inproj",
    )(xs, g_pre, mod_lat, mod_ctx, *([w] * nw))


def _prep_kernel(p_ref, cos_ref, sin_ref, qn_ref, kn_ref, swap_ref, o_ref, *, kinds):
    cos = cos_ref[...]
    sin = sin_ref[...]
    first = {None: cos, "q": cos * qn_ref[0:1, :], "k": cos * kn_ref[0:1, :]}
    second = {None: sin, "q": sin * qn_ref[1:2, :], "k": sin * kn_ref[1:2, :]}
    swap = swap_ref[...]
    for t2 in range(len(kinds) // 2):
        pair = p_ref[:, 2 * t2 * HEAD_DIM:(2 * t2 + 2) * HEAD_DIM]
        swapped = jnp.dot(pair, swap, preferred_element_type=F32)
        for h in range(2):
            norm, scale = kinds[2 * t2 + h]
            hs = slice(h * HEAD_DIM, (h + 1) * HEAD_DIM)
            x = pair[:, hs].astype(F32)
            y = x * first[norm] + swapped[:, hs] * second[norm]
            if norm is not None:
                y = y * lax.rsqrt(jnp.mean(x * x, axis=-1, keepdims=True) + EPS)
            if scale != 1.0:
                y = y * scale
            t = 2 * t2 + h
            o_ref[:, t * HEAD_DIM:(t + 1) * HEAD_DIM] = y.astype(o_ref.dtype)


def _prep(p, cos, sin_s, qn, kn, *, kinds):
    b, t, n = p.shape
    tp = 256
    npre = len(kinds) * HEAD_DIM
    assert t % tp == 0 and len(kinds) % 2 == 0
    lanes = jnp.arange(2 * HEAD_DIM)
    swap = (lanes[:, None] == (lanes[None, :] ^ 32)).astype(BF16)
    with_swapped = lambda g: jnp.stack([g, g[jnp.arange(HEAD_DIM) ^ 32]])
    return pl.pallas_call(
        functools.partial(_prep_kernel, kinds=kinds),
        grid=(b, t // tp),
        in_specs=[
            pl.BlockSpec((None, tp, npre), lambda bb, i: (bb, i, 0)),
            pl.BlockSpec((tp, HEAD_DIM), lambda bb, i: (i, 0)),
            pl.BlockSpec((tp, HEAD_DIM), lambda bb, i: (i, 0)),
            pl.BlockSpec((2, HEAD_DIM), lambda bb, i: (0, 0)),
            pl.BlockSpec((2, HEAD_DIM), lambda bb, i: (0, 0)),
            pl.BlockSpec((2 * HEAD_DIM, 2 * HEAD_DIM), lambda bb, i: (0, 0)),
        ],
        out_specs=pl.BlockSpec((None, tp, npre), lambda bb, i: (bb, i, 0)),
        out_shape=jax.ShapeDtypeStruct(p.shape, p.dtype),
        input_output_aliases={0: 0},
        compiler_params=_params(("arbitrary", "arbitrary")),
        name="prep",
    )(p, cos, sin_s, with_swapped(qn), with_swapped(kn), swap)


VT_ROWS = HEAD_DIM + 16
SOFTMAX_ROWS = 128
STEPS_PER_TRIP = 2


def _ones_row_block(width):
    first = lax.broadcasted_iota(jnp.int32, (VT_ROWS - HEAD_DIM, width), 0) == 0
    return jnp.where(first, 1.0, 0.0).astype(BF16)


def _mixa_kernel(sink_ref, q_ref, z_ref, k_ref, v_ref, o_ref, vt_ref, s_ref, p_ref, m_ref, e_ref,
                 *, tq, nsub, ctx_len, s_lat):
    kv = pl.program_id(1)
    blk = pl.program_id(2)
    wl = tq + 2 * WINDOW
    nkeys = ctx_len + wl

    @pl.when(blk == 0)
    def _():
        def tbody(c, carry):
            r0 = pl.multiple_of(c * HEAD_DIM, HEAD_DIM)
            vt_ref[c, 0:HEAD_DIM, :] = v_ref[pl.ds(r0, HEAD_DIM), :].astype(F32).T.astype(BF16)
            vt_ref[c, HEAD_DIM:VT_ROWS, :] = _ones_row_block(HEAD_DIM)
            return carry

        lax.fori_loop(0, (ctx_len + s_lat) // HEAD_DIM, tbody, 0)

    def geometry(u):
        q0 = (blk * nsub + u) * tq - ctx_len
        cs = pl.multiple_of(jnp.clip(q0 - WINDOW, 0, s_lat - wl), HEAD_DIM)
        return q0, cs

    def q_rows(u):
        return pl.ds(pl.multiple_of(u * tq, tq), tq)

    def head_cols(g):
        return slice(g * HEAD_DIM, (g + 1) * HEAD_DIM)

    def scores(u, buf):
        q0, cs = geometry(u)
        keys = jnp.concatenate([k_ref[0:ctx_len, :], k_ref[pl.ds(ctx_len + cs, wl), :]], axis=0)
        ss = [lax.dot_general(keys, q_ref[q_rows(u), head_cols(g)], NT_DIMS, preferred_element_type=F32)
              for g in range(GROUP)]
        ms = []
        for g in range(GROUP):
            s_ref[buf, g, 0:ctx_len, :] = ss[g][0:ctx_len]
            ms.append(jnp.maximum(jnp.max(ss[g][0:ctx_len], axis=0, keepdims=True), sink_ref[kv * GROUP + g]))
        row = lax.broadcasted_iota(jnp.int32, (SOFTMAX_ROWS, tq), 0)
        col = lax.broadcasted_iota(jnp.int32, (SOFTMAX_ROWS, tq), 1)
        diff = row - col
        for r in range(0, wl, SOFTMAX_ROWS):
            rel = diff + (cs + r - q0)
            valid = jnp.logical_and(jnp.abs(rel) <= WINDOW, q0 >= 0)
            for g in range(GROUP):
                piece = jnp.where(valid, ss[g][ctx_len + r:ctx_len + r + SOFTMAX_ROWS], -jnp.inf)
                s_ref[buf, g, ctx_len + r:ctx_len + r + SOFTMAX_ROWS, :] = piece
                ms[g] = jnp.maximum(ms[g], jnp.max(piece, axis=0, keepdims=True))
        for g in range(GROUP):
            m_ref[buf, g] = ms[g]

    def softmax(buf):
        for g in range(GROUP):
            m = m_ref[buf, g]
            e_ref[buf, g] = jnp.exp2(sink_ref[kv * GROUP + g] - m)
            for r in range(0, nkeys, SOFTMAX_ROWS):
                p_ref[buf, g, r:r + SOFTMAX_ROWS, :] = jnp.exp2(
                    s_ref[buf, g, r:r + SOFTMAX_ROWS, :] - m).astype(BF16)

    def output(u, buf):
        _, cs = geometry(u)
        c0 = (ctx_len + cs) // HEAD_DIM
        tiles = [vt_ref[c] for c in range(ctx_len // HEAD_DIM)] + [vt_ref[c0 + c] for c in range(wl // HEAD_DIM)]
        vt = jnp.concatenate(tiles, axis=1)
        for g in range(GROUP):
            pv = jnp.dot(vt, p_ref[buf, g], preferred_element_type=F32)
            den = pv[HEAD_DIM:HEAD_DIM + 1] + e_ref[buf, g]
            o = (pv[0:HEAD_DIM] / den).T
            z = z_ref[q_rows(u), head_cols(g)].astype(F32)
            o_ref[q_rows(u), head_cols(g)] = (o * _silu(z)).astype(o_ref.dtype)

    def step(u, par):
        softmax(par)
        scores(u + 1, 1 - par)
        output(u - 1, 1 - par)

    scores(0, 0)
    scores(1, 1)
    softmax(0)
    step(1, 1)
    trips = (nsub - 3) // STEPS_PER_TRIP

    def body(j, carry):
        u = 2 + STEPS_PER_TRIP * j
        for k in range(STEPS_PER_TRIP):
            step(u + k, k % 2)
        return carry

    lax.fori_loop(0, trips, body, 0)
    for u in range(2 + trips * STEPS_PER_TRIP, nsub - 1):
        step(u, u % 2)
    last = nsub - 1
    softmax(last % 2)
    output(last - 1, (last - 1) % 2)
    output(last, last % 2)


def _mixa(sink2, p, *, lay, ctx_len):
    b, t, _ = p.shape
    tq = 256
    nsub = 11 if t % (11 * tq) == 0 else 3
    gw = GROUP * HEAD_DIM
    nkv = lay["akv"]
    s_lat = t - ctx_len
    wl = tq + 2 * WINDOW
    tb = tq * nsub
    assert t % tb == 0 and ctx_len % tq == 0 and s_lat >= wl and WINDOW == HEAD_DIM and nsub >= 3
    qoff, zoff, koff, voff = lay["qa"] // GROUP, lay["za"] // GROUP, lay["ka"], lay["va"]
    return pl.pallas_call(
        functools.partial(_mixa_kernel, tq=tq, nsub=nsub, ctx_len=ctx_len, s_lat=s_lat),
        grid=(b, nkv, t // tb),
        in_specs=[
            pl.BlockSpec(memory_space=pltpu.SMEM),
            pl.BlockSpec((None, tb, gw), lambda bb, k, i: (bb, i, qoff + k)),
            pl.BlockSpec((None, tb, gw), lambda bb, k, i: (bb, i, zoff + k)),
            pl.BlockSpec((None, t, HEAD_DIM), lambda bb, k, i: (bb, 0, koff + k)),
            pl.BlockSpec((None, t, HEAD_DIM), lambda bb, k, i: (bb, 0, voff + k)),
        ],
        out_specs=pl.BlockSpec((None, tb, gw), lambda bb, k, i: (bb, i, k)),
        out_shape=jax.ShapeDtypeStruct((b, t, lay["nh"] * HEAD_DIM), BF16),
        scratch_shapes=[pltpu.VMEM((t // HEAD_DIM, VT_ROWS, HEAD_DIM), BF16),
                        pltpu.VMEM((2, GROUP, ctx_len + wl, tq), F32),
                        pltpu.VMEM((2, GROUP, ctx_len + wl, tq), BF16),
                        pltpu.VMEM((2, GROUP, 1, tq), F32), pltpu.VMEM((2, GROUP, 1, tq), F32)],
        compiler_params=_params(("arbitrary", "arbitrary", "arbitrary")),
        name="mixa",
    )(sink2, p, p, p, p)


def _mixc_kernel(q_ref, z_ref, k_ref, v_ref, oin_ref, o_ref,
                 vt_ref, qt_ref, s_ref, p_ref, a_ref, mc_ref, m_ref, acc_ref, *, tq, tk, ctx_len, nchunk, kvs):
    del oin_ref
    qi = pl.program_id(2)
    units = [(kvi, kvi * GROUP + g) for kvi in range(kvs) for g in range(GROUP)]

    def kv_cols(kvi):
        return slice(kvi * HEAD_DIM, (kvi + 1) * HEAD_DIM)

    @pl.when(qi == 0)
    def _():
        def tbody(c, carry):
            r0 = pl.multiple_of(c * tk, tk)
            for kvi in range(kvs):
                vt_ref[kvi, c, 0:HEAD_DIM, :] = v_ref[pl.ds(r0, tk), kv_cols(kvi)].astype(F32).T.astype(BF16)
                vt_ref[kvi, c, HEAD_DIM:VT_ROWS, :] = _ones_row_block(tk)
            return carry

        lax.fori_loop(0, nchunk, tbody, 0)

    for _, u in units:
        qt_ref[u] = q_ref[:, u * HEAD_DIM:(u + 1) * HEAD_DIM].astype(F32).T.astype(BF16)

    def scores(c, buf, units=units):
        r0 = pl.multiple_of(c * tk, tk)
        for kvi, u in units:
            s = jnp.dot(k_ref[pl.ds(r0, tk), kv_cols(kvi)], qt_ref[u], preferred_element_type=F32)
            s_ref[buf, u] = s
            mc_ref[buf, u] = jnp.max(s, axis=0, keepdims=True)

    def softmax(buf, first):
        for _, u in units:
            m_cur = mc_ref[buf, u]
            if first:
                m_new = m_cur
            else:
                m_prev = m_ref[u]
                m_new = jnp.maximum(m_prev, m_cur)
                a_ref[buf, u] = jnp.exp2(m_prev - m_new)
            m_ref[u] = m_new
            for r in range(0, tk, SOFTMAX_ROWS):
                p_ref[buf, u, r:r + SOFTMAX_ROWS, :] = jnp.exp2(
                    s_ref[buf, u, r:r + SOFTMAX_ROWS, :] - m_new).astype(BF16)

    def output(c, buf, first, units=units):
        for kvi, u in units:
            pv = jnp.dot(vt_ref[kvi, c], p_ref[buf, u], preferred_element_type=F32)
            acc_ref[u] = pv if first else a_ref[buf, u] * acc_ref[u] + pv

    nctx = ctx_len // tq

    @pl.when(qi < nctx)
    def _():
        for kvi, u in units:
            s = jnp.dot(k_ref[0:ctx_len, kv_cols(kvi)], qt_ref[u], preferred_element_type=F32)
            p = jnp.exp2(s - jnp.max(s, axis=0, keepdims=True))
            acc_ref[u] = jnp.dot(vt_ref[kvi, 0, :, 0:ctx_len], p.astype(BF16), preferred_element_type=F32)

    @pl.when(qi >= nctx)
    def _():
        def step(i, par, first=False):
            softmax(par, False)
            for lo in range(0, len(units), GROUP):
                scores(i + 1, 1 - par, units[lo:lo + GROUP])
                output(i - 1, 1 - par, first, units[lo:lo + GROUP])

        scores(0, 0)
        scores(1, 1)
        softmax(0, True)
        mid = nchunk - 2
        if mid >= 1:
            step(1, 1, first=True)
        rest = max(mid - 1, 0)
        trips = rest // STEPS_PER_TRIP

        def body(j, carry):
            i = 2 + STEPS_PER_TRIP * j
            for k in range(STEPS_PER_TRIP):
                step(i + k, k % 2)
            return carry

        lax.fori_loop(0, trips, body, 0)
        for i in range(2 + trips * STEPS_PER_TRIP, nchunk - 1):
            step(i, i % 2)
        last = nchunk - 1
        softmax(last % 2, False)
        output(last - 1, (last - 1) % 2, last == 1)
        output(last, last % 2, False)

    o = jnp.concatenate([(acc_ref[u, 0:HEAD_DIM, :] / acc_ref[u, HEAD_DIM:HEAD_DIM + 1, :]).T
                         for _, u in units], axis=1)
    o_ref[...] = (o * _silu(z_ref[...].astype(F32))).astype(o_ref.dtype)


def _mixc(p, o_prev, *, lay, ctx_len):
    b, t, _ = p.shape
    tq, tk = 256, 768
    nkv = lay["ckv"]
    kvs = 2 if nkv % 2 == 0 else 1
    nu = kvs * GROUP
    gw = nu * HEAD_DIM
    kw = kvs * HEAD_DIM
    assert t % tq == 0 and ctx_len % tq == 0 and t % tk == 0 and ctx_len <= tk
    assert all(lay[name] % nu == 0 for name in ("qc", "zc", "oc")) and lay["kc"] % kvs == 0 and lay["vc"] % kvs == 0
    qoff, zoff, ooff = lay["qc"] // nu, lay["zc"] // nu, lay["oc"] // nu
    koff, voff = lay["kc"] // kvs, lay["vc"] // kvs
    return pl.pallas_call(
        functools.partial(_mixc_kernel, tq=tq, tk=tk, ctx_len=ctx_len, nchunk=t // tk, kvs=kvs),
        grid=(b, nkv // kvs, t // tq),
        in_specs=[
            pl.BlockSpec((None, tq, gw), lambda bb, k, i: (bb, i, qoff + k)),
            pl.BlockSpec((None, tq, gw), lambda bb, k, i: (bb, i, zoff + k)),
            pl.BlockSpec((None, t, kw), lambda bb, k, i: (bb, 0, koff + k)),
            pl.BlockSpec((None, t, kw), lambda bb, k, i: (bb, 0, voff + k)),
            pl.BlockSpec(memory_space=pl.ANY),
        ],
        out_specs=pl.BlockSpec((None, tq, gw), lambda bb, k, i: (bb, i, ooff + k)),
        out_shape=jax.ShapeDtypeStruct(o_prev.shape, o_prev.dtype),
        input_output_aliases={4: 0},
        scratch_shapes=[pltpu.VMEM((kvs, t // tk, VT_ROWS, tk), BF16), pltpu.VMEM((nu, HEAD_DIM, tq), BF16),
                        pltpu.VMEM((2, nu, tk, tq), F32), pltpu.VMEM((2, nu, tk, tq), BF16),
                        pltpu.VMEM((2, nu, 1, tq), F32), pltpu.VMEM((2, nu, 1, tq), F32),
                        pltpu.VMEM((nu, 1, tq), F32),
                        pltpu.VMEM((nu, VT_ROWS, tq), F32)],
        compiler_params=_params(("arbitrary", "arbitrary", "arbitrary")),
        name="mixc",
    )(p, p, p, p, o_prev)


def _ret_state_kernel(gc_ref, kf_ref, vf_ref, kb_ref, vb_ref, wk_ref, of_ref, ob_ref, sf_ref, sb_ref, *, nh):
    step = pl.program_id(1)

    @pl.when(step == 0)
    def _():
        sf_ref[...] = jnp.zeros(sf_ref.shape, F32)
        sb_ref[...] = jnp.zeros(sb_ref.shape, F32)

    heads = [slice(h * HEAD_DIM, (h + 1) * HEAD_DIM) for h in range(nh)]
    for d, (k_ref, v_ref, o_ref, s_ref, order) in enumerate(
            ((kf_ref, vf_ref, of_ref, sf_ref, (0, 1)), (kb_ref, vb_ref, ob_ref, sb_ref, (1, 0)))):
        wk = wk_ref[d]
        for ci in order:
            rs = slice(ci * RET_CHUNK, (ci + 1) * RET_CHUNK)
            k = (k_ref[rs, :].astype(F32) * wk).astype(BF16)
            v = v_ref[rs, :]
            o_ref[ci] = s_ref[...].astype(o_ref.dtype)
            kvs = [lax.dot_general(k[:, hs], v[:, hs], TN_DIMS, preferred_element_type=F32) for hs in heads]
            for h, hs in enumerate(heads):
                s_ref[hs, :] = gc_ref[d, h] * s_ref[hs, :] + kvs[h]


def _ret_states(gc, p, wk, *, lay):
    b, t, _ = p.shape
    nh = lay["bh"]
    bw = nh * HEAD_DIM
    tb = 2 * RET_CHUNK
    nblk = t // tb
    koff, voff = lay["kb"] // nh, lay["vb"] // nh
    assert t % tb == 0

    def bwd_blk(s):
        return jnp.where(s == 0, 0, nblk - s)

    state = jax.ShapeDtypeStruct((b, t // RET_CHUNK, bw, HEAD_DIM), BF16)
    return pl.pallas_call(
        functools.partial(_ret_state_kernel, nh=nh),
        grid=(b, nblk),
        in_specs=[
            pl.BlockSpec(memory_space=pltpu.SMEM),
            pl.BlockSpec((None, tb, bw), lambda bb, s: (bb, s, koff)),
            pl.BlockSpec((None, tb, bw), lambda bb, s: (bb, s, voff)),
            pl.BlockSpec((None, tb, bw), lambda bb, s: (bb, bwd_blk(s), koff)),
            pl.BlockSpec((None, tb, bw), lambda bb, s: (bb, bwd_blk(s), voff)),
            pl.BlockSpec((2, RET_CHUNK, bw), lambda bb, s: (0, 0, 0)),
        ],
        out_specs=[pl.BlockSpec((None, 2, bw, HEAD_DIM), lambda bb, s: (bb, s, 0, 0)),
                   pl.BlockSpec((None, 2, bw, HEAD_DIM), lambda bb, s: (bb, bwd_blk(s), 0, 0))],
        out_shape=[state, state],
        scratch_shapes=[pltpu.VMEM((bw, HEAD_DIM), F32), pltpu.VMEM((bw, HEAD_DIM), F32)],
        compiler_params=_params(("arbitrary", "arbitrary")),
        name="ret_states",
    )(gc, p, p, p, p, wk)


def _ret_out_kernel(q_ref, k_ref, v_ref, z_ref, spf_ref, spb_ref, dm_ref, wq_ref, oin_ref, o_ref, *, nh, nc):
    del oin_ref
    heads = [slice(h * HEAD_DIM, (h + 1) * HEAD_DIM) for h in range(nh)]
    for c in range(nc):
        rs = slice(c * RET_CHUNK, (c + 1) * RET_CHUNK)
        qs = [q_ref[rs, hs] for hs in heads]
        intra = [lax.dot_general(qs[h], k_ref[rs, heads[h]], NT_DIMS, preferred_element_type=F32) * dm_ref[h]
                 for h in range(nh)]
        outs = []
        for h, hs in enumerate(heads):
            q32 = qs[h].astype(F32)
            lhs = jnp.concatenate([intra[h].astype(BF16),
                                   (q32 * wq_ref[0, :, hs]).astype(BF16),
                                   (q32 * wq_ref[1, :, hs]).astype(BF16)], axis=1)
            rhs = jnp.concatenate([v_ref[rs, hs], spf_ref[c, hs, :], spb_ref[c, hs, :]], axis=0)
            outs.append(jnp.dot(lhs, rhs, preferred_element_type=F32))
        for h, hs in enumerate(heads):
            o = outs[h]
            mu = jnp.mean(o, axis=-1, keepdims=True)
            oc = o - mu
            var = jnp.mean(oc * oc, axis=-1, keepdims=True)
            y = oc * lax.rsqrt(var + EPS)
            o_ref[rs, hs] = (y * _silu(z_ref[rs, hs].astype(F32))).astype(o_ref.dtype)


def _ret_out(p, sprev_f, sprev_b, dmat, wq, o_prev, *, lay):
    b, t, _ = p.shape
    nh = lay["bh"]
    bw = nh * HEAD_DIM
    nc = 6 if t % (6 * RET_CHUNK) == 0 else 2
    tb = nc * RET_CHUNK
    assert t % tb == 0
    qoff, koff, voff, zoff = lay["qb"] // nh, lay["kb"] // nh, lay["vb"] // nh, lay["zb"] // nh
    ooff = lay["ob"] // nh
    return pl.pallas_call(
        functools.partial(_ret_out_kernel, nh=nh, nc=nc),
        grid=(b, t // tb),
        in_specs=[
            pl.BlockSpec((None, tb, bw), lambda bb, i: (bb, i, qoff)),
            pl.BlockSpec((None, tb, bw), lambda bb, i: (bb, i, koff)),
            pl.BlockSpec((None, tb, bw), lambda bb, i: (bb, i, voff)),
            pl.BlockSpec((None, tb, bw), lambda bb, i: (bb, i, zoff)),
            pl.BlockSpec((None, nc, bw, HEAD_DIM), lambda bb, i: (bb, i, 0, 0)),
            pl.BlockSpec((None, nc, bw, HEAD_DIM), lambda bb, i: (bb, i, 0, 0)),
            pl.BlockSpec((nh, RET_CHUNK, RET_CHUNK), lambda bb, i: (0, 0, 0)),
            pl.BlockSpec((2, RET_CHUNK, bw), lambda bb, i: (0, 0, 0)),
            pl.BlockSpec(memory_space=pl.ANY),
        ],
        out_specs=pl.BlockSpec((None, tb, bw), lambda bb, i: (bb, i, ooff)),
        out_shape=jax.ShapeDtypeStruct(o_prev.shape, o_prev.dtype),
        input_output_aliases={8: 0},
        compiler_params=_params(("arbitrary", "arbitrary")),
        name="ret_out",
    )(p, p, p, p, sprev_f, sprev_b, dmat, wq, o_prev)


def _outproj_kernel(o_ref, w_ref, x_ref, gl_ref, gc_ref, gp_ref, y_ref, *, tm, rc, first_lat, segs):
    i = pl.program_id(1)
    o = jnp.concatenate([o_ref[:, a:b] for a, b in segs], axis=1)
    y_ref[...] = jnp.dot(o, w_ref[...], preferred_element_type=F32)
    gain = jnp.where(i < first_lat, gc_ref[...], gl_ref[...]) * gp_ref[...]

    def body(c, carry):
        r0 = pl.multiple_of(c * rc, rc)
        y = y_ref[pl.ds(r0, rc), :]
        ms = jnp.mean(y * y, axis=-1, keepdims=True)
        y_ref[pl.ds(r0, rc), :] = x_ref[pl.ds(r0, rc), :] + (y * lax.rsqrt(ms + EPS)) * gain
        return carry

    lax.fori_loop(0, tm // rc, body, 0)


def _outproj(o, w, layer, xs, gate_lat, gate_ctx, g_post, *, ctx_len, segs, latent_only):
    b, t, d = xs.shape
    kdim = o.shape[-1]
    tm, rc = 256, 64
    assert t % tm == 0 and ctx_len % tm == 0 and w.shape[1] == kdim
    nctx = ctx_len // tm
    off = nctx if latent_only else 0
    first_lat = 0 if latent_only else nctx
    t_out = t - off * tm
    return pl.pallas_call(
        functools.partial(_outproj_kernel, tm=tm, rc=rc, first_lat=first_lat, segs=segs),
        grid=(b, t_out // tm),
        in_specs=[
            pl.BlockSpec((None, tm, kdim), lambda bb, i: (bb, i + off, 0)),
            pl.BlockSpec((None, kdim, d), lambda bb, i: (layer, 0, 0), pipeline_mode=pl.Buffered(1)),
            pl.BlockSpec((None, tm, d), lambda bb, i: (bb, i + off, 0)),
            pl.BlockSpec((None, 1, d), lambda bb, i: (bb, 0, 0)),
            pl.BlockSpec((1, d), lambda bb, i: (0, 0)),
            pl.BlockSpec((1, d), lambda bb, i: (0, 0)),
        ],
        out_specs=pl.BlockSpec((None, tm, d), lambda bb, i: (bb, i, 0)),
        out_shape=jax.ShapeDtypeStruct((b, t_out, d), xs.dtype),
        compiler_params=pltpu.CompilerParams(dimension_semantics=("arbitrary", "arbitrary"),
                                             vmem_limit_bytes=OUTPROJ_VMEM_LIMIT),
        name="outproj",
    )(o, w, xs, gate_lat, gate_ctx, g_post)


def _layout(d_model):
    nh = d_model // HEAD_DIM
    aq = 3 * nh // 8
    akv = aq // GROUP
    bh = nh // 4
    cq = nh - aq - bh
    ckv = cq // GROUP
    order = [("qa", aq), ("qc", cq), ("ka", akv), ("kc", ckv), ("qb", bh), ("kb", bh),
             ("za", aq), ("zc", cq), ("zb", bh), ("va", akv), ("vc", ckv), ("vb", bh)]
    lay = {"nh": nh, "aq": aq, "akv": akv, "bh": bh, "cq": cq, "ckv": ckv}
    off = 0
    for name, width in order:
        lay[name] = off
        off += width
    lay["nprep"] = lay["za"]
    lay["oa"], lay["oc"], lay["ob"] = 0, aq, aq + cq
    ref_order = [("qa", aq), ("ka", akv), ("va", akv), ("qb", bh), ("kb", bh), ("vb", bh),
                 ("qc", cq), ("kc", ckv), ("vc", ckv), ("za", aq), ("zb", bh), ("zc", cq)]
    ref_off, off = {}, 0
    for name, width in ref_order:
        ref_off[name] = off
        off += width
    lay["o_segs"] = ((0, aq * HEAD_DIM), ((aq + cq) * HEAD_DIM, nh * HEAD_DIM),
                     (aq * HEAD_DIM, (aq + cq) * HEAD_DIM))
    tu = math.gcd(4, *[width for _, width in order])
    lay["tiles"] = {"tn": tu * HEAD_DIM,
                    "src": tuple((lay[name] // tu, ref_off[name] // tu) for name, _ in order)}
    scale = HEAD_DIM ** -0.5
    lay["kinds"] = tuple(
        [(None, scale * LOG2E)] * aq + [("q", scale * LOG2E)] * cq + [(None, 1.0)] * akv
        + [("k", 1.0)] * ckv + [(None, 1.0)] * bh + [(None, scale)] * bh)
    return lay


def _rope_tables(n_lat, ctx_len):
    rows = n_lat // GRID_W
    row = jnp.repeat(jnp.arange(rows, dtype=F32), GRID_W)
    col = jnp.tile(jnp.arange(GRID_W, dtype=F32), rows)
    axis_dim = HEAD_DIM // 2
    inv = ROPE_BASE ** (-jnp.arange(0, axis_dim, 2, dtype=F32) / axis_dim)
    ang_r = row[:, None] * inv
    ang_c = col[:, None] * inv
    ang = jnp.concatenate([ang_r, ang_r, ang_c, ang_c], axis=-1)
    sign = jnp.where((jnp.arange(HEAD_DIM) // 32) % 2 == 1, 1.0, -1.0).astype(F32)
    cos = jnp.concatenate([jnp.ones((ctx_len, HEAD_DIM), F32), jnp.cos(ang)], axis=0)
    sin_s = jnp.concatenate([jnp.zeros((ctx_len, HEAD_DIM), F32), jnp.sin(ang) * sign], axis=0)
    return cos, sin_s


def _decay_tables(ret_decay_l):
    log_g = jax.nn.log_sigmoid(ret_decay_l.astype(F32))
    nh = log_g.shape[1]
    pos = jnp.arange(RET_CHUNK, dtype=F32)
    lane = lambda w: jnp.repeat(w, HEAD_DIM, axis=-1)
    wk_f = jnp.exp((RET_CHUNK - 1 - pos)[:, None] * log_g[0][None])
    wk_b = jnp.exp(pos[:, None] * log_g[1][None])
    wq_f = jnp.exp((pos + 1.0)[:, None] * log_g[0][None])
    wq_b = jnp.exp((RET_CHUNK - pos)[:, None] * log_g[1][None])
    wk = jnp.stack([lane(wk_f), lane(wk_b)])
    wq = jnp.stack([lane(wq_f), lane(wq_b)])
    gc = jnp.exp(RET_CHUNK * log_g)
    diff = pos[:, None] - pos[None, :]
    ef = jnp.where((diff >= 0)[None], diff[None] * log_g[0][:, None, None], -jnp.inf)
    eb = jnp.where((diff <= 0)[None], -diff[None] * log_g[1][:, None, None], -jnp.inf)
    dmat = jnp.exp(ef) + jnp.exp(eb)
    del nh
    return wk, wq, gc, dmat


def kernel(x, c, ctx, c_ctx, w_ada, b_ada, g_pre, g_post, w_in, w_out, sink_a, qnorm_c, knorm_c, ret_decay):
    bsz, n_lat, d = x.shape
    ctx_len = ctx.shape[1]
    depth = w_ada.shape[0]
    lay = _layout(d)

    cond_rows = jnp.zeros((8, d), F32).at[0:bsz].set(c).at[bsz].set(c_ctx)
    mods = _ada(cond_rows, w_ada, b_ada)
    cos, sin_s = _rope_tables(n_lat, ctx_len)
    w_in_p = w_in.astype(BF16)
    w_out_p = w_out.astype(BF16)

    xs = jnp.concatenate([ctx, x], axis=1)
    for l in range(depth):
        mod_lat = mods[l, 0:bsz].reshape(bsz, 1, 3 * d)
        mod_ctx = mods[l, bsz:bsz + 1]
        p = _inproj(xs, g_pre[l].reshape(1, d), mod_lat, mod_ctx, w_in_p, l, lay["tiles"], ctx_len=ctx_len)
        p = _prep(p, cos, sin_s, qnorm_c[l], knorm_c[l], kinds=lay["kinds"])
        o = _mixa(sink_a[l] * LOG2E, p, lay=lay, ctx_len=ctx_len)
        o = _mixc(p, o, lay=lay, ctx_len=ctx_len)
        wk, wq, gc, dmat = _decay_tables(ret_decay[l])
        sprev_f, sprev_b = _ret_states(gc, p, wk, lay=lay)
        o = _ret_out(p, sprev_f, sprev_b, dmat, wq, o, lay=lay)
        xs = _outproj(o, w_out_p, l, xs, mod_lat[:, :, 2 * d:], mod_ctx[:, 2 * d:],
                      g_post[l].reshape(1, d), ctx_len=ctx_len, segs=lay["o_segs"],
                      latent_only=(l == depth - 1))
    return xs
```

```python
import functools
import math

import jax
import jax.numpy as jnp
from jax import lax
from jax.experimental import pallas as pl
from jax.experimental.pallas import tpu as pltpu

HEAD_DIM = 128
GRID_W = 64
WINDOW = 128
RET_CHUNK = 128
ROPE_BASE = 10000.0
EPS = 1e-6
LOG2E = math.log2(math.e)
GROUP = 3
VMEM_LIMIT = 56 * 1024 * 1024
OUTPROJ_VMEM_LIMIT = 60 * 1024 * 1024
INPROJ_TILES_PER_STEP = 3

F32 = jnp.float32
BF16 = jnp.bfloat16
NT_DIMS = (((1,), (1,)), ((), ()))
TN_DIMS = (((0,), (0,)), ((), ()))


def _params(sem):
    return pltpu.CompilerParams(dimension_semantics=sem, vmem_limit_bytes=VMEM_LIMIT)


def _silu(z):
    return z / (1.0 + jnp.exp(-z))


def _ada_kernel(c_ref, w_ref, b_ref, o_ref):
    cond = _silu(c_ref[...])
    o_ref[...] = jnp.dot(cond.astype(BF16), w_ref[...].astype(BF16),
                         preferred_element_type=F32) + b_ref[...]


def _ada(cond_rows, w_ada, b_ada):
    depth, d, n3 = w_ada.shape
    tn = 512
    return pl.pallas_call(
        _ada_kernel,
        grid=(depth, n3 // tn),
        in_specs=[
            pl.BlockSpec((8, d), lambda l, j: (0, 0)),
            pl.BlockSpec((None, d, tn), lambda l, j: (l, 0, j)),
            pl.BlockSpec((None, 1, tn), lambda l, j: (l, 0, j)),
        ],
        out_specs=pl.BlockSpec((None, 8, tn), lambda l, j: (l, 0, j)),
        out_shape=jax.ShapeDtypeStruct((depth, 8, n3), F32),
        compiler_params=_params(("arbitrary", "arbitrary")),
        name="ada",
    )(cond_rows, w_ada, b_ada.reshape(depth, 1, n3))


def _prenorm_kernel(x_ref, g_ref, ml_ref, mc_ref, h_ref, *, tm, rc, ctx_len, d):
    i = pl.program_id(1)
    g = g_ref[...]
    sh_l = ml_ref[:, 0:d]
    sc_l = 1.0 + ml_ref[:, d:2 * d]
    sh_c = mc_ref[:, 0:d]
    sc_c = 1.0 + mc_ref[:, d:2 * d]

    def body(c, carry):
        r0 = pl.multiple_of(c * rc, rc)
        is_ctx = jnp.logical_and(i == 0, r0 < ctx_len)
        sh = jnp.where(is_ctx, sh_c, sh_l)
        sc = jnp.where(is_ctx, sc_c, sc_l)
        x = x_ref[pl.ds(r0, rc), :]
        ms = jnp.mean(x * x, axis=-1, keepdims=True)
        y = x * lax.rsqrt(ms + EPS) * g
        h_ref[pl.ds(r0, rc), :] = (y * sc + sh).astype(BF16)
        return carry

    lax.fori_loop(0, tm // rc, body, 0, unroll=2)


def _prenorm(xs, g_pre, mod_lat, mod_ctx, *, ctx_len):
    b, t, d = xs.shape
    tm, rc = 768, 32
    assert t % tm == 0 and ctx_len % rc == 0 and ctx_len <= tm
    return pl.pallas_call(
        functools.partial(_prenorm_kernel, tm=tm, rc=rc, ctx_len=ctx_len, d=d),
        grid=(b, t // tm),
        in_specs=[
            pl.BlockSpec((None, tm, d), lambda bb, i: (bb, i, 0)),
            pl.BlockSpec((1, d), lambda bb, i: (0, 0)),
            pl.BlockSpec((None, 1, 3 * d), lambda bb, i: (bb, 0, 0)),
            pl.BlockSpec((1, 3 * d), lambda bb, i: (0, 0)),
        ],
        out_specs=pl.BlockSpec((None, tm, d), lambda bb, i: (bb, i, 0)),
        out_shape=jax.ShapeDtypeStruct((b, t, d), BF16),
        compiler_params=_params(("arbitrary", "arbitrary")),
        name="prenorm",
    )(xs, g_pre, mod_lat, mod_ctx)


def _inproj_kernel(h_ref, *refs, tn):
    w_refs, o_ref = refs[:-1], refs[-1]
    for u, w_ref in enumerate(w_refs):
        o_ref[:, u * tn:(u + 1) * tn] = jnp.dot(
            h_ref[...], w_ref[...], preferred_element_type=F32).astype(o_ref.dtype)


def _inproj(h, w, layer, tiles):
    b, t, d = h.shape
    n = w.shape[-1]
    tm = 1408 if t % 1408 == 0 else 768
    tn = tiles["tn"]
    src = tiles["src"]
    nw = INPROJ_TILES_PER_STEP
    assert t % tm == 0 and n % (nw * tn) == 0

    def src_tile(j):
        out = j - src[0][0] + src[0][1]
        for start, ref_start in src[1:]:
            out = jnp.where(j >= start, j - start + ref_start, out)
        return out

    def w_spec(u):
        return pl.BlockSpec((None, d, tn), lambda bb, i, j: (layer, 0, src_tile(nw * j + u)))

    return pl.pallas_call(
        functools.partial(_inproj_kernel, tn=tn),
        grid=(b, t // tm, n // (nw * tn)),
        in_specs=[pl.BlockSpec((None, tm, d), lambda bb, i, j: (bb, i, 0))] + [w_spec(u) for u in range(nw)],
        out_specs=pl.BlockSpec((None, tm, nw * tn), lambda bb, i, j: (bb, i, j)),
        out_shape=jax.ShapeDtypeStruct((b, t, n), BF16),
        compiler_params=pltpu.CompilerParams(dimension_semantics=("arbitrary", "arbitrary", "arbitrary"),
                                             vmem_limit_bytes=OUTPROJ_VMEM_LIMIT),
        name="inproj",
    )(h, *([w] * nw))


def _prep_kernel(p_ref, cos_ref, sin_ref, qn_ref, kn_ref, swap_ref, o_ref, *, kinds):
    cos = cos_ref[...]
    sin = sin_ref[...]
    first = {None: cos, "q": cos * qn_ref[0:1, :], "k": cos * kn_ref[0:1, :]}
    second = {None: sin, "q": sin * qn_ref[1:2, :], "k": sin * kn_ref[1:2, :]}
    swap = swap_ref[...]
    for t2 in range(len(kinds) // 2):
        pair = p_ref[:, 2 * t2 * HEAD_DIM:(2 * t2 + 2) * HEAD_DIM]
        swapped = jnp.dot(pair, swap, preferred_element_type=F32)
        for h in range(2):
            norm, scale = kinds[2 * t2 + h]
            hs = slice(h * HEAD_DIM, (h + 1) * HEAD_DIM)
            x = pair[:, hs].astype(F32)
            y = x * first[norm] + swapped[:, hs] * second[norm]
            if norm is not None:
                y = y * lax.rsqrt(jnp.mean(x * x, axis=-1, keepdims=True) + EPS)
            if scale != 1.0:
                y = y * scale
            t = 2 * t2 + h
            o_ref[:, t * HEAD_DIM:(t + 1) * HEAD_DIM] = y.astype(o_ref.dtype)


def _prep(p, cos, sin_s, qn, kn, *, kinds):
    b, t, n = p.shape
    tp = 256
    npre = len(kinds) * HEAD_DIM
    assert t % tp == 0 and len(kinds) % 2 == 0
    lanes = jnp.arange(2 * HEAD_DIM)
    swap = (lanes[:, None] == (lanes[None, :] ^ 32)).astype(BF16)
    with_swapped = lambda g: jnp.stack([g, g[jnp.arange(HEAD_DIM) ^ 32]])
    return pl.pallas_call(
        functools.partial(_prep_kernel, kinds=kinds),
        grid=(b, t // tp),
        in_specs=[
            pl.BlockSpec((None, tp, npre), lambda bb, i: (bb, i, 0)),
            pl.BlockSpec((tp, HEAD_DIM), lambda bb, i: (i, 0)),
            pl.BlockSpec((tp, HEAD_DIM), lambda bb, i: (i, 0)),
            pl.BlockSpec((2, HEAD_DIM), lambda bb, i: (0, 0)),
            pl.BlockSpec((2, HEAD_DIM), lambda bb, i: (0, 0)),
            pl.BlockSpec((2 * HEAD_DIM, 2 * HEAD_DIM), lambda bb, i: (0, 0)),
        ],
        out_specs=pl.BlockSpec((None, tp, npre), lambda bb, i: (bb, i, 0)),
        out_shape=jax.ShapeDtypeStruct(p.shape, p.dtype),
        input_output_aliases={0: 0},
        compiler_params=_params(("arbitrary", "arbitrary")),
        name="prep",
    )(p, cos, sin_s, with_swapped(qn), with_swapped(kn), swap)


VT_ROWS = HEAD_DIM + 16
SOFTMAX_ROWS = 128
STEPS_PER_TRIP = 2


def _ones_row_block(width):
    first = lax.broadcasted_iota(jnp.int32, (VT_ROWS - HEAD_DIM, width), 0) == 0
    return jnp.where(first, 1.0, 0.0).astype(BF16)


def _mixa_kernel(sink_ref, q_ref, z_ref, k_ref, v_ref, o_ref, vt_ref, s_ref, p_ref, m_ref, e_ref,
                 *, tq, nsub, ctx_len, s_lat):
    kv = pl.program_id(1)
    blk = pl.program_id(2)
    wl = tq + 2 * WINDOW
    nkeys = ctx_len + wl

    @pl.when(blk == 0)
    def _():
        def tbody(c, carry):
            r0 = pl.multiple_of(c * HEAD_DIM, HEAD_DIM)
            vt_ref[c, 0:HEAD_DIM, :] = v_ref[pl.ds(r0, HEAD_DIM), :].astype(F32).T.astype(BF16)
            vt_ref[c, HEAD_DIM:VT_ROWS, :] = _ones_row_block(HEAD_DIM)
            return carry

        lax.fori_loop(0, (ctx_len + s_lat) // HEAD_DIM, tbody, 0)

    def geometry(u):
        q0 = (blk * nsub + u) * tq - ctx_len
        cs = pl.multiple_of(jnp.clip(q0 - WINDOW, 0, s_lat - wl), HEAD_DIM)
        return q0, cs

    def q_rows(u):
        return pl.ds(pl.multiple_of(u * tq, tq), tq)

    def head_cols(g):
        return slice(g * HEAD_DIM, (g + 1) * HEAD_DIM)

    def scores(u, buf):
        q0, cs = geometry(u)
        keys = jnp.concatenate([k_ref[0:ctx_len, :], k_ref[pl.ds(ctx_len + cs, wl), :]], axis=0)
        ss = [lax.dot_general(keys, q_ref[q_rows(u), head_cols(g)], NT_DIMS, preferred_element_type=F32)
              for g in range(GROUP)]
        ms = []
        for g in range(GROUP):
            s_ref[buf, g, 0:ctx_len, :] = ss[g][0:ctx_len]
            ms.append(jnp.maximum(jnp.max(ss[g][0:ctx_len], axis=0, keepdims=True), sink_ref[kv * GROUP + g]))
        row = lax.broadcasted_iota(jnp.int32, (SOFTMAX_ROWS, tq), 0)
        col = lax.broadcasted_iota(jnp.int32, (SOFTMAX_ROWS, tq), 1)
        diff = row - col
        for r in range(0, wl, SOFTMAX_ROWS):
            rel = diff + (cs + r - q0)
            valid = jnp.logical_and(jnp.abs(rel) <= WINDOW, q0 >= 0)
            for g in range(GROUP):
                piece = jnp.where(valid, ss[g][ctx_len + r:ctx_len + r + SOFTMAX_ROWS], -jnp.inf)
                s_ref[buf, g, ctx_len + r:ctx_len + r + SOFTMAX_ROWS, :] = piece
                ms[g] = jnp.maximum(ms[g], jnp.max(piece, axis=0, keepdims=True))
        for g in range(GROUP):
            m_ref[buf, g] = ms[g]

    def softmax(buf):
        for g in range(GROUP):
            m = m_ref[buf, g]
            e_ref[buf, g] = jnp.exp2(sink_ref[kv * GROUP + g] - m)
            for r in range(0, nkeys, SOFTMAX_ROWS):
                p_ref[buf, g, r:r + SOFTMAX_ROWS, :] = jnp.exp2(
                    s_ref[buf, g, r:r + SOFTMAX_ROWS, :] - m).astype(BF16)

    def output(u, buf):
        _, cs = geometry(u)
        c0 = (ctx_len + cs) // HEAD_DIM
        tiles = [vt_ref[c] for c in range(ctx_len // HEAD_DIM)] + [vt_ref[c0 + c] for c in range(wl // HEAD_DIM)]
        vt = jnp.concatenate(tiles, axis=1)
        for g in range(GROUP):
            pv = jnp.dot(vt, p_ref[buf, g], preferred_element_type=F32)
            den = pv[HEAD_DIM:HEAD_DIM + 1] + e_ref[buf, g]
            o = (pv[0:HEAD_DIM] / den).T
            z = z_ref[q_rows(u), head_cols(g)].astype(F32)
            o_ref[q_rows(u), head_cols(g)] = (o * _silu(z)).astype(o_ref.dtype)

    def step(u, par):
        softmax(par)
        scores(u + 1, 1 - par)
        output(u - 1, 1 - par)

    scores(0, 0)
    scores(1, 1)
    softmax(0)
    step(1, 1)
    trips = (nsub - 3) // STEPS_PER_TRIP

    def body(j, carry):
        u = 2 + STEPS_PER_TRIP * j
        for k in range(STEPS_PER_TRIP):
            step(u + k, k % 2)
        return carry

    lax.fori_loop(0, trips, body, 0)
    for u in range(2 + trips * STEPS_PER_TRIP, nsub - 1):
        step(u, u % 2)
    last = nsub - 1
    softmax(last % 2)
    output(last - 1, (last - 1) % 2)
    output(last, last % 2)


def _mixa(sink2, p, *, lay, ctx_len):
    b, t, _ = p.shape
    tq = 256
    nsub = 11 if t % (11 * tq) == 0 else 3
    gw = GROUP * HEAD_DIM
    nkv = lay["akv"]
    s_lat = t - ctx_len
    wl = tq + 2 * WINDOW
    tb = tq * nsub
    assert t % tb == 0 and ctx_len % tq == 0 and s_lat >= wl and WINDOW == HEAD_DIM and nsub >= 3
    qoff, zoff, koff, voff = lay["qa"] // GROUP, lay["za"] // GROUP, lay["ka"], lay["va"]
    return pl.pallas_call(
        functools.partial(_mixa_kernel, tq=tq, nsub=nsub, ctx_len=ctx_len, s_lat=s_lat),
        grid=(b, nkv, t // tb),
        in_specs=[
            pl.BlockSpec(memory_space=pltpu.SMEM),
            pl.BlockSpec((None, tb, gw), lambda bb, k, i: (bb, i, qoff + k)),
            pl.BlockSpec((None, tb, gw), lambda bb, k, i: (bb, i, zoff + k)),
            pl.BlockSpec((None, t, HEAD_DIM), lambda bb, k, i: (bb, 0, koff + k)),
            pl.BlockSpec((None, t, HEAD_DIM), lambda bb, k, i: (bb, 0, voff + k)),
        ],
        out_specs=pl.BlockSpec((None, tb, gw), lambda bb, k, i: (bb, i, k)),
        out_shape=jax.ShapeDtypeStruct((b, t, lay["nh"] * HEAD_DIM), BF16),
        scratch_shapes=[pltpu.VMEM((t // HEAD_DIM, VT_ROWS, HEAD_DIM), BF16),
                        pltpu.VMEM((2, GROUP, ctx_len + wl, tq), F32),
                        pltpu.VMEM((2, GROUP, ctx_len + wl, tq), BF16),
                        pltpu.VMEM((2, GROUP, 1, tq), F32), pltpu.VMEM((2, GROUP, 1, tq), F32)],
        compiler_params=_params(("arbitrary", "arbitrary", "arbitrary")),
        name="mixa",
    )(sink2, p, p, p, p)


def _mixc_kernel(q_ref, z_ref, k_ref, v_ref, oin_ref, o_ref,
                 vt_ref, qt_ref, s_ref, p_ref, a_ref, mc_ref, m_ref, acc_ref, *, tq, tk, ctx_len, nchunk, kvs):
    del oin_ref
    qi = pl.program_id(2)
    units = [(kvi, kvi * GROUP + g) for kvi in range(kvs) for g in range(GROUP)]

    def kv_cols(kvi):
        return slice(kvi * HEAD_DIM, (kvi + 1) * HEAD_DIM)

    @pl.when(qi == 0)
    def _():
        def tbody(c, carry):
            r0 = pl.multiple_of(c * tk, tk)
            for kvi in range(kvs):
                vt_ref[kvi, c, 0:HEAD_DIM, :] = v_ref[pl.ds(r0, tk), kv_cols(kvi)].astype(F32).T.astype(BF16)
                vt_ref[kvi, c, HEAD_DIM:VT_ROWS, :] = _ones_row_block(tk)
            return carry

        lax.fori_loop(0, nchunk, tbody, 0)

    for _, u in units:
        qt_ref[u] = q_ref[:, u * HEAD_DIM:(u + 1) * HEAD_DIM].astype(F32).T.astype(BF16)

    def scores(c, buf, units=units):
        r0 = pl.multiple_of(c * tk, tk)
        for kvi, u in units:
            s = jnp.dot(k_ref[pl.ds(r0, tk), kv_cols(kvi)], qt_ref[u], preferred_element_type=F32)
            s_ref[buf, u] = s
            mc_ref[buf, u] = jnp.max(s, axis=0, keepdims=True)

    def softmax(buf, first):
        for _, u in units:
            m_cur = mc_ref[buf, u]
            if first:
                m_new = m_cur
            else:
                m_prev = m_ref[u]
                m_new = jnp.maximum(m_prev, m_cur)
                a_ref[buf, u] = jnp.exp2(m_prev - m_new)
            m_ref[u] = m_new
            for r in range(0, tk, SOFTMAX_ROWS):
                p_ref[buf, u, r:r + SOFTMAX_ROWS, :] = jnp.exp2(
                    s_ref[buf, u, r:r + SOFTMAX_ROWS, :] - m_new).astype(BF16)

    def output(c, buf, first, units=units):
        for kvi, u in units:
            pv = jnp.dot(vt_ref[kvi, c], p_ref[buf, u], preferred_element_type=F32)
            acc_ref[u] = pv if first else a_ref[buf, u] * acc_ref[u] + pv

    nctx = ctx_len // tq

    @pl.when(qi < nctx)
    def _():
        for kvi, u in units:
            s = jnp.dot(k_ref[0:ctx_len, kv_cols(kvi)], qt_ref[u], preferred_element_type=F32)
            p = jnp.exp2(s - jnp.max(s, axis=0, keepdims=True))
            acc_ref[u] = jnp.dot(vt_ref[kvi, 0, :, 0:ctx_len], p.astype(BF16), preferred_element_type=F32)

    @pl.when(qi >= nctx)
    def _():
        def step(i, par, first=False):
            softmax(par, False)
            for lo in range(0, len(units), GROUP):
                scores(i + 1, 1 - par, units[lo:lo + GROUP])
                output(i - 1, 1 - par, first, units[lo:lo + GROUP])

        scores(0, 0)
        scores(1, 1)
        softmax(0, True)
        mid = nchunk - 2
        if mid >= 1:
            step(1, 1, first=True)
        rest = max(mid - 1, 0)
        trips = rest // STEPS_PER_TRIP

        def body(j, carry):
            i = 2 + STEPS_PER_TRIP * j
            for k in range(STEPS_PER_TRIP):
                step(i + k, k % 2)
            return carry

        lax.fori_loop(0, trips, body, 0)
        for i in range(2 + trips * STEPS_PER_TRIP, nchunk - 1):
            step(i, i % 2)
        last = nchunk - 1
        softmax(last % 2, False)
        output(last - 1, (last - 1) % 2, last == 1)
        output(last, last % 2, False)

    o = jnp.concatenate([(acc_ref[u, 0:HEAD_DIM, :] / acc_ref[u, HEAD_DIM:HEAD_DIM + 1, :]).T
                         for _, u in units], axis=1)
    o_ref[...] = (o * _silu(z_ref[...].astype(F32))).astype(o_ref.dtype)


def _mixc(p, o_prev, *, lay, ctx_len):
    b, t, _ = p.shape
    tq, tk = 256, 768
    nkv = lay["ckv"]
    kvs = 2 if nkv % 2 == 0 else 1
    nu = kvs * GROUP
    gw = nu * HEAD_DIM
    kw = kvs * HEAD_DIM
    assert t % tq == 0 and ctx_len % tq == 0 and t % tk == 0 and ctx_len <= tk
    assert all(lay[name] % nu == 0 for name in ("qc", "zc", "oc")) and lay["kc"] % kvs == 0 and lay["vc"] % kvs == 0
    qoff, zoff, ooff = lay["qc"] // nu, lay["zc"] // nu, lay["oc"] // nu
    koff, voff = lay["kc"] // kvs, lay["vc"] // kvs
    return pl.pallas_call(
        functools.partial(_mixc_kernel, tq=tq, tk=tk, ctx_len=ctx_len, nchunk=t // tk, kvs=kvs),
        grid=(b, nkv // kvs, t // tq),
        in_specs=[
            pl.BlockSpec((None, tq, gw), lambda bb, k, i: (bb, i, qoff + k)),
            pl.BlockSpec((None, tq, gw), lambda bb, k, i: (bb, i, zoff + k)),
            pl.BlockSpec((None, t, kw), lambda bb, k, i: (bb, 0, koff + k)),
            pl.BlockSpec((None, t, kw), lambda bb, k, i: (bb, 0, voff + k)),
            pl.BlockSpec(memory_space=pl.ANY),
        ],
        out_specs=pl.BlockSpec((None, tq, gw), lambda bb, k, i: (bb, i, ooff + k)),
        out_shape=jax.ShapeDtypeStruct(o_prev.shape, o_prev.dtype),
        input_output_aliases={4: 0},
        scratch_shapes=[pltpu.VMEM((kvs, t // tk, VT_ROWS, tk), BF16), pltpu.VMEM((nu, HEAD_DIM, tq), BF16),
                        pltpu.VMEM((2, nu, tk, tq), F32), pltpu.VMEM((2, nu, tk, tq), BF16),
                        pltpu.VMEM((2, nu, 1, tq), F32), pltpu.VMEM((2, nu, 1, tq), F32),
                        pltpu.VMEM((nu, 1, tq), F32),
                        pltpu.VMEM((nu, VT_ROWS, tq), F32)],
        compiler_params=_params(("arbitrary", "arbitrary", "arbitrary")),
        name="mixc",
    )(p, p, p, p, o_prev)


def _ret_state_kernel(gc_ref, kf_ref, vf_ref, kb_ref, vb_ref, wk_ref, of_ref, ob_ref, sf_ref, sb_ref, *, nh):
    step = pl.program_id(1)

    @pl.when(step == 0)
    def _():
        sf_ref[...] = jnp.zeros(sf_ref.shape, F32)
        sb_ref[...] = jnp.zeros(sb_ref.shape, F32)

    heads = [slice(h * HEAD_DIM, (h + 1) * HEAD_DIM) for h in range(nh)]
    for d, (k_ref, v_ref, o_ref, s_ref, order) in enumerate(
            ((kf_ref, vf_ref, of_ref, sf_ref, (0, 1)), (kb_ref, vb_ref, ob_ref, sb_ref, (1, 0)))):
        wk = wk_ref[d]
        for ci in order:
            rs = slice(ci * RET_CHUNK, (ci + 1) * RET_CHUNK)
            k = (k_ref[rs, :].astype(F32) * wk).astype(BF16)
            v = v_ref[rs, :]
            o_ref[ci] = s_ref[...].astype(o_ref.dtype)
            kvs = [lax.dot_general(k[:, hs], v[:, hs], TN_DIMS, preferred_element_type=F32) for hs in heads]
            for h, hs in enumerate(heads):
                s_ref[hs, :] = gc_ref[d, h] * s_ref[hs, :] + kvs[h]


def _ret_states(gc, p, wk, *, lay):
    b, t, _ = p.shape
    nh = lay["bh"]
    bw = nh * HEAD_DIM
    tb = 2 * RET_CHUNK
    nblk = t // tb
    koff, voff = lay["kb"] // nh, lay["vb"] // nh
    assert t % tb == 0

    def bwd_blk(s):
        return jnp.where(s == 0, 0, nblk - s)

    state = jax.ShapeDtypeStruct((b, t // RET_CHUNK, bw, HEAD_DIM), BF16)
    return pl.pallas_call(
        functools.partial(_ret_state_kernel, nh=nh),
        grid=(b, nblk),
        in_specs=[
            pl.BlockSpec(memory_space=pltpu.SMEM),
            pl.BlockSpec((None, tb, bw), lambda bb, s: (bb, s, koff)),
            pl.BlockSpec((None, tb, bw), lambda bb, s: (bb, s, voff)),
            pl.BlockSpec((None, tb, bw), lambda bb, s: (bb, bwd_blk(s), koff)),
            pl.BlockSpec((None, tb, bw), lambda bb, s: (bb, bwd_blk(s), voff)),
            pl.BlockSpec((2, RET_CHUNK, bw), lambda bb, s: (0, 0, 0)),
        ],
        out_specs=[pl.BlockSpec((None, 2, bw, HEAD_DIM), lambda bb, s: (bb, s, 0, 0)),
                   pl.BlockSpec((None, 2, bw, HEAD_DIM), lambda bb, s: (bb, bwd_blk(s), 0, 0))],
        out_shape=[state, state],
        scratch_shapes=[pltpu.VMEM((bw, HEAD_DIM), F32), pltpu.VMEM((bw, HEAD_DIM), F32)],
        compiler_params=_params(("arbitrary", "arbitrary")),
        name="ret_states",
    )(gc, p, p, p, p, wk)


def _ret_out_kernel(q_ref, k_ref, v_ref, z_ref, spf_ref, spb_ref, dm_ref, wq_ref, oin_ref, o_ref, *, nh, nc):
    del oin_ref
    heads = [slice(h * HEAD_DIM, (h + 1) * HEAD_DIM) for h in range(nh)]
    for c in range(nc):
        rs = slice(c * RET_CHUNK, (c + 1) * RET_CHUNK)
        qs = [q_ref[rs, hs] for hs in heads]
        intra = [lax.dot_general(qs[h], k_ref[rs, heads[h]], NT_DIMS, preferred_element_type=F32) * dm_ref[h]
                 for h in range(nh)]
        outs = []
        for h, hs in enumerate(heads):
            q32 = qs[h].astype(F32)
            lhs = jnp.concatenate([intra[h].astype(BF16),
                                   (q32 * wq_ref[0, :, hs]).astype(BF16),
                                   (q32 * wq_ref[1, :, hs]).astype(BF16)], axis=1)
            rhs = jnp.concatenate([v_ref[rs, hs], spf_ref[c, hs, :], spb_ref[c, hs, :]], axis=0)
            outs.append(jnp.dot(lhs, rhs, preferred_element_type=F32))
        for h, hs in enumerate(heads):
            o = outs[h]
            mu = jnp.mean(o, axis=-1, keepdims=True)
            oc = o - mu
            var = jnp.mean(oc * oc, axis=-1, keepdims=True)
            y = oc * lax.rsqrt(var + EPS)
            o_ref[rs, hs] = (y * _silu(z_ref[rs, hs].astype(F32))).astype(o_ref.dtype)


def _ret_out(p, sprev_f, sprev_b, dmat, wq, o_prev, *, lay):
    b, t, _ = p.shape
    nh = lay["bh"]
    bw = nh * HEAD_DIM
    nc = 6 if t % (6 * RET_CHUNK) == 0 else 2
    tb = nc * RET_CHUNK
    assert t % tb == 0
    qoff, koff, voff, zoff = lay["qb"] // nh, lay["kb"] // nh, lay["vb"] // nh, lay["zb"] // nh
    ooff = lay["ob"] // nh
    return pl.pallas_call(
        functools.partial(_ret_out_kernel, nh=nh, nc=nc),
        grid=(b, t // tb),
        in_specs=[
            pl.BlockSpec((None, tb, bw), lambda bb, i: (bb, i, qoff)),
            pl.BlockSpec((None, tb, bw), lambda bb, i: (bb, i, koff)),
            pl.BlockSpec((None, tb, bw), lambda bb, i: (bb, i, voff)),
            pl.BlockSpec((None, tb, bw), lambda bb, i: (bb, i, zoff)),
            pl.BlockSpec((None, nc, bw, HEAD_DIM), lambda bb, i: (bb, i, 0, 0)),
            pl.BlockSpec((None, nc, bw, HEAD_DIM), lambda bb, i: (bb, i, 0, 0)),
            pl.BlockSpec((nh, RET_CHUNK, RET_CHUNK), lambda bb, i: (0, 0, 0)),
            pl.BlockSpec((2, RET_CHUNK, bw), lambda bb, i: (0, 0, 0)),
            pl.BlockSpec(memory_space=pl.ANY),
        ],
        out_specs=pl.BlockSpec((None, tb, bw), lambda bb, i: (bb, i, ooff)),
        out_shape=jax.ShapeDtypeStruct(o_prev.shape, o_prev.dtype),
        input_output_aliases={8: 0},
        compiler_params=_params(("arbitrary", "arbitrary")),
        name="ret_out",
    )(p, p, p, p, sprev_f, sprev_b, dmat, wq, o_prev)


def _outproj_kernel(o_ref, w_ref, x_ref, gl_ref, gc_ref, gp_ref, y_ref, *, tm, rc, first_lat, segs):
    i = pl.program_id(1)
    o = jnp.concatenate([o_ref[:, a:b] for a, b in segs], axis=1)
    y_ref[...] = jnp.dot(o, w_ref[...], preferred_element_type=F32)
    gain = jnp.where(i < first_lat, gc_ref[...], gl_ref[...]) * gp_ref[...]

    def body(c, carry):
        r0 = pl.multiple_of(c * rc, rc)
        y = y_ref[pl.ds(r0, rc), :]
        ms = jnp.mean(y * y, axis=-1, keepdims=True)
        y_ref[pl.ds(r0, rc), :] = x_ref[pl.ds(r0, rc), :] + (y * lax.rsqrt(ms + EPS)) * gain
        return carry

    lax.fori_loop(0, tm // rc, body, 0)


def _outproj(o, w, layer, xs, gate_lat, gate_ctx, g_post, *, ctx_len, segs, latent_only):
    b, t, d = xs.shape
    kdim = o.shape[-1]
    tm, rc = 256, 64
    assert t % tm == 0 and ctx_len % tm == 0 and w.shape[1] == kdim
    nctx = ctx_len // tm
    off = nctx if latent_only else 0
    first_lat = 0 if latent_only else nctx
    t_out = t - off * tm
    return pl.pallas_call(
        functools.partial(_outproj_kernel, tm=tm, rc=rc, first_lat=first_lat, segs=segs),
        grid=(b, t_out // tm),
        in_specs=[
            pl.BlockSpec((None, tm, kdim), lambda bb, i: (bb, i + off, 0)),
            pl.BlockSpec((None, kdim, d), lambda bb, i: (layer, 0, 0), pipeline_mode=pl.Buffered(1)),
            pl.BlockSpec((None, tm, d), lambda bb, i: (bb, i + off, 0)),
            pl.BlockSpec((None, 1, d), lambda bb, i: (bb, 0, 0)),
            pl.BlockSpec((1, d), lambda bb, i: (0, 0)),
            pl.BlockSpec((1, d), lambda bb, i: (0, 0)),
        ],
        out_specs=pl.BlockSpec((None, tm, d), lambda bb, i: (bb, i, 0)),
        out_shape=jax.ShapeDtypeStruct((b, t_out, d), xs.dtype),
        compiler_params=pltpu.CompilerParams(dimension_semantics=("arbitrary", "arbitrary"),
                                             vmem_limit_bytes=OUTPROJ_VMEM_LIMIT),
        name="outproj",
    )(o, w, xs, gate_lat, gate_ctx, g_post)


def _layout(d_model):
    nh = d_model // HEAD_DIM
    aq = 3 * nh // 8
    akv = aq // GROUP
    bh = nh // 4
    cq = nh - aq - bh
    ckv = cq // GROUP
    order = [("qa", aq), ("qc", cq), ("ka", akv), ("kc", ckv), ("qb", bh), ("kb", bh),
             ("za", aq), ("zc", cq), ("zb", bh), ("va", akv), ("vc", ckv), ("vb", bh)]
    lay = {"nh": nh, "aq": aq, "akv": akv, "bh": bh, "cq": cq, "ckv": ckv}
    off = 0
    for name, width in order:
        lay[name] = off
        off += width
    lay["nprep"] = lay["za"]
    lay["oa"], lay["oc"], lay["ob"] = 0, aq, aq + cq
    ref_order = [("qa", aq), ("ka", akv), ("va", akv), ("qb", bh), ("kb", bh), ("vb", bh),
                 ("qc", cq), ("kc", ckv), ("vc", ckv), ("za", aq), ("zb", bh), ("zc", cq)]
    ref_off, off = {}, 0
    for name, width in ref_order:
        ref_off[name] = off
        off += width
    lay["o_segs"] = ((0, aq * HEAD_DIM), ((aq + cq) * HEAD_DIM, nh * HEAD_DIM),
                     (aq * HEAD_DIM, (aq + cq) * HEAD_DIM))
    tu = math.gcd(4, *[width for _, width in order])
    lay["tiles"] = {"tn": tu * HEAD_DIM,
                    "src": tuple((lay[name] // tu, ref_off[name] // tu) for name, _ in order)}
    scale = HEAD_DIM ** -0.5
    lay["kinds"] = tuple(
        [(None, scale * LOG2E)] * aq + [("q", scale * LOG2E)] * cq + [(None, 1.0)] * akv
        + [("k", 1.0)] * ckv + [(None, 1.0)] * bh + [(None, scale)] * bh)
    return lay


def _rope_tables(n_lat, ctx_len):
    rows = n_lat // GRID_W
    row = jnp.repeat(jnp.arange(rows, dtype=F32), GRID_W)
    col = jnp.tile(jnp.arange(GRID_W, dtype=F32), rows)
    axis_dim = HEAD_DIM // 2
    inv = ROPE_BASE ** (-jnp.arange(0, axis_dim, 2, dtype=F32) / axis_dim)
    ang_r = row[:, None] * inv
    ang_c = col[:, None] * inv
    ang = jnp.concatenate([ang_r, ang_r, ang_c, ang_c], axis=-1)
    sign = jnp.where((jnp.arange(HEAD_DIM) // 32) % 2 == 1, 1.0, -1.0).astype(F32)
    cos = jnp.concatenate([jnp.ones((ctx_len, HEAD_DIM), F32), jnp.cos(ang)], axis=0)
    sin_s = jnp.concatenate([jnp.zeros((ctx_len, HEAD_DIM), F32), jnp.sin(ang) * sign], axis=0)
    return cos, sin_s


def _decay_tables(ret_decay_l):
    log_g = jax.nn.log_sigmoid(ret_decay_l.astype(F32))
    nh = log_g.shape[1]
    pos = jnp.arange(RET_CHUNK, dtype=F32)
    lane = lambda w: jnp.repeat(w, HEAD_DIM, axis=-1)
    wk_f = jnp.exp((RET_CHUNK - 1 - pos)[:, None] * log_g[0][None])
    wk_b = jnp.exp(pos[:, None] * log_g[1][None])
    wq_f = jnp.exp((pos + 1.0)[:, None] * log_g[0][None])
    wq_b = jnp.exp((RET_CHUNK - pos)[:, None] * log_g[1][None])
    wk = jnp.stack([lane(wk_f), lane(wk_b)])
    wq = jnp.stack([lane(wq_f), lane(wq_b)])
    gc = jnp.exp(RET_CHUNK * log_g)
    diff = pos[:, None] - pos[None, :]
    ef = jnp.where((diff >= 0)[None], diff[None] * log_g[0][:, None, None], -jnp.inf)
    eb = jnp.where((diff <= 0)[None], -diff[None] * log_g[1][:, None, None], -jnp.inf)
    dmat = jnp.exp(ef) + jnp.exp(eb)
    del nh
    return wk, wq, gc, dmat


def kernel(x, c, ctx, c_ctx, w_ada, b_ada, g_pre, g_post, w_in, w_out, sink_a, qnorm_c, knorm_c, ret_decay):
    bsz, n_lat, d = x.shape
    ctx_len = ctx.shape[1]
    depth = w_ada.shape[0]
    lay = _layout(d)

    cond_rows = jnp.zeros((8, d), F32).at[0:bsz].set(c).at[bsz].set(c_ctx)
    mods = _ada(cond_rows, w_ada, b_ada)
    cos, sin_s = _rope_tables(n_lat, ctx_len)
    w_in_p = w_in.astype(BF16)
    w_out_p = w_out.astype(BF16)

    xs = jnp.concatenate([ctx, x], axis=1)
    for l in range(depth):
        mod_lat = mods[l, 0:bsz].reshape(bsz, 1, 3 * d)
        mod_ctx = mods[l, bsz:bsz + 1]
        h = _prenorm(xs, g_pre[l].reshape(1, d), mod_lat, mod_ctx, ctx_len=ctx_len)
        p = _inproj(h, w_in_p, l, lay["tiles"])
        p = _prep(p, cos, sin_s, qnorm_c[l], knorm_c[l], kinds=lay["kinds"])
        o = _mixa(sink_a[l] * LOG2E, p, lay=lay, ctx_len=ctx_len)
        o = _mixc(p, o, lay=lay, ctx_len=ctx_len)
        wk, wq, gc, dmat = _decay_tables(ret_decay[l])
        sprev_f, sprev_b = _ret_states(gc, p, wk, lay=lay)
        o = _ret_out(p, sprev_f, sprev_b, dmat, wq, o, lay=lay)
        xs = _outproj(o, w_out_p, l, xs, mod_lat[:, :, 2 * d:], mod_ctx[:, 2 * d:],
                      g_post[l].reshape(1, d), ctx_len=ctx_len, segs=lay["o_segs"],
                      latent_only=(l == depth - 1))
    return xs
```

```python
import functools
import math

import jax
import jax.numpy as jnp
from jax import lax
from jax.experimental import pallas as pl
from jax.experimental.pallas import tpu as pltpu

HEAD_DIM = 128
GRID_W = 64
WINDOW = 128
RET_CHUNK = 128
ROPE_BASE = 10000.0
EPS = 1e-6
LOG2E = math.log2(math.e)
GROUP = 3
VMEM_LIMIT = 56 * 1024 * 1024
OUTPROJ_VMEM_LIMIT = 60 * 1024 * 1024
INPROJ_TILES_PER_STEP = 3

F32 = jnp.float32
BF16 = jnp.bfloat16
NT_DIMS = (((1,), (1,)), ((), ()))
TN_DIMS = (((0,), (0,)), ((), ()))


def _params(sem):
    return pltpu.CompilerParams(dimension_semantics=sem, vmem_limit_bytes=VMEM_LIMIT)


def _silu(z):
    return z / (1.0 + jnp.exp(-z))


def _ada_kernel(c_ref, w_ref, b_ref, o_ref):
    cond = _silu(c_ref[...])
    o_ref[...] = jnp.dot(cond.astype(BF16), w_ref[...].astype(BF16),
                         preferred_element_type=F32) + b_ref[...]


def _ada(cond_rows, w_ada, b_ada):
    depth, d, n3 = w_ada.shape
    tn = 512
    return pl.pallas_call(
        _ada_kernel,
        grid=(depth, n3 // tn),
        in_specs=[
            pl.BlockSpec((8, d), lambda l, j: (0, 0)),
            pl.BlockSpec((None, d, tn), lambda l, j: (l, 0, j)),
            pl.BlockSpec((None, 1, tn), lambda l, j: (l, 0, j)),
        ],
        out_specs=pl.BlockSpec((None, 8, tn), lambda l, j: (l, 0, j)),
        out_shape=jax.ShapeDtypeStruct((depth, 8, n3), F32),
        compiler_params=_params(("arbitrary", "arbitrary")),
        name="ada",
    )(cond_rows, w_ada, b_ada.reshape(depth, 1, n3))


def _prenorm_kernel(x_ref, g_ref, ml_ref, mc_ref, h_ref, *, tm, rc, ctx_len, d):
    i = pl.program_id(1)
    g = g_ref[...]
    sh_l = ml_ref[:, 0:d]
    sc_l = 1.0 + ml_ref[:, d:2 * d]
    sh_c = mc_ref[:, 0:d]
    sc_c = 1.0 + mc_ref[:, d:2 * d]

    def body(c, carry):
        r0 = pl.multiple_of(c * rc, rc)
        is_ctx = jnp.logical_and(i == 0, r0 < ctx_len)
        sh = jnp.where(is_ctx, sh_c, sh_l)
        sc = jnp.where(is_ctx, sc_c, sc_l)
        x = x_ref[pl.ds(r0, rc), :]
        ms = jnp.mean(x * x, axis=-1, keepdims=True)
        y = x * lax.rsqrt(ms + EPS) * g
        h_ref[pl.ds(r0, rc), :] = (y * sc + sh).astype(BF16)
        return carry

    lax.fori_loop(0, tm // rc, body, 0, unroll=2)


def _prenorm(xs, g_pre, mod_lat, mod_ctx, *, ctx_len):
    b, t, d = xs.shape
    tm, rc = 768, 32
    assert t % tm == 0 and ctx_len % rc == 0 and ctx_len <= tm
    return pl.pallas_call(
        functools.partial(_prenorm_kernel, tm=tm, rc=rc, ctx_len=ctx_len, d=d),
        grid=(b, t // tm),
        in_specs=[
            pl.BlockSpec((None, tm, d), lambda bb, i: (bb, i, 0)),
            pl.BlockSpec((1, d), lambda bb, i: (0, 0)),
            pl.BlockSpec((None, 1, 3 * d), lambda bb, i: (bb, 0, 0)),
            pl.BlockSpec((1, 3 * d), lambda bb, i: (0, 0)),
        ],
        out_specs=pl.BlockSpec((None, tm, d), lambda bb, i: (bb, i, 0)),
        out_shape=jax.ShapeDtypeStruct((b, t, d), BF16),
        compiler_params=_params(("arbitrary", "arbitrary")),
        name="prenorm",
    )(xs, g_pre, mod_lat, mod_ctx)


def _prenorm_first_kernel(ctx_ref, x_ref, g_ref, ml_ref, mc_ref, xs_ref, h_ref, *, tm, rc, d):
    is_ctx = pl.program_id(1) == 0
    g = g_ref[...]
    sh = jnp.where(is_ctx, mc_ref[:, 0:d], ml_ref[:, 0:d])
    sc = 1.0 + jnp.where(is_ctx, mc_ref[:, d:2 * d], ml_ref[:, d:2 * d])

    def body(c, carry):
        r0 = pl.multiple_of(c * rc, rc)
        x = jnp.where(is_ctx, ctx_ref[pl.ds(r0, rc), :], x_ref[pl.ds(r0, rc), :])
        xs_ref[pl.ds(r0, rc), :] = x
        ms = jnp.mean(x * x, axis=-1, keepdims=True)
        y = x * lax.rsqrt(ms + EPS) * g
        h_ref[pl.ds(r0, rc), :] = (y * sc + sh).astype(BF16)
        return carry

    lax.fori_loop(0, tm // rc, body, 0, unroll=2)


def _prenorm_first(ctx, x, g_pre, mod_lat, mod_ctx):
    b, n_lat, d = x.shape
    tm, rc = ctx.shape[1], 32
    t = tm + n_lat
    assert n_lat % tm == 0 and tm % rc == 0
    return pl.pallas_call(
        functools.partial(_prenorm_first_kernel, tm=tm, rc=rc, d=d),
        grid=(b, t // tm),
        in_specs=[
            pl.BlockSpec((None, tm, d), lambda bb, i: (bb, 0, 0)),
            pl.BlockSpec((None, tm, d), lambda bb, i: (bb, jnp.maximum(i - 1, 0), 0)),
            pl.BlockSpec((1, d), lambda bb, i: (0, 0)),
            pl.BlockSpec((None, 1, 3 * d), lambda bb, i: (bb, 0, 0)),
            pl.BlockSpec((1, 3 * d), lambda bb, i: (0, 0)),
        ],
        out_specs=[pl.BlockSpec((None, tm, d), lambda bb, i: (bb, i, 0)),
                   pl.BlockSpec((None, tm, d), lambda bb, i: (bb, i, 0))],
        out_shape=[jax.ShapeDtypeStruct((b, t, d), x.dtype), jax.ShapeDtypeStruct((b, t, d), BF16)],
        compiler_params=_params(("arbitrary", "arbitrary")),
        name="prenorm_first",
    )(ctx, x, g_pre, mod_lat, mod_ctx)


def _inproj_kernel(h_ref, *refs, tn):
    w_refs, o_ref = refs[:-1], refs[-1]
    for u, w_ref in enumerate(w_refs):
        o_ref[:, u * tn:(u + 1) * tn] = jnp.dot(
            h_ref[...], w_ref[...], preferred_element_type=F32).astype(o_ref.dtype)


def _inproj(h, w, layer, tiles):
    b, t, d = h.shape
    n = w.shape[-1]
    tm = 1408 if t % 1408 == 0 else 768
    tn = tiles["tn"]
    src = tiles["src"]
    nw = INPROJ_TILES_PER_STEP
    assert t % tm == 0 and n % (nw * tn) == 0

    def src_tile(j):
        out = j - src[0][0] + src[0][1]
        for start, ref_start in src[1:]:
            out = jnp.where(j >= start, j - start + ref_start, out)
        return out

    def w_spec(u):
        return pl.BlockSpec((None, d, tn), lambda bb, i, j: (layer, 0, src_tile(nw * j + u)))

    return pl.pallas_call(
        functools.partial(_inproj_kernel, tn=tn),
        grid=(b, t // tm, n // (nw * tn)),
        in_specs=[pl.BlockSpec((None, tm, d), lambda bb, i, j: (bb, i, 0))] + [w_spec(u) for u in range(nw)],
        out_specs=pl.BlockSpec((None, tm, nw * tn), lambda bb, i, j: (bb, i, j)),
        out_shape=jax.ShapeDtypeStruct((b, t, n), BF16),
        compiler_params=pltpu.CompilerParams(dimension_semantics=("arbitrary", "arbitrary", "arbitrary"),
                                             vmem_limit_bytes=OUTPROJ_VMEM_LIMIT),
        name="inproj",
    )(h, *([w] * nw))


def _prep_kernel(p_ref, cos_ref, sin_ref, qn_ref, kn_ref, swap_ref, o_ref, *, kinds):
    cos = cos_ref[...]
    sin = sin_ref[...]
    first = {None: cos, "q": cos * qn_ref[0:1, :], "k": cos * kn_ref[0:1, :]}
    second = {None: sin, "q": sin * qn_ref[1:2, :], "k": sin * kn_ref[1:2, :]}
    swap = swap_ref[...]
    for t2 in range(len(kinds) // 2):
        pair = p_ref[:, 2 * t2 * HEAD_DIM:(2 * t2 + 2) * HEAD_DIM]
        swapped = jnp.dot(pair, swap, preferred_element_type=F32)
        for h in range(2):
            norm, scale = kinds[2 * t2 + h]
            hs = slice(h * HEAD_DIM, (h + 1) * HEAD_DIM)
            x = pair[:, hs].astype(F32)
            y = x * first[norm] + swapped[:, hs] * second[norm]
            if norm is not None:
                y = y * lax.rsqrt(jnp.mean(x * x, axis=-1, keepdims=True) + EPS)
            if scale != 1.0:
                y = y * scale
            t = 2 * t2 + h
            o_ref[:, t * HEAD_DIM:(t + 1) * HEAD_DIM] = y.astype(o_ref.dtype)


def _prep(p, cos, sin_s, qn, kn, *, kinds):
    b, t, n = p.shape
    tp = 256
    npre = len(kinds) * HEAD_DIM
    assert t % tp == 0 and len(kinds) % 2 == 0
    lanes = jnp.arange(2 * HEAD_DIM)
    swap = (lanes[:, None] == (lanes[None, :] ^ 32)).astype(BF16)
    with_swapped = lambda g: jnp.stack([g, g[jnp.arange(HEAD_DIM) ^ 32]])
    return pl.pallas_call(
        functools.partial(_prep_kernel, kinds=kinds),
        grid=(b, t // tp),
        in_specs=[
            pl.BlockSpec((None, tp, npre), lambda bb, i: (bb, i, 0)),
            pl.BlockSpec((tp, HEAD_DIM), lambda bb, i: (i, 0)),
            pl.BlockSpec((tp, HEAD_DIM), lambda bb, i: (i, 0)),
            pl.BlockSpec((2, HEAD_DIM), lambda bb, i: (0, 0)),
            pl.BlockSpec((2, HEAD_DIM), lambda bb, i: (0, 0)),
            pl.BlockSpec((2 * HEAD_DIM, 2 * HEAD_DIM), lambda bb, i: (0, 0)),
        ],
        out_specs=pl.BlockSpec((None, tp, npre), lambda bb, i: (bb, i, 0)),
        out_shape=jax.ShapeDtypeStruct(p.shape, p.dtype),
        input_output_aliases={0: 0},
        compiler_params=_params(("arbitrary", "arbitrary")),
        name="prep",
    )(p, cos, sin_s, with_swapped(qn), with_swapped(kn), swap)


VT_ROWS = HEAD_DIM + 16
SOFTMAX_ROWS = 128
STEPS_PER_TRIP = 2


def _ones_row_block(width):
    first = lax.broadcasted_iota(jnp.int32, (VT_ROWS - HEAD_DIM, width), 0) == 0
    return jnp.where(first, 1.0, 0.0).astype(BF16)


def _mixa_kernel(sink_ref, q_ref, z_ref, k_ref, v_ref, o_ref, vt_ref, s_ref, p_ref, m_ref, e_ref,
                 *, tq, nsub, ctx_len, s_lat):
    kv = pl.program_id(1)
    blk = pl.program_id(2)
    wl = tq + 2 * WINDOW
    nkeys = ctx_len + wl

    @pl.when(blk == 0)
    def _():
        def tbody(c, carry):
            r0 = pl.multiple_of(c * HEAD_DIM, HEAD_DIM)
            vt_ref[c, 0:HEAD_DIM, :] = v_ref[pl.ds(r0, HEAD_DIM), :].astype(F32).T.astype(BF16)
            vt_ref[c, HEAD_DIM:VT_ROWS, :] = _ones_row_block(HEAD_DIM)
            return carry

        lax.fori_loop(0, (ctx_len + s_lat) // HEAD_DIM, tbody, 0)

    def geometry(u):
        q0 = (blk * nsub + u) * tq - ctx_len
        cs = pl.multiple_of(jnp.clip(q0 - WINDOW, 0, s_lat - wl), HEAD_DIM)
        return q0, cs

    def q_rows(u):
        return pl.ds(pl.multiple_of(u * tq, tq), tq)

    def head_cols(g):
        return slice(g * HEAD_DIM, (g + 1) * HEAD_DIM)

    def scores(u, buf):
        q0, cs = geometry(u)
        keys = jnp.concatenate([k_ref[0:ctx_len, :], k_ref[pl.ds(ctx_len + cs, wl), :]], axis=0)
        ss = [lax.dot_general(keys, q_ref[q_rows(u), head_cols(g)], NT_DIMS, preferred_element_type=F32)
              for g in range(GROUP)]
        ms = []
        for g in range(GROUP):
            s_ref[buf, g, 0:ctx_len, :] = ss[g][0:ctx_len]
            ms.append(jnp.maximum(jnp.max(ss[g][0:ctx_len], axis=0, keepdims=True), sink_ref[kv * GROUP + g]))
        row = lax.broadcasted_iota(jnp.int32, (SOFTMAX_ROWS, tq), 0)
        col = lax.broadcasted_iota(jnp.int32, (SOFTMAX_ROWS, tq), 1)
        diff = row - col
        for r in range(0, wl, SOFTMAX_ROWS):
            rel = diff + (cs + r - q0)
            valid = jnp.logical_and(jnp.abs(rel) <= WINDOW, q0 >= 0)
            for g in range(GROUP):
                piece = jnp.where(valid, ss[g][ctx_len + r:ctx_len + r + SOFTMAX_ROWS], -jnp.inf)
                s_ref[buf, g, ctx_len + r:ctx_len + r + SOFTMAX_ROWS, :] = piece
                ms[g] = jnp.maximum(ms[g], jnp.max(piece, axis=0, keepdims=True))
        for g in range(GROUP):
            m_ref[buf, g] = ms[g]

    def softmax(buf):
        for g in range(GROUP):
            m = m_ref[buf, g]
            e_ref[buf, g] = jnp.exp2(sink_ref[kv * GROUP + g] - m)
            for r in range(0, nkeys, SOFTMAX_ROWS):
                p_ref[buf, g, r:r + SOFTMAX_ROWS, :] = jnp.exp2(
                    s_ref[buf, g, r:r + SOFTMAX_ROWS, :] - m).astype(BF16)

    def output(u, buf):
        _, cs = geometry(u)
        c0 = (ctx_len + cs) // HEAD_DIM
        tiles = [vt_ref[c] for c in range(ctx_len // HEAD_DIM)] + [vt_ref[c0 + c] for c in range(wl // HEAD_DIM)]
        vt = jnp.concatenate(tiles, axis=1)
        for g in range(GROUP):
            pv = jnp.dot(vt, p_ref[buf, g], preferred_element_type=F32)
            den = pv[HEAD_DIM:HEAD_DIM + 1] + e_ref[buf, g]
            o = (pv[0:HEAD_DIM] / den).T
            z = z_ref[q_rows(u), head_cols(g)].astype(F32)
            o_ref[q_rows(u), head_cols(g)] = (o * _silu(z)).astype(o_ref.dtype)

    def step(u, par):
        softmax(par)
        scores(u + 1, 1 - par)
        output(u - 1, 1 - par)

    scores(0, 0)
    scores(1, 1)
    softmax(0)
    step(1, 1)
    trips = (nsub - 3) // STEPS_PER_TRIP

    def body(j, carry):
        u = 2 + STEPS_PER_TRIP * j
        for k in range(STEPS_PER_TRIP):
            step(u + k, k % 2)
        return carry

    lax.fori_loop(0, trips, body, 0)
    for u in range(2 + trips * STEPS_PER_TRIP, nsub - 1):
        step(u, u % 2)
    last = nsub - 1
    softmax(last % 2)
    output(last - 1, (last - 1) % 2)
    output(last, last % 2)


def _mixa(sink2, p, *, lay, ctx_len):
    b, t, _ = p.shape
    tq = 256
    nsub = 11 if t % (11 * tq) == 0 else 3
    gw = GROUP * HEAD_DIM
    nkv = lay["akv"]
    s_lat = t - ctx_len
    wl = tq + 2 * WINDOW
    tb = tq * nsub
    assert t % tb == 0 and ctx_len % tq == 0 and s_lat >= wl and WINDOW == HEAD_DIM and nsub >= 3
    qoff, zoff, koff, voff = lay["qa"] // GROUP, lay["za"] // GROUP, lay["ka"], lay["va"]
    return pl.pallas_call(
        functools.partial(_mixa_kernel, tq=tq, nsub=nsub, ctx_len=ctx_len, s_lat=s_lat),
        grid=(b, nkv, t // tb),
        in_specs=[
            pl.BlockSpec(memory_space=pltpu.SMEM),
            pl.BlockSpec((None, tb, gw), lambda bb, k, i: (bb, i, qoff + k)),
            pl.BlockSpec((None, tb, gw), lambda bb, k, i: (bb, i, zoff + k)),
            pl.BlockSpec((None, t, HEAD_DIM), lambda bb, k, i: (bb, 0, koff + k)),
            pl.BlockSpec((None, t, HEAD_DIM), lambda bb, k, i: (bb, 0, voff + k)),
        ],
        out_specs=pl.BlockSpec((None, tb, gw), lambda bb, k, i: (bb, i, k)),
        out_shape=jax.ShapeDtypeStruct((b, t, lay["nh"] * HEAD_DIM), BF16),
        scratch_shapes=[pltpu.VMEM((t // HEAD_DIM, VT_ROWS, HEAD_DIM), BF16),
                        pltpu.VMEM((2, GROUP, ctx_len + wl, tq), F32),
                        pltpu.VMEM((2, GROUP, ctx_len + wl, tq), BF16),
                        pltpu.VMEM((2, GROUP, 1, tq), F32), pltpu.VMEM((2, GROUP, 1, tq), F32)],
        compiler_params=_params(("arbitrary", "arbitrary", "arbitrary")),
        name="mixa",
    )(sink2, p, p, p, p)


def _mixc_kernel(q_ref, z_ref, k_ref, v_ref, oin_ref, o_ref,
                 vt_ref, qt_ref, s_ref, p_ref, a_ref, mc_ref, m_ref, acc_ref, *, tq, tk, ctx_len, nchunk, kvs):
    del oin_ref
    qi = pl.program_id(2)
    units = [(kvi, kvi * GROUP + g) for kvi in range(kvs) for g in range(GROUP)]

    def kv_cols(kvi):
        return slice(kvi * HEAD_DIM, (kvi + 1) * HEAD_DIM)

    @pl.when(qi == 0)
    def _():
        def tbody(c, carry):
            r0 = pl.multiple_of(c * tk, tk)
            for kvi in range(kvs):
                vt_ref[kvi, c, 0:HEAD_DIM, :] = v_ref[pl.ds(r0, tk), kv_cols(kvi)].astype(F32).T.astype(BF16)
                vt_ref[kvi, c, HEAD_DIM:VT_ROWS, :] = _ones_row_block(tk)
            return carry

        lax.fori_loop(0, nchunk, tbody, 0)

    for _, u in units:
        qt_ref[u] = q_ref[:, u * HEAD_DIM:(u + 1) * HEAD_DIM].astype(F32).T.astype(BF16)

    def scores(c, buf, units=units):
        r0 = pl.multiple_of(c * tk, tk)
        for kvi, u in units:
            s = jnp.dot(k_ref[pl.ds(r0, tk), kv_cols(kvi)], qt_ref[u], preferred_element_type=F32)
            s_ref[buf, u] = s
            mc_ref[buf, u] = jnp.max(s, axis=0, keepdims=True)

    def softmax(buf, first):
        for _, u in units:
            m_cur = mc_ref[buf, u]
            if first:
                m_new = m_cur
            else:
                m_prev = m_ref[u]
                m_new = jnp.maximum(m_prev, m_cur)
                a_ref[buf, u] = jnp.exp2(m_prev - m_new)
            m_ref[u] = m_new
            for r in range(0, tk, SOFTMAX_ROWS):
                p_ref[buf, u, r:r + SOFTMAX_ROWS, :] = jnp.exp2(
                    s_ref[buf, u, r:r + SOFTMAX_ROWS, :] - m_new).astype(BF16)

    def output(c, buf, first, units=units):
        for kvi, u in units:
            pv = jnp.dot(vt_ref[kvi, c], p_ref[buf, u], preferred_element_type=F32)
            acc_ref[u] = pv if first else a_ref[buf, u] * acc_ref[u] + pv

    nctx = ctx_len // tq

    @pl.when(qi < nctx)
    def _():
        for kvi, u in units:
            s = jnp.dot(k_ref[0:ctx_len, kv_cols(kvi)], qt_ref[u], preferred_element_type=F32)
            p = jnp.exp2(s - jnp.max(s, axis=0, keepdims=True))
            acc_ref[u] = jnp.dot(vt_ref[kvi, 0, :, 0:ctx_len], p.astype(BF16), preferred_element_type=F32)

    @pl.when(qi >= nctx)
    def _():
        def step(i, par, first=False):
            softmax(par, False)
            for lo in range(0, len(units), GROUP):
                scores(i + 1, 1 - par, units[lo:lo + GROUP])
                output(i - 1, 1 - par, first, units[lo:lo + GROUP])

        scores(0, 0)
        scores(1, 1)
        softmax(0, True)
        mid = nchunk - 2
        if mid >= 1:
            step(1, 1, first=True)
        rest = max(mid - 1, 0)
        trips = rest // STEPS_PER_TRIP

        def body(j, carry):
            i = 2 + STEPS_PER_TRIP * j
            for k in range(STEPS_PER_TRIP):
                step(i + k, k % 2)
            return carry

        lax.fori_loop(0, trips, body, 0)
        for i in range(2 + trips * STEPS_PER_TRIP, nchunk - 1):
            step(i, i % 2)
        last = nchunk - 1
        softmax(last % 2, False)
        output(last - 1, (last - 1) % 2, last == 1)
        output(last, last % 2, False)

    o = jnp.concatenate([(acc_ref[u, 0:HEAD_DIM, :] / acc_ref[u, HEAD_DIM:HEAD_DIM + 1, :]).T
                         for _, u in units], axis=1)
    o_ref[...] = (o * _silu(z_ref[...].astype(F32))).astype(o_ref.dtype)


def _mixc(p, o_prev, *, lay, ctx_len):
    b, t, _ = p.shape
    tq, tk = 256, 768
    nkv = lay["ckv"]
    kvs = 2 if nkv % 2 == 0 else 1
    nu = kvs * GROUP
    gw = nu * HEAD_DIM
    kw = kvs * HEAD_DIM
    assert t % tq == 0 and ctx_len % tq == 0 and t % tk == 0 and ctx_len <= tk
    assert all(lay[name] % nu == 0 for name in ("qc", "zc", "oc")) and lay["kc"] % kvs == 0 and lay["vc"] % kvs == 0
    qoff, zoff, ooff = lay["qc"] // nu, lay["zc"] // nu, lay["oc"] // nu
    koff, voff = lay["kc"] // kvs, lay["vc"] // kvs
    return pl.pallas_call(
        functools.partial(_mixc_kernel, tq=tq, tk=tk, ctx_len=ctx_len, nchunk=t // tk, kvs=kvs),
        grid=(b, nkv // kvs, t // tq),
        in_specs=[
            pl.BlockSpec((None, tq, gw), lambda bb, k, i: (bb, i, qoff + k)),
            pl.BlockSpec((None, tq, gw), lambda bb, k, i: (bb, i, zoff + k)),
            pl.BlockSpec((None, t, kw), lambda bb, k, i: (bb, 0, koff + k)),
            pl.BlockSpec((None, t, kw), lambda bb, k, i: (bb, 0, voff + k)),
            pl.BlockSpec(memory_space=pl.ANY),
        ],
        out_specs=pl.BlockSpec((None, tq, gw), lambda bb, k, i: (bb, i, ooff + k)),
        out_shape=jax.ShapeDtypeStruct(o_prev.shape, o_prev.dtype),
        input_output_aliases={4: 0},
        scratch_shapes=[pltpu.VMEM((kvs, t // tk, VT_ROWS, tk), BF16), pltpu.VMEM((nu, HEAD_DIM, tq), BF16),
                        pltpu.VMEM((2, nu, tk, tq), F32), pltpu.VMEM((2, nu, tk, tq), BF16),
                        pltpu.VMEM((2, nu, 1, tq), F32), pltpu.VMEM((2, nu, 1, tq), F32),
                        pltpu.VMEM((nu, 1, tq), F32),
                        pltpu.VMEM((nu, VT_ROWS, tq), F32)],
        compiler_params=_params(("arbitrary", "arbitrary", "arbitrary")),
        name="mixc",
    )(p, p, p, p, o_prev)


def _ret_state_kernel(gc_ref, kf_ref, vf_ref, kb_ref, vb_ref, wk_ref, of_ref, ob_ref, sf_ref, sb_ref, *, nh):
    step = pl.program_id(1)

    @pl.when(step == 0)
    def _():
        sf_ref[...] = jnp.zeros(sf_ref.shape, F32)
        sb_ref[...] = jnp.zeros(sb_ref.shape, F32)

    heads = [slice(h * HEAD_DIM, (h + 1) * HEAD_DIM) for h in range(nh)]
    for d, (k_ref, v_ref, o_ref, s_ref, order) in enumerate(
            ((kf_ref, vf_ref, of_ref, sf_ref, (0, 1)), (kb_ref, vb_ref, ob_ref, sb_ref, (1, 0)))):
        wk = wk_ref[d]
        for ci in order:
            rs = slice(ci * RET_CHUNK, (ci + 1) * RET_CHUNK)
            k = (k_ref[rs, :].astype(F32) * wk).astype(BF16)
            v = v_ref[rs, :]
            o_ref[ci] = s_ref[...].astype(o_ref.dtype)
            kvs = [lax.dot_general(k[:, hs], v[:, hs], TN_DIMS, preferred_element_type=F32) for hs in heads]
            for h, hs in enumerate(heads):
                s_ref[hs, :] = gc_ref[d, h] * s_ref[hs, :] + kvs[h]


def _ret_states(gc, p, wk, *, lay):
    b, t, _ = p.shape
    nh = lay["bh"]
    bw = nh * HEAD_DIM
    tb = 2 * RET_CHUNK
    nblk = t // tb
    koff, voff = lay["kb"] // nh, lay["vb"] // nh
    assert t % tb == 0

    def bwd_blk(s):
        return jnp.where(s == 0, 0, nblk - s)

    state = jax.ShapeDtypeStruct((b, t // RET_CHUNK, bw, HEAD_DIM), BF16)
    return pl.pallas_call(
        functools.partial(_ret_state_kernel, nh=nh),
        grid=(b, nblk),
        in_specs=[
            pl.BlockSpec(memory_space=pltpu.SMEM),
            pl.BlockSpec((None, tb, bw), lambda bb, s: (bb, s, koff)),
            pl.BlockSpec((None, tb, bw), lambda bb, s: (bb, s, voff)),
            pl.BlockSpec((None, tb, bw), lambda bb, s: (bb, bwd_blk(s), koff)),
            pl.BlockSpec((None, tb, bw), lambda bb, s: (bb, bwd_blk(s), voff)),
            pl.BlockSpec((2, RET_CHUNK, bw), lambda bb, s: (0, 0, 0)),
        ],
        out_specs=[pl.BlockSpec((None, 2, bw, HEAD_DIM), lambda bb, s: (bb, s, 0, 0)),
                   pl.BlockSpec((None, 2, bw, HEAD_DIM), lambda bb, s: (bb, bwd_blk(s), 0, 0))],
        out_shape=[state, state],
        scratch_shapes=[pltpu.VMEM((bw, HEAD_DIM), F32), pltpu.VMEM((bw, HEAD_DIM), F32)],
        compiler_params=_params(("arbitrary", "arbitrary")),
        name="ret_states",
    )(gc, p, p, p, p, wk)


def _ret_out_kernel(q_ref, k_ref, v_ref, z_ref, spf_ref, spb_ref, dm_ref, wq_ref, oin_ref, o_ref, *, nh, nc):
    del oin_ref
    heads = [slice(h * HEAD_DIM, (h + 1) * HEAD_DIM) for h in range(nh)]
    for c in range(nc):
        rs = slice(c * RET_CHUNK, (c + 1) * RET_CHUNK)
        qs = [q_ref[rs, hs] for hs in heads]
        intra = [lax.dot_general(qs[h], k_ref[rs, heads[h]], NT_DIMS, preferred_element_type=F32) * dm_ref[h]
                 for h in range(nh)]
        outs = []
        for h, hs in enumerate(heads):
            q32 = qs[h].astype(F32)
            lhs = jnp.concatenate([intra[h].astype(BF16),
                                   (q32 * wq_ref[0, :, hs]).astype(BF16),
                                   (q32 * wq_ref[1, :, hs]).astype(BF16)], axis=1)
            rhs = jnp.concatenate([v_ref[rs, hs], spf_ref[c, hs, :], spb_ref[c, hs, :]], axis=0)
            outs.append(jnp.dot(lhs, rhs, preferred_element_type=F32))
        for h, hs in enumerate(heads):
            o = outs[h]
            mu = jnp.mean(o, axis=-1, keepdims=True)
            oc = o - mu
            var = jnp.mean(oc * oc, axis=-1, keepdims=True)
            y = oc * lax.rsqrt(var + EPS)
            o_ref[rs, hs] = (y * _silu(z_ref[rs, hs].astype(F32))).astype(o_ref.dtype)


def _ret_out(p, sprev_f, sprev_b, dmat, wq, o_prev, *, lay):
    b, t, _ = p.shape
    nh = lay["bh"]
    bw = nh * HEAD_DIM
    nc = 6 if t % (6 * RET_CHUNK) == 0 else 2
    tb = nc * RET_CHUNK
    assert t % tb == 0
    qoff, koff, voff, zoff = lay["qb"] // nh, lay["kb"] // nh, lay["vb"] // nh, lay["zb"] // nh
    ooff = lay["ob"] // nh
    return pl.pallas_call(
        functools.partial(_ret_out_kernel, nh=nh, nc=nc),
        grid=(b, t // tb),
        in_specs=[
            pl.BlockSpec((None, tb, bw), lambda bb, i: (bb, i, qoff)),
            pl.BlockSpec((None, tb, bw), lambda bb, i: (bb, i, koff)),
            pl.BlockSpec((None, tb, bw), lambda bb, i: (bb, i, voff)),
            pl.BlockSpec((None, tb, bw), lambda bb, i: (bb, i, zoff)),
            pl.BlockSpec((None, nc, bw, HEAD_DIM), lambda bb, i: (bb, i, 0, 0)),
            pl.BlockSpec((None, nc, bw, HEAD_DIM), lambda bb, i: (bb, i, 0, 0)),
            pl.BlockSpec((nh, RET_CHUNK, RET_CHUNK), lambda bb, i: (0, 0, 0)),
            pl.BlockSpec((2, RET_CHUNK, bw), lambda bb, i: (0, 0, 0)),
            pl.BlockSpec(memory_space=pl.ANY),
        ],
        out_specs=pl.BlockSpec((None, tb, bw), lambda bb, i: (bb, i, ooff)),
        out_shape=jax.ShapeDtypeStruct(o_prev.shape, o_prev.dtype),
        input_output_aliases={8: 0},
        compiler_params=_params(("arbitrary", "arbitrary")),
        name="ret_out",
    )(p, p, p, p, sprev_f, sprev_b, dmat, wq, o_prev)


def _outproj_kernel(o_ref, w_ref, x_ref, gl_ref, gc_ref, gp_ref, y_ref, *, tm, rc, first_lat, segs):
    i = pl.program_id(1)
    o = jnp.concatenate([o_ref[:, a:b] for a, b in segs], axis=1)
    y_ref[...] = jnp.dot(o, w_ref[...], preferred_element_type=F32)
    gain = jnp.where(i < first_lat, gc_ref[...], gl_ref[...]) * gp_ref[...]

    def body(c, carry):
        r0 = pl.multiple_of(c * rc, rc)
        y = y_ref[pl.ds(r0, rc), :]
        ms = jnp.mean(y * y, axis=-1, keepdims=True)
        y_ref[pl.ds(r0, rc), :] = x_ref[pl.ds(r0, rc), :] + (y * lax.rsqrt(ms + EPS)) * gain
        return carry

    lax.fori_loop(0, tm // rc, body, 0)


def _outproj(o, w, layer, xs, gate_lat, gate_ctx, g_post, *, ctx_len, segs, latent_only):
    b, t, d = xs.shape
    kdim = o.shape[-1]
    tm, rc = 256, 64
    assert t % tm == 0 and ctx_len % tm == 0 and w.shape[1] == kdim
    nctx = ctx_len // tm
    off = nctx if latent_only else 0
    first_lat = 0 if latent_only else nctx
    t_out = t - off * tm
    return pl.pallas_call(
        functools.partial(_outproj_kernel, tm=tm, rc=rc, first_lat=first_lat, segs=segs),
        grid=(b, t_out // tm),
        in_specs=[
            pl.BlockSpec((None, tm, kdim), lambda bb, i: (bb, i + off, 0)),
            pl.BlockSpec((None, kdim, d), lambda bb, i: (layer, 0, 0), pipeline_mode=pl.Buffered(1)),
            pl.BlockSpec((None, tm, d), lambda bb, i: (bb, i + off, 0)),
            pl.BlockSpec((None, 1, d), lambda bb, i: (bb, 0, 0)),
            pl.BlockSpec((1, d), lambda bb, i: (0, 0)),
            pl.BlockSpec((1, d), lambda bb, i: (0, 0)),
        ],
        out_specs=pl.BlockSpec((None, tm, d), lambda bb, i: (bb, i, 0)),
        out_shape=jax.ShapeDtypeStruct((b, t_out, d), xs.dtype),
        compiler_params=pltpu.CompilerParams(dimension_semantics=("arbitrary", "arbitrary"),
                                             vmem_limit_bytes=OUTPROJ_VMEM_LIMIT),
        name="outproj",
    )(o, w, xs, gate_lat, gate_ctx, g_post)


def _layout(d_model):
    nh = d_model // HEAD_DIM
    aq = 3 * nh // 8
    akv = aq // GROUP
    bh = nh // 4
    cq = nh - aq - bh
    ckv = cq // GROUP
    order = [("qa", aq), ("qc", cq), ("ka", akv), ("kc", ckv), ("qb", bh), ("kb", bh),
             ("za", aq), ("zc", cq), ("zb", bh), ("va", akv), ("vc", ckv), ("vb", bh)]
    lay = {"nh": nh, "aq": aq, "akv": akv, "bh": bh, "cq": cq, "ckv": ckv}
    off = 0
    for name, width in order:
        lay[name] = off
        off += width
    lay["nprep"] = lay["za"]
    lay["oa"], lay["oc"], lay["ob"] = 0, aq, aq + cq
    ref_order = [("qa", aq), ("ka", akv), ("va", akv), ("qb", bh), ("kb", bh), ("vb", bh),
                 ("qc", cq), ("kc", ckv), ("vc", ckv), ("za", aq), ("zb", bh), ("zc", cq)]
    ref_off, off = {}, 0
    for name, width in ref_order:
        ref_off[name] = off
        off += width
    lay["o_segs"] = ((0, aq * HEAD_DIM), ((aq + cq) * HEAD_DIM, nh * HEAD_DIM),
                     (aq * HEAD_DIM, (aq + cq) * HEAD_DIM))
    tu = math.gcd(4, *[width for _, width in order])
    lay["tiles"] = {"tn": tu * HEAD_DIM,
                    "src": tuple((lay[name] // tu, ref_off[name] // tu) for name, _ in order)}
    scale = HEAD_DIM ** -0.5
    lay["kinds"] = tuple(
        [(None, scale * LOG2E)] * aq + [("q", scale * LOG2E)] * cq + [(None, 1.0)] * akv
        + [("k", 1.0)] * ckv + [(None, 1.0)] * bh + [(None, scale)] * bh)
    return lay


def _rope_tables(n_lat, ctx_len):
    rows = n_lat // GRID_W
    row = jnp.repeat(jnp.arange(rows, dtype=F32), GRID_W)
    col = jnp.tile(jnp.arange(GRID_W, dtype=F32), rows)
    axis_dim = HEAD_DIM // 2
    inv = ROPE_BASE ** (-jnp.arange(0, axis_dim, 2, dtype=F32) / axis_dim)
    ang_r = row[:, None] * inv
    ang_c = col[:, None] * inv
    ang = jnp.concatenate([ang_r, ang_r, ang_c, ang_c], axis=-1)
    sign = jnp.where((jnp.arange(HEAD_DIM) // 32) % 2 == 1, 1.0, -1.0).astype(F32)
    cos = jnp.concatenate([jnp.ones((ctx_len, HEAD_DIM), F32), jnp.cos(ang)], axis=0)
    sin_s = jnp.concatenate([jnp.zeros((ctx_len, HEAD_DIM), F32), jnp.sin(ang) * sign], axis=0)
    return cos, sin_s


def _decay_tables(ret_decay_l):
    log_g = jax.nn.log_sigmoid(ret_decay_l.astype(F32))
    nh = log_g.shape[1]
    pos = jnp.arange(RET_CHUNK, dtype=F32)
    lane = lambda w: jnp.repeat(w, HEAD_DIM, axis=-1)
    wk_f = jnp.exp((RET_CHUNK - 1 - pos)[:, None] * log_g[0][None])
    wk_b = jnp.exp(pos[:, None] * log_g[1][None])
    wq_f = jnp.exp((pos + 1.0)[:, None] * log_g[0][None])
    wq_b = jnp.exp((RET_CHUNK - pos)[:, None] * log_g[1][None])
    wk = jnp.stack([lane(wk_f), lane(wk_b)])
    wq = jnp.stack([lane(wq_f), lane(wq_b)])
    gc = jnp.exp(RET_CHUNK * log_g)
    diff = pos[:, None] - pos[None, :]
    ef = jnp.where((diff >= 0)[None], diff[None] * log_g[0][:, None, None], -jnp.inf)
    eb = jnp.where((diff <= 0)[None], -diff[None] * log_g[1][:, None, None], -jnp.inf)
    dmat = jnp.exp(ef) + jnp.exp(eb)
    del nh
    return wk, wq, gc, dmat


def kernel(x, c, ctx, c_ctx, w_ada, b_ada, g_pre, g_post, w_in, w_out, sink_a, qnorm_c, knorm_c, ret_decay):
    bsz, n_lat, d = x.shape
    ctx_len = ctx.shape[1]
    depth = w_ada.shape[0]
    lay = _layout(d)

    cond_rows = jnp.zeros((8, d), F32).at[0:bsz].set(c).at[bsz].set(c_ctx)
    mods = _ada(cond_rows, w_ada, b_ada)
    cos, sin_s = _rope_tables(n_lat, ctx_len)
    w_in_p = w_in.astype(BF16)
    w_out_p = w_out.astype(BF16)

    xs = None
    for l in range(depth):
        mod_lat = mods[l, 0:bsz].reshape(bsz, 1, 3 * d)
        mod_ctx = mods[l, bsz:bsz + 1]
        if l == 0:
            xs, h = _prenorm_first(ctx, x, g_pre[l].reshape(1, d), mod_lat, mod_ctx)
        else:
            h = _prenorm(xs, g_pre[l].reshape(1, d), mod_lat, mod_ctx, ctx_len=ctx_len)
        p = _inproj(h, w_in_p, l, lay["tiles"])
        p = _prep(p, cos, sin_s, qnorm_c[l], knorm_c[l], kinds=lay["kinds"])
        o = _mixa(sink_a[l] * LOG2E, p, lay=lay, ctx_len=ctx_len)
        o = _mixc(p, o, lay=lay, ctx_len=ctx_len)
        wk, wq, gc, dmat = _decay_tables(ret_decay[l])
        sprev_f, sprev_b = _ret_states(gc, p, wk, lay=lay)
        o = _ret_out(p, sprev_f, sprev_b, dmat, wq, o, lay=lay)
        xs = _outproj(o, w_out_p, l, xs, mod_lat[:, :, 2 * d:], mod_ctx[:, 2 * d:],
                      g_post[l].reshape(1, d), ctx_len=ctx_len, segs=lay["o_segs"],
                      latent_only=(l == depth - 1))
    return xs
```

```python
import functools
import math

import jax
import jax.numpy as jnp
from jax import lax
from jax.experimental import pallas as pl
from jax.experimental.pallas import tpu as pltpu

HEAD_DIM = 128
GRID_W = 64
WINDOW = 128
RET_CHUNK = 128
ROPE_BASE = 10000.0
EPS = 1e-6
LOG2E = math.log2(math.e)
GROUP = 3
VMEM_LIMIT = 56 * 1024 * 1024
OUTPROJ_VMEM_LIMIT = 60 * 1024 * 1024
INPROJ_TILES_PER_STEP = 3

F32 = jnp.float32
BF16 = jnp.bfloat16
NT_DIMS = (((1,), (1,)), ((), ()))
TN_DIMS = (((0,), (0,)), ((), ()))


def _params(sem):
    return pltpu.CompilerParams(dimension_semantics=sem, vmem_limit_bytes=VMEM_LIMIT)


def _silu(z):
    return z / (1.0 + jnp.exp(-z))


def _ada_kernel(c_ref, w_ref, b_ref, o_ref):
    cond = _silu(c_ref[...])
    o_ref[...] = jnp.dot(cond.astype(BF16), w_ref[...].astype(BF16),
                         preferred_element_type=F32) + b_ref[...]


def _ada(cond_rows, w_ada, b_ada):
    depth, d, n3 = w_ada.shape
    tn = 512
    return pl.pallas_call(
        _ada_kernel,
        grid=(depth, n3 // tn),
        in_specs=[
            pl.BlockSpec((8, d), lambda l, j: (0, 0)),
            pl.BlockSpec((None, d, tn), lambda l, j: (l, 0, j)),
            pl.BlockSpec((None, 1, tn), lambda l, j: (l, 0, j)),
        ],
        out_specs=pl.BlockSpec((None, 8, tn), lambda l, j: (l, 0, j)),
        out_shape=jax.ShapeDtypeStruct((depth, 8, n3), F32),
        compiler_params=_params(("arbitrary", "arbitrary")),
        name="ada",
    )(cond_rows, w_ada, b_ada.reshape(depth, 1, n3))


def _prenorm_kernel(x_ref, g_ref, ml_ref, mc_ref, h_ref, *, tm, rc, ctx_len, d):
    i = pl.program_id(1)
    g = g_ref[...]
    sh_l = ml_ref[:, 0:d]
    sc_l = 1.0 + ml_ref[:, d:2 * d]
    sh_c = mc_ref[:, 0:d]
    sc_c = 1.0 + mc_ref[:, d:2 * d]

    def body(c, carry):
        r0 = pl.multiple_of(c * rc, rc)
        is_ctx = jnp.logical_and(i == 0, r0 < ctx_len)
        sh = jnp.where(is_ctx, sh_c, sh_l)
        sc = jnp.where(is_ctx, sc_c, sc_l)
        x = x_ref[pl.ds(r0, rc), :]
        ms = jnp.mean(x * x, axis=-1, keepdims=True)
        y = x * lax.rsqrt(ms + EPS) * g
        h_ref[pl.ds(r0, rc), :] = (y * sc + sh).astype(BF16)
        return carry

    lax.fori_loop(0, tm // rc, body, 0, unroll=2)


def _prenorm(xs, g_pre, mod_lat, mod_ctx, *, ctx_len):
    b, t, d = xs.shape
    tm, rc = 768, 32
    assert t % tm == 0 and ctx_len % rc == 0 and ctx_len <= tm
    return pl.pallas_call(
        functools.partial(_prenorm_kernel, tm=tm, rc=rc, ctx_len=ctx_len, d=d),
        grid=(b, t // tm),
        in_specs=[
            pl.BlockSpec((None, tm, d), lambda bb, i: (bb, i, 0)),
            pl.BlockSpec((1, d), lambda bb, i: (0, 0)),
            pl.BlockSpec((None, 1, 3 * d), lambda bb, i: (bb, 0, 0)),
            pl.BlockSpec((1, 3 * d), lambda bb, i: (0, 0)),
        ],
        out_specs=pl.BlockSpec((None, tm, d), lambda bb, i: (bb, i, 0)),
        out_shape=jax.ShapeDtypeStruct((b, t, d), BF16),
        compiler_params=_params(("arbitrary", "arbitrary")),
        name="prenorm",
    )(xs, g_pre, mod_lat, mod_ctx)


def _prenorm_first_kernel(ctx_ref, x_ref, g_ref, ml_ref, mc_ref, xs_ref, h_ref, *, tm, rc, d):
    is_ctx = pl.program_id(1) == 0
    g = g_ref[...]
    sh = jnp.where(is_ctx, mc_ref[:, 0:d], ml_ref[:, 0:d])
    sc = 1.0 + jnp.where(is_ctx, mc_ref[:, d:2 * d], ml_ref[:, d:2 * d])

    def body(c, carry):
        r0 = pl.multiple_of(c * rc, rc)
        x = jnp.where(is_ctx, ctx_ref[pl.ds(r0, rc), :], x_ref[pl.ds(r0, rc), :])
        xs_ref[pl.ds(r0, rc), :] = x
        ms = jnp.mean(x * x, axis=-1, keepdims=True)
        y = x * lax.rsqrt(ms + EPS) * g
        h_ref[pl.ds(r0, rc), :] = (y * sc + sh).astype(BF16)
        return carry

    lax.fori_loop(0, tm // rc, body, 0, unroll=2)


def _prenorm_first(ctx, x, g_pre, mod_lat, mod_ctx):
    b, n_lat, d = x.shape
    tm, rc = ctx.shape[1], 32
    t = tm + n_lat
    assert n_lat % tm == 0 and tm % rc == 0
    return pl.pallas_call(
        functools.partial(_prenorm_first_kernel, tm=tm, rc=rc, d=d),
        grid=(b, t // tm),
        in_specs=[
            pl.BlockSpec((None, tm, d), lambda bb, i: (bb, 0, 0)),
            pl.BlockSpec((None, tm, d), lambda bb, i: (bb, jnp.maximum(i - 1, 0), 0)),
            pl.BlockSpec((1, d), lambda bb, i: (0, 0)),
            pl.BlockSpec((None, 1, 3 * d), lambda bb, i: (bb, 0, 0)),
            pl.BlockSpec((1, 3 * d), lambda bb, i: (0, 0)),
        ],
        out_specs=[pl.BlockSpec((None, tm, d), lambda bb, i: (bb, i, 0)),
                   pl.BlockSpec((None, tm, d), lambda bb, i: (bb, i, 0))],
        out_shape=[jax.ShapeDtypeStruct((b, t, d), x.dtype), jax.ShapeDtypeStruct((b, t, d), BF16)],
        compiler_params=_params(("arbitrary", "arbitrary")),
        name="prenorm_first",
    )(ctx, x, g_pre, mod_lat, mod_ctx)


def _inproj_kernel(h_ref, *refs, tn):
    w_refs, o_ref = refs[:-1], refs[-1]
    for u, w_ref in enumerate(w_refs):
        o_ref[:, u * tn:(u + 1) * tn] = jnp.dot(
            h_ref[...], w_ref[...], preferred_element_type=F32).astype(o_ref.dtype)


def _inproj(h, w, layer, tiles):
    b, t, d = h.shape
    n = w.shape[-1]
    tm = 1408 if t % 1408 == 0 else 768
    tn = tiles["tn"]
    src = tiles["src"]
    nw = INPROJ_TILES_PER_STEP
    assert t % tm == 0 and n % (nw * tn) == 0

    def src_tile(j):
        out = j - src[0][0] + src[0][1]
        for start, ref_start in src[1:]:
            out = jnp.where(j >= start, j - start + ref_start, out)
        return out

    def w_spec(u):
        return pl.BlockSpec((None, d, tn), lambda bb, i, j: (layer, 0, src_tile(nw * j + u)))

    return pl.pallas_call(
        functools.partial(_inproj_kernel, tn=tn),
        grid=(b, t // tm, n // (nw * tn)),
        in_specs=[pl.BlockSpec((None, tm, d), lambda bb, i, j: (bb, i, 0))] + [w_spec(u) for u in range(nw)],
        out_specs=pl.BlockSpec((None, tm, nw * tn), lambda bb, i, j: (bb, i, j)),
        out_shape=jax.ShapeDtypeStruct((b, t, n), BF16),
        compiler_params=pltpu.CompilerParams(dimension_semantics=("arbitrary", "arbitrary", "arbitrary"),
                                             vmem_limit_bytes=OUTPROJ_VMEM_LIMIT),
        name="inproj",
    )(h, *([w] * nw))


def _prep_kernel(p_ref, cos_ref, sin_ref, qn_ref, kn_ref, swap_ref, o_ref, *, kinds):
    cos = cos_ref[...]
    sin = sin_ref[...]
    first = {None: cos, "q": cos * qn_ref[0:1, :], "k": cos * kn_ref[0:1, :]}
    second = {None: sin, "q": sin * qn_ref[1:2, :], "k": sin * kn_ref[1:2, :]}
    swap = swap_ref[...]
    for t2 in range(len(kinds) // 2):
        pair = p_ref[:, 2 * t2 * HEAD_DIM:(2 * t2 + 2) * HEAD_DIM]
        swapped = jnp.dot(pair, swap, preferred_element_type=F32)
        for h in range(2):
            norm, scale = kinds[2 * t2 + h]
            hs = slice(h * HEAD_DIM, (h + 1) * HEAD_DIM)
            x = pair[:, hs].astype(F32)
            y = x * first[norm] + swapped[:, hs] * second[norm]
            if norm is not None:
                y = y * lax.rsqrt(jnp.mean(x * x, axis=-1, keepdims=True) + EPS)
            if scale != 1.0:
                y = y * scale
            t = 2 * t2 + h
            o_ref[:, t * HEAD_DIM:(t + 1) * HEAD_DIM] = y.astype(o_ref.dtype)


def _prep(p, cos, sin_s, qn, kn, *, kinds):
    b, t, n = p.shape
    tp = 768 if t % 768 == 0 else 256
    npre = len(kinds) * HEAD_DIM
    assert t % tp == 0 and len(kinds) % 2 == 0
    lanes = jnp.arange(2 * HEAD_DIM)
    swap = (lanes[:, None] == (lanes[None, :] ^ 32)).astype(BF16)
    with_swapped = lambda g: jnp.stack([g, g[jnp.arange(HEAD_DIM) ^ 32]])
    return pl.pallas_call(
        functools.partial(_prep_kernel, kinds=kinds),
        grid=(b, t // tp),
        in_specs=[
            pl.BlockSpec((None, tp, npre), lambda bb, i: (bb, i, 0)),
            pl.BlockSpec((tp, HEAD_DIM), lambda bb, i: (i, 0)),
            pl.BlockSpec((tp, HEAD_DIM), lambda bb, i: (i, 0)),
            pl.BlockSpec((2, HEAD_DIM), lambda bb, i: (0, 0)),
            pl.BlockSpec((2, HEAD_DIM), lambda bb, i: (0, 0)),
            pl.BlockSpec((2 * HEAD_DIM, 2 * HEAD_DIM), lambda bb, i: (0, 0)),
        ],
        out_specs=pl.BlockSpec((None, tp, npre), lambda bb, i: (bb, i, 0)),
        out_shape=jax.ShapeDtypeStruct(p.shape, p.dtype),
        input_output_aliases={0: 0},
        compiler_params=_params(("arbitrary", "arbitrary")),
        name="prep",
    )(p, cos, sin_s, with_swapped(qn), with_swapped(kn), swap)


VT_ROWS = HEAD_DIM + 16
SOFTMAX_ROWS = 128
STEPS_PER_TRIP = 2


def _ones_row_block(width):
    first = lax.broadcasted_iota(jnp.int32, (VT_ROWS - HEAD_DIM, width), 0) == 0
    return jnp.where(first, 1.0, 0.0).astype(BF16)


def _mixa_kernel(sink_ref, q_ref, z_ref, k_ref, v_ref, o_ref, vt_ref, s_ref, p_ref, m_ref, e_ref,
                 *, tq, nsub, ctx_len, s_lat):
    kv = pl.program_id(1)
    blk = pl.program_id(2)
    wl = tq + 2 * WINDOW
    nkeys = ctx_len + wl

    @pl.when(blk == 0)
    def _():
        def tbody(c, carry):
            r0 = pl.multiple_of(c * HEAD_DIM, HEAD_DIM)
            vt_ref[c, 0:HEAD_DIM, :] = v_ref[pl.ds(r0, HEAD_DIM), :].astype(F32).T.astype(BF16)
            vt_ref[c, HEAD_DIM:VT_ROWS, :] = _ones_row_block(HEAD_DIM)
            return carry

        lax.fori_loop(0, (ctx_len + s_lat) // HEAD_DIM, tbody, 0)

    def geometry(u):
        q0 = (blk * nsub + u) * tq - ctx_len
        cs = pl.multiple_of(jnp.clip(q0 - WINDOW, 0, s_lat - wl), HEAD_DIM)
        return q0, cs

    def q_rows(u):
        return pl.ds(pl.multiple_of(u * tq, tq), tq)

    def head_cols(g):
        return slice(g * HEAD_DIM, (g + 1) * HEAD_DIM)

    def scores(u, buf):
        q0, cs = geometry(u)
        keys = jnp.concatenate([k_ref[0:ctx_len, :], k_ref[pl.ds(ctx_len + cs, wl), :]], axis=0)
        ss = [lax.dot_general(keys, q_ref[q_rows(u), head_cols(g)], NT_DIMS, preferred_element_type=F32)
              for g in range(GROUP)]
        ms = []
        for g in range(GROUP):
            s_ref[buf, g, 0:ctx_len, :] = ss[g][0:ctx_len]
            ms.append(jnp.maximum(jnp.max(ss[g][0:ctx_len], axis=0, keepdims=True), sink_ref[kv * GROUP + g]))
        row = lax.broadcasted_iota(jnp.int32, (SOFTMAX_ROWS, tq), 0)
        col = lax.broadcasted_iota(jnp.int32, (SOFTMAX_ROWS, tq), 1)
        diff = row - col
        for r in range(0, wl, SOFTMAX_ROWS):
            rel = diff + (cs + r - q0)
            valid = jnp.logical_and(jnp.abs(rel) <= WINDOW, q0 >= 0)
            for g in range(GROUP):
                piece = jnp.where(valid, ss[g][ctx_len + r:ctx_len + r + SOFTMAX_ROWS], -jnp.inf)
                s_ref[buf, g, ctx_len + r:ctx_len + r + SOFTMAX_ROWS, :] = piece
                ms[g] = jnp.maximum(ms[g], jnp.max(piece, axis=0, keepdims=True))
        for g in range(GROUP):
            m_ref[buf, g] = ms[g]

    def softmax(buf):
        for g in range(GROUP):
            m = m_ref[buf, g]
            e_ref[buf, g] = jnp.exp2(sink_ref[kv * GROUP + g] - m)
            for r in range(0, nkeys, SOFTMAX_ROWS):
                p_ref[buf, g, r:r + SOFTMAX_ROWS, :] = jnp.exp2(
                    s_ref[buf, g, r:r + SOFTMAX_ROWS, :] - m).astype(BF16)

    def output(u, buf):
        _, cs = geometry(u)
        c0 = (ctx_len + cs) // HEAD_DIM
        tiles = [vt_ref[c] for c in range(ctx_len // HEAD_DIM)] + [vt_ref[c0 + c] for c in range(wl // HEAD_DIM)]
        vt = jnp.concatenate(tiles, axis=1)
        for g in range(GROUP):
            pv = jnp.dot(vt, p_ref[buf, g], preferred_element_type=F32)
            den = pv[HEAD_DIM:HEAD_DIM + 1] + e_ref[buf, g]
            o = (pv[0:HEAD_DIM] / den).T
            z = z_ref[q_rows(u), head_cols(g)].astype(F32)
            o_ref[q_rows(u), head_cols(g)] = (o * _silu(z)).astype(o_ref.dtype)

    def step(u, par):
        softmax(par)
        scores(u + 1, 1 - par)
        output(u - 1, 1 - par)

    scores(0, 0)
    scores(1, 1)
    softmax(0)
    step(1, 1)
    trips = (nsub - 3) // STEPS_PER_TRIP

    def body(j, carry):
        u = 2 + STEPS_PER_TRIP * j
        for k in range(STEPS_PER_TRIP):
            step(u + k, k % 2)
        return carry

    lax.fori_loop(0, trips, body, 0)
    for u in range(2 + trips * STEPS_PER_TRIP, nsub - 1):
        step(u, u % 2)
    last = nsub - 1
    softmax(last % 2)
    output(last - 1, (last - 1) % 2)
    output(last, last % 2)


def _mixa(sink2, p, *, lay, ctx_len):
    b, t, _ = p.shape
    tq = 256
    nsub = 11 if t % (11 * tq) == 0 else 3
    gw = GROUP * HEAD_DIM
    nkv = lay["akv"]
    s_lat = t - ctx_len
    wl = tq + 2 * WINDOW
    tb = tq * nsub
    assert t % tb == 0 and ctx_len % tq == 0 and s_lat >= wl and WINDOW == HEAD_DIM and nsub >= 3
    qoff, zoff, koff, voff = lay["qa"] // GROUP, lay["za"] // GROUP, lay["ka"], lay["va"]
    return pl.pallas_call(
        functools.partial(_mixa_kernel, tq=tq, nsub=nsub, ctx_len=ctx_len, s_lat=s_lat),
        grid=(b, nkv, t // tb),
        in_specs=[
            pl.BlockSpec(memory_space=pltpu.SMEM),
            pl.BlockSpec((None, tb, gw), lambda bb, k, i: (bb, i, qoff + k)),
            pl.BlockSpec((None, tb, gw), lambda bb, k, i: (bb, i, zoff + k)),
            pl.BlockSpec((None, t, HEAD_DIM), lambda bb, k, i: (bb, 0, koff + k)),
            pl.BlockSpec((None, t, HEAD_DIM), lambda bb, k, i: (bb, 0, voff + k)),
        ],
        out_specs=pl.BlockSpec((None, tb, gw), lambda bb, k, i: (bb, i, k)),
        out_shape=jax.ShapeDtypeStruct((b, t, lay["nh"] * HEAD_DIM), BF16),
        scratch_shapes=[pltpu.VMEM((t // HEAD_DIM, VT_ROWS, HEAD_DIM), BF16),
                        pltpu.VMEM((2, GROUP, ctx_len + wl, tq), F32),
                        pltpu.VMEM((2, GROUP, ctx_len + wl, tq), BF16),
                        pltpu.VMEM((2, GROUP, 1, tq), F32), pltpu.VMEM((2, GROUP, 1, tq), F32)],
        compiler_params=_params(("arbitrary", "arbitrary", "arbitrary")),
        name="mixa",
    )(sink2, p, p, p, p)


def _mixc_kernel(q_ref, z_ref, k_ref, v_ref, oin_ref, o_ref,
                 vt_ref, qt_ref, s_ref, p_ref, a_ref, mc_ref, m_ref, acc_ref, *, tq, tk, ctx_len, nchunk, kvs):
    del oin_ref
    qi = pl.program_id(2)
    units = [(kvi, kvi * GROUP + g) for kvi in range(kvs) for g in range(GROUP)]

    def kv_cols(kvi):
        return slice(kvi * HEAD_DIM, (kvi + 1) * HEAD_DIM)

    @pl.when(qi == 0)
    def _():
        def tbody(c, carry):
            r0 = pl.multiple_of(c * tk, tk)
            for kvi in range(kvs):
                vt_ref[kvi, c, 0:HEAD_DIM, :] = v_ref[pl.ds(r0, tk), kv_cols(kvi)].astype(F32).T.astype(BF16)
                vt_ref[kvi, c, HEAD_DIM:VT_ROWS, :] = _ones_row_block(tk)
            return carry

        lax.fori_loop(0, nchunk, tbody, 0)

    for _, u in units:
        qt_ref[u] = q_ref[:, u * HEAD_DIM:(u + 1) * HEAD_DIM].astype(F32).T.astype(BF16)

    def scores(c, buf, units=units):
        r0 = pl.multiple_of(c * tk, tk)
        for kvi, u in units:
            s = jnp.dot(k_ref[pl.ds(r0, tk), kv_cols(kvi)], qt_ref[u], preferred_element_type=F32)
            s_ref[buf, u] = s
            mc_ref[buf, u] = jnp.max(s, axis=0, keepdims=True)

    def softmax(buf, first):
        for _, u in units:
            m_cur = mc_ref[buf, u]
            if first:
                m_new = m_cur
            else:
                m_prev = m_ref[u]
                m_new = jnp.maximum(m_prev, m_cur)
                a_ref[buf, u] = jnp.exp2(m_prev - m_new)
            m_ref[u] = m_new
            for r in range(0, tk, SOFTMAX_ROWS):
                p_ref[buf, u, r:r + SOFTMAX_ROWS, :] = jnp.exp2(
                    s_ref[buf, u, r:r + SOFTMAX_ROWS, :] - m_new).astype(BF16)

    def output(c, buf, first, units=units):
        for kvi, u in units:
            pv = jnp.dot(vt_ref[kvi, c], p_ref[buf, u], preferred_element_type=F32)
            acc_ref[u] = pv if first else a_ref[buf, u] * acc_ref[u] + pv

    nctx = ctx_len // tq

    @pl.when(qi < nctx)
    def _():
        for kvi, u in units:
            s = jnp.dot(k_ref[0:ctx_len, kv_cols(kvi)], qt_ref[u], preferred_element_type=F32)
            p = jnp.exp2(s - jnp.max(s, axis=0, keepdims=True))
            acc_ref[u] = jnp.dot(vt_ref[kvi, 0, :, 0:ctx_len], p.astype(BF16), preferred_element_type=F32)

    @pl.when(qi >= nctx)
    def _():
        def step(i, par, first=False):
            softmax(par, False)
            for lo in range(0, len(units), GROUP):
                scores(i + 1, 1 - par, units[lo:lo + GROUP])
                output(i - 1, 1 - par, first, units[lo:lo + GROUP])

        scores(0, 0)
        scores(1, 1)
        softmax(0, True)
        mid = nchunk - 2
        if mid >= 1:
            step(1, 1, first=True)
        rest = max(mid - 1, 0)
        trips = rest // STEPS_PER_TRIP

        def body(j, carry):
            i = 2 + STEPS_PER_TRIP * j
            for k in range(STEPS_PER_TRIP):
                step(i + k, k % 2)
            return carry

        lax.fori_loop(0, trips, body, 0)
        for i in range(2 + trips * STEPS_PER_TRIP, nchunk - 1):
            step(i, i % 2)
        last = nchunk - 1
        softmax(last % 2, False)
        output(last - 1, (last - 1) % 2, last == 1)
        output(last, last % 2, False)

    o = jnp.concatenate([(acc_ref[u, 0:HEAD_DIM, :] / acc_ref[u, HEAD_DIM:HEAD_DIM + 1, :]).T
                         for _, u in units], axis=1)
    o_ref[...] = (o * _silu(z_ref[...].astype(F32))).astype(o_ref.dtype)


def _mixc(p, o_prev, *, lay, ctx_len):
    b, t, _ = p.shape
    tq, tk = 256, 768
    nkv = lay["ckv"]
    kvs = 2 if nkv % 2 == 0 else 1
    nu = kvs * GROUP
    gw = nu * HEAD_DIM
    kw = kvs * HEAD_DIM
    assert t % tq == 0 and ctx_len % tq == 0 and t % tk == 0 and ctx_len <= tk
    assert all(lay[name] % nu == 0 for name in ("qc", "zc", "oc")) and lay["kc"] % kvs == 0 and lay["vc"] % kvs == 0
    qoff, zoff, ooff = lay["qc"] // nu, lay["zc"] // nu, lay["oc"] // nu
    koff, voff = lay["kc"] // kvs, lay["vc"] // kvs
    return pl.pallas_call(
        functools.partial(_mixc_kernel, tq=tq, tk=tk, ctx_len=ctx_len, nchunk=t // tk, kvs=kvs),
        grid=(b, nkv // kvs, t // tq),
        in_specs=[
            pl.BlockSpec((None, tq, gw), lambda bb, k, i: (bb, i, qoff + k)),
            pl.BlockSpec((None, tq, gw), lambda bb, k, i: (bb, i, zoff + k)),
            pl.BlockSpec((None, t, kw), lambda bb, k, i: (bb, 0, koff + k)),
            pl.BlockSpec((None, t, kw), lambda bb, k, i: (bb, 0, voff + k)),
            pl.BlockSpec(memory_space=pl.ANY),
        ],
        out_specs=pl.BlockSpec((None, tq, gw), lambda bb, k, i: (bb, i, ooff + k)),
        out_shape=jax.ShapeDtypeStruct(o_prev.shape, o_prev.dtype),
        input_output_aliases={4: 0},
        scratch_shapes=[pltpu.VMEM((kvs, t // tk, VT_ROWS, tk), BF16), pltpu.VMEM((nu, HEAD_DIM, tq), BF16),
                        pltpu.VMEM((2, nu, tk, tq), F32), pltpu.VMEM((2, nu, tk, tq), BF16),
                        pltpu.VMEM((2, nu, 1, tq), F32), pltpu.VMEM((2, nu, 1, tq), F32),
                        pltpu.VMEM((nu, 1, tq), F32),
                        pltpu.VMEM((nu, VT_ROWS, tq), F32)],
        compiler_params=_params(("arbitrary", "arbitrary", "arbitrary")),
        name="mixc",
    )(p, p, p, p, o_prev)


def _ret_state_kernel(gc_ref, kf_ref, vf_ref, kb_ref, vb_ref, wk_ref, of_ref, ob_ref, sf_ref, sb_ref, *, nh):
    step = pl.program_id(1)

    @pl.when(step == 0)
    def _():
        sf_ref[...] = jnp.zeros(sf_ref.shape, F32)
        sb_ref[...] = jnp.zeros(sb_ref.shape, F32)

    heads = [slice(h * HEAD_DIM, (h + 1) * HEAD_DIM) for h in range(nh)]
    for d, (k_ref, v_ref, o_ref, s_ref, order) in enumerate(
            ((kf_ref, vf_ref, of_ref, sf_ref, (0, 1)), (kb_ref, vb_ref, ob_ref, sb_ref, (1, 0)))):
        wk = wk_ref[d]
        for ci in order:
            rs = slice(ci * RET_CHUNK, (ci + 1) * RET_CHUNK)
            k = (k_ref[rs, :].astype(F32) * wk).astype(BF16)
            v = v_ref[rs, :]
            o_ref[ci] = s_ref[...].astype(o_ref.dtype)
            kvs = [lax.dot_general(k[:, hs], v[:, hs], TN_DIMS, preferred_element_type=F32) for hs in heads]
            for h, hs in enumerate(heads):
                s_ref[hs, :] = gc_ref[d, h] * s_ref[hs, :] + kvs[h]


def _ret_states(gc, p, wk, *, lay):
    b, t, _ = p.shape
    nh = lay["bh"]
    bw = nh * HEAD_DIM
    tb = 2 * RET_CHUNK
    nblk = t // tb
    koff, voff = lay["kb"] // nh, lay["vb"] // nh
    assert t % tb == 0

    def bwd_blk(s):
        return jnp.where(s == 0, 0, nblk - s)

    state = jax.ShapeDtypeStruct((b, t // RET_CHUNK, bw, HEAD_DIM), BF16)
    return pl.pallas_call(
        functools.partial(_ret_state_kernel, nh=nh),
        grid=(b, nblk),
        in_specs=[
            pl.BlockSpec(memory_space=pltpu.SMEM),
            pl.BlockSpec((None, tb, bw), lambda bb, s: (bb, s, koff)),
            pl.BlockSpec((None, tb, bw), lambda bb, s: (bb, s, voff)),
            pl.BlockSpec((None, tb, bw), lambda bb, s: (bb, bwd_blk(s), koff)),
            pl.BlockSpec((None, tb, bw), lambda bb, s: (bb, bwd_blk(s), voff)),
            pl.BlockSpec((2, RET_CHUNK, bw), lambda bb, s: (0, 0, 0)),
        ],
        out_specs=[pl.BlockSpec((None, 2, bw, HEAD_DIM), lambda bb, s: (bb, s, 0, 0)),
                   pl.BlockSpec((None, 2, bw, HEAD_DIM), lambda bb, s: (bb, bwd_blk(s), 0, 0))],
        out_shape=[state, state],
        scratch_shapes=[pltpu.VMEM((bw, HEAD_DIM), F32), pltpu.VMEM((bw, HEAD_DIM), F32)],
        compiler_params=_params(("arbitrary", "arbitrary")),
        name="ret_states",
    )(gc, p, p, p, p, wk)


def _ret_out_kernel(q_ref, k_ref, v_ref, z_ref, spf_ref, spb_ref, dm_ref, wq_ref, oin_ref, o_ref, *, nh, nc):
    del oin_ref
    heads = [slice(h * HEAD_DIM, (h + 1) * HEAD_DIM) for h in range(nh)]
    for c in range(nc):
        rs = slice(c * RET_CHUNK, (c + 1) * RET_CHUNK)
        qs = [q_ref[rs, hs] for hs in heads]
        intra = [lax.dot_general(qs[h], k_ref[rs, heads[h]], NT_DIMS, preferred_element_type=F32) * dm_ref[h]
                 for h in range(nh)]
        outs = []
        for h, hs in enumerate(heads):
            q32 = qs[h].astype(F32)
            lhs = jnp.concatenate([intra[h].astype(BF16),
                                   (q32 * wq_ref[0, :, hs]).astype(BF16),
                                   (q32 * wq_ref[1, :, hs]).astype(BF16)], axis=1)
            rhs = jnp.concatenate([v_ref[rs, hs], spf_ref[c, hs, :], spb_ref[c, hs, :]], axis=0)
            outs.append(jnp.dot(lhs, rhs, preferred_element_type=F32))
        for h, hs in enumerate(heads):
            o = outs[h]
            mu = jnp.mean(o, axis=-1, keepdims=True)
            oc = o - mu
            var = jnp.mean(oc * oc, axis=-1, keepdims=True)
            y = oc * lax.rsqrt(var + EPS)
            o_ref[rs, hs] = (y * _silu(z_ref[rs, hs].astype(F32))).astype(o_ref.dtype)


def _ret_out(p, sprev_f, sprev_b, dmat, wq, o_prev, *, lay):
    b, t, _ = p.shape
    nh = lay["bh"]
    bw = nh * HEAD_DIM
    nc = 6 if t % (6 * RET_CHUNK) == 0 else 2
    tb = nc * RET_CHUNK
    assert t % tb == 0
    qoff, koff, voff, zoff = lay["qb"] // nh, lay["kb"] // nh, lay["vb"] // nh, lay["zb"] // nh
    ooff = lay["ob"] // nh
    return pl.pallas_call(
        functools.partial(_ret_out_kernel, nh=nh, nc=nc),
        grid=(b, t // tb),
        in_specs=[
            pl.BlockSpec((None, tb, bw), lambda bb, i: (bb, i, qoff)),
            pl.BlockSpec((None, tb, bw), lambda bb, i: (bb, i, koff)),
            pl.BlockSpec((None, tb, bw), lambda bb, i: (bb, i, voff)),
            pl.BlockSpec((None, tb, bw), lambda bb, i: (bb, i, zoff)),
            pl.BlockSpec((None, nc, bw, HEAD_DIM), lambda bb, i: (bb, i, 0, 0)),
            pl.BlockSpec((None, nc, bw, HEAD_DIM), lambda bb, i: (bb, i, 0, 0)),
            pl.BlockSpec((nh, RET_CHUNK, RET_CHUNK), lambda bb, i: (0, 0, 0)),
            pl.BlockSpec((2, RET_CHUNK, bw), lambda bb, i: (0, 0, 0)),
            pl.BlockSpec(memory_space=pl.ANY),
        ],
        out_specs=pl.BlockSpec((None, tb, bw), lambda bb, i: (bb, i, ooff)),
        out_shape=jax.ShapeDtypeStruct(o_prev.shape, o_prev.dtype),
        input_output_aliases={8: 0},
        compiler_params=_params(("arbitrary", "arbitrary")),
        name="ret_out",
    )(p, p, p, p, sprev_f, sprev_b, dmat, wq, o_prev)


def _outproj_kernel(o_ref, w_ref, x_ref, gl_ref, gc_ref, gp_ref, y_ref, *, tm, rc, first_lat, segs):
    i = pl.program_id(1)
    o = jnp.concatenate([o_ref[:, a:b] for a, b in segs], axis=1)
    y_ref[...] = jnp.dot(o, w_ref[...], preferred_element_type=F32)
    gain = jnp.where(i < first_lat, gc_ref[...], gl_ref[...]) * gp_ref[...]

    def body(c, carry):
        r0 = pl.multiple_of(c * rc, rc)
        y = y_ref[pl.ds(r0, rc), :]
        ms = jnp.mean(y * y, axis=-1, keepdims=True)
        y_ref[pl.ds(r0, rc), :] = x_ref[pl.ds(r0, rc), :] + (y * lax.rsqrt(ms + EPS)) * gain
        return carry

    lax.fori_loop(0, tm // rc, body, 0)


def _outproj(o, w, layer, xs, gate_lat, gate_ctx, g_post, *, ctx_len, segs, latent_only):
    b, t, d = xs.shape
    kdim = o.shape[-1]
    tm, rc = 256, 64
    assert t % tm == 0 and ctx_len % tm == 0 and w.shape[1] == kdim
    nctx = ctx_len // tm
    off = nctx if latent_only else 0
    first_lat = 0 if latent_only else nctx
    t_out = t - off * tm
    return pl.pallas_call(
        functools.partial(_outproj_kernel, tm=tm, rc=rc, first_lat=first_lat, segs=segs),
        grid=(b, t_out // tm),
        in_specs=[
            pl.BlockSpec((None, tm, kdim), lambda bb, i: (bb, i + off, 0)),
            pl.BlockSpec((None, kdim, d), lambda bb, i: (layer, 0, 0), pipeline_mode=pl.Buffered(1)),
            pl.BlockSpec((None, tm, d), lambda bb, i: (bb, i + off, 0)),
            pl.BlockSpec((None, 1, d), lambda bb, i: (bb, 0, 0)),
            pl.BlockSpec((1, d), lambda bb, i: (0, 0)),
            pl.BlockSpec((1, d), lambda bb, i: (0, 0)),
        ],
        out_specs=pl.BlockSpec((None, tm, d), lambda bb, i: (bb, i, 0)),
        out_shape=jax.ShapeDtypeStruct((b, t_out, d), xs.dtype),
        compiler_params=pltpu.CompilerParams(dimension_semantics=("arbitrary", "arbitrary"),
                                             vmem_limit_bytes=OUTPROJ_VMEM_LIMIT),
        name="outproj",
    )(o, w, xs, gate_lat, gate_ctx, g_post)


def _layout(d_model):
    nh = d_model // HEAD_DIM
    aq = 3 * nh // 8
    akv = aq // GROUP
    bh = nh // 4
    cq = nh - aq - bh
    ckv = cq // GROUP
    order = [("qa", aq), ("qc", cq), ("ka", akv), ("kc", ckv), ("qb", bh), ("kb", bh),
             ("za", aq), ("zc", cq), ("zb", bh), ("va", akv), ("vc", ckv), ("vb", bh)]
    lay = {"nh": nh, "aq": aq, "akv": akv, "bh": bh, "cq": cq, "ckv": ckv}
    off = 0
    for name, width in order:
        lay[name] = off
        off += width
    lay["nprep"] = lay["za"]
    lay["oa"], lay["oc"], lay["ob"] = 0, aq, aq + cq
    ref_order = [("qa", aq), ("ka", akv), ("va", akv), ("qb", bh), ("kb", bh), ("vb", bh),
                 ("qc", cq), ("kc", ckv), ("vc", ckv), ("za", aq), ("zb", bh), ("zc", cq)]
    ref_off, off = {}, 0
    for name, width in ref_order:
        ref_off[name] = off
        off += width
    lay["o_segs"] = ((0, aq * HEAD_DIM), ((aq + cq) * HEAD_DIM, nh * HEAD_DIM),
                     (aq * HEAD_DIM, (aq + cq) * HEAD_DIM))
    tu = math.gcd(4, *[width for _, width in order])
    lay["tiles"] = {"tn": tu * HEAD_DIM,
                    "src": tuple((lay[name] // tu, ref_off[name] // tu) for name, _ in order)}
    scale = HEAD_DIM ** -0.5
    lay["kinds"] = tuple(
        [(None, scale * LOG2E)] * aq + [("q", scale * LOG2E)] * cq + [(None, 1.0)] * akv
        + [("k", 1.0)] * ckv + [(None, 1.0)] * bh + [(None, scale)] * bh)
    return lay


def _rope_tables(n_lat, ctx_len):
    rows = n_lat // GRID_W
    row = jnp.repeat(jnp.arange(rows, dtype=F32), GRID_W)
    col = jnp.tile(jnp.arange(GRID_W, dtype=F32), rows)
    axis_dim = HEAD_DIM // 2
    inv = ROPE_BASE ** (-jnp.arange(0, axis_dim, 2, dtype=F32) / axis_dim)
    ang_r = row[:, None] * inv
    ang_c = col[:, None] * inv
    ang = jnp.concatenate([ang_r, ang_r, ang_c, ang_c], axis=-1)
    sign = jnp.where((jnp.arange(HEAD_DIM) // 32) % 2 == 1, 1.0, -1.0).astype(F32)
    cos = jnp.concatenate([jnp.ones((ctx_len, HEAD_DIM), F32), jnp.cos(ang)], axis=0)
    sin_s = jnp.concatenate([jnp.zeros((ctx_len, HEAD_DIM), F32), jnp.sin(ang) * sign], axis=0)
    return cos, sin_s


def _decay_tables(ret_decay_l):
    log_g = jax.nn.log_sigmoid(ret_decay_l.astype(F32))
    nh = log_g.shape[1]
    pos = jnp.arange(RET_CHUNK, dtype=F32)
    lane = lambda w: jnp.repeat(w, HEAD_DIM, axis=-1)
    wk_f = jnp.exp((RET_CHUNK - 1 - pos)[:, None] * log_g[0][None])
    wk_b = jnp.exp(pos[:, None] * log_g[1][None])
    wq_f = jnp.exp((pos + 1.0)[:, None] * log_g[0][None])
    wq_b = jnp.exp((RET_CHUNK - pos)[:, None] * log_g[1][None])
    wk = jnp.stack([lane(wk_f), lane(wk_b)])
    wq = jnp.stack([lane(wq_f), lane(wq_b)])
    gc = jnp.exp(RET_CHUNK * log_g)
    diff = pos[:, None] - pos[None, :]
    ef = jnp.where((diff >= 0)[None], diff[None] * log_g[0][:, None, None], -jnp.inf)
    eb = jnp.where((diff <= 0)[None], -diff[None] * log_g[1][:, None, None], -jnp.inf)
    dmat = jnp.exp(ef) + jnp.exp(eb)
    del nh
    return wk, wq, gc, dmat


def kernel(x, c, ctx, c_ctx, w_ada, b_ada, g_pre, g_post, w_in, w_out, sink_a, qnorm_c, knorm_c, ret_decay):
    bsz, n_lat, d = x.shape
    ctx_len = ctx.shape[1]
    depth = w_ada.shape[0]
    lay = _layout(d)

    cond_rows = jnp.zeros((8, d), F32).at[0:bsz].set(c).at[bsz].set(c_ctx)
    mods = _ada(cond_rows, w_ada, b_ada)
    cos, sin_s = _rope_tables(n_lat, ctx_len)
    w_in_p = w_in.astype(BF16)
    w_out_p = w_out.astype(BF16)

    xs = None
    for l in range(depth):
        mod_lat = mods[l, 0:bsz].reshape(bsz, 1, 3 * d)
        mod_ctx = mods[l, bsz:bsz + 1]
        if l == 0:
            xs, h = _prenorm_first(ctx, x, g_pre[l].reshape(1, d), mod_lat, mod_ctx)
        else:
            h = _prenorm(xs, g_pre[l].reshape(1, d), mod_lat, mod_ctx, ctx_len=ctx_len)
        p = _inproj(h, w_in_p, l, lay["tiles"])
        p = _prep(p, cos, sin_s, qnorm_c[l], knorm_c[l], kinds=lay["kinds"])
        o = _mixa(sink_a[l] * LOG2E, p, lay=lay, ctx_len=ctx_len)
        o = _mixc(p, o, lay=lay, ctx_len=ctx_len)
        wk, wq, gc, dmat = _decay_tables(ret_decay[l])
        sprev_f, sprev_b = _ret_states(gc, p, wk, lay=lay)
        o = _ret_out(p, sprev_f, sprev_b, dmat, wq, o, lay=lay)
        xs = _outproj(o, w_out_p, l, xs, mod_lat[:, :, 2 * d:], mod_ctx[:, 2 * d:],
                      g_post[l].reshape(1, d), ctx_len=ctx_len, segs=lay["o_segs"],
                      latent_only=(l == depth - 1))
    return xs
```
